```python
import jax
import jax.numpy as jnp
from jax import lax
import numpy as np

D_MODEL = 2048
BATCH = 2
SEQ = 16384
DEPTH = 4

HEAD_DIM = 128
ROPE_THETA = 10000.0
LN_EPS = 1e-5
Q_BLOCK = 128

NSA_HEADS = D_MODEL // (2 * HEAD_DIM)
NSA_KV_HEADS = 2
CMP_BLOCK = 32
CMP_STRIDE = 16
SEL_BLOCK = 64
N_SELECT = 16
NSA_WINDOW = 512
NSA_Q_BLOCK = 64

SWA_HEADS = D_MODEL // (2 * HEAD_DIM)
SWA_KV_HEADS = 2
SWA_WINDOW = 128

DIL_HEADS = D_MODEL // HEAD_DIM
DIL_PATTERNS = ((128, 1), (512, 4), (2048, 16))

D_FF = 4 * D_MODEL
DN_ALPHA = (2 * DEPTH) ** 0.25
DN_BETA = (8 * DEPTH) ** -0.25
N_EVEN = (DEPTH + 1) // 2
N_ODD = DEPTH // 2

NSA_Q = NSA_HEADS * HEAD_DIM
NSA_KV = NSA_KV_HEADS * HEAD_DIM
SWA_Q = SWA_HEADS * HEAD_DIM
SWA_KV = SWA_KV_HEADS * HEAD_DIM
EVEN_WIDTHS = (NSA_Q, 3 * NSA_KV, 3 * NSA_KV, 3 * NSA_HEADS, SWA_Q, SWA_KV, SWA_KV)
EVEN_IN = sum(EVEN_WIDTHS)
EVEN_OUT = NSA_Q + SWA_Q
DIL_QKV = 3 * DIL_HEADS * HEAD_DIM
ODD_IN = len(DIL_PATTERNS) * DIL_QKV
ODD_OUT = DIL_HEADS * HEAD_DIM

kernel_name = 'hybrid_nsa_swa_dilated_deepnorm'


def _layer_norm(x, g, b):
    xf = x.astype(jnp.float32)
    mu = jnp.mean(xf, axis=-1, keepdims=True)
    var = jnp.mean(jnp.square(xf - mu), axis=-1, keepdims=True)
    y = (xf - mu) * lax.rsqrt(var + LN_EPS) * g.astype(jnp.float32) + b.astype(jnp.float32)
    return y.astype(x.dtype)


def _rope_tables(positions):
    inv_freq = ROPE_THETA ** (-jnp.arange(0, HEAD_DIM, 2, dtype=jnp.float32) / HEAD_DIM)
    ang = positions.astype(jnp.float32)[..., None] * inv_freq
    return jnp.cos(ang)[:, :, None, :], jnp.sin(ang)[:, :, None, :]


def _rope(t, cos, sin):
    tf = t.astype(jnp.float32)
    t1, t2 = jnp.split(tf, 2, axis=-1)
    return jnp.concatenate([t1 * cos - t2 * sin, t2 * cos + t1 * sin], axis=-1).astype(t.dtype)


def _banded_attention(q, k, v, window, sinks=None):
    B, Hq, L, dh = q.shape
    Hkv = k.shape[1]
    grp = Hq // Hkv
    blk = Q_BLOCK
    nb = -(-L // blk)
    Lp = nb * blk
    span = window - 1 + blk
    qp = jnp.pad(q, ((0, 0), (0, 0), (0, Lp - L), (0, 0))).reshape(B, Hkv, grp, Lp, dh)
    kp = jnp.pad(k, ((0, 0), (0, 0), (window - 1, Lp - L), (0, 0)))
    vp = jnp.pad(v, ((0, 0), (0, 0), (window - 1, Lp - L), (0, 0)))
    rel = jnp.arange(span)[None, :] - jnp.arange(blk)[:, None] - (window - 1)
    band = (rel <= 0) & (rel > -window)
    sink = None if sinks is None else sinks.astype(jnp.float32).reshape(1, Hkv, grp, 1)
    scale = dh ** -0.5

    def step(b):
        start = b * blk
        qb = lax.dynamic_slice_in_dim(qp, start, blk, axis=3)
        kb = lax.dynamic_slice_in_dim(kp, start, span, axis=2)
        vb = lax.dynamic_slice_in_dim(vp, start, span, axis=2)
        kpos = start - (window - 1) + jnp.arange(span)
        mask = band & (kpos >= 0)[None, :]
        s = jnp.einsum('bkgqd,bksd->bkgqs', qb, kb).astype(jnp.float32) * scale
        s = jnp.where(mask, s, -jnp.inf)
        m = jnp.max(s, axis=-1)
        if sink is not None:
            m = jnp.maximum(m, sink)
        p = jnp.exp(s - m[..., None])
        l = jnp.sum(p, axis=-1)
        if sink is not None:
            l = l + jnp.exp(sink - m)
        o = jnp.einsum('bkgqs,bksd->bkgqd', p, vb.astype(jnp.float32)) / l[..., None]
        return o.astype(q.dtype), m + jnp.log(l)

    o, lse = lax.map(step, jnp.arange(nb))
    o = jnp.moveaxis(o, 0, 3).reshape(B, Hq, Lp, dh)[:, :, :L]
    lse = jnp.moveaxis(lse, 0, 3).reshape(B, Hq, Lp)[:, :, :L]
    return o, lse


def _compress(k, pe, w1, w2):
    n_cmp = (k.shape[2] - CMP_BLOCK) // CMP_STRIDE + 1
    idx = np.arange(n_cmp)[:, None] * CMP_STRIDE + np.arange(CMP_BLOCK)[None, :]
    blocks = k[:, :, idx] + pe
    flat = blocks.reshape(blocks.shape[:3] + (CMP_BLOCK * HEAD_DIM,))
    return jax.nn.gelu(flat @ w1) @ w2


def _cmp_to_sel(n_cmp, n_blk):
    pos = np.arange(n_cmp)[:, None] * CMP_STRIDE + np.arange(CMP_BLOCK)[None, :]
    owner = pos // SEL_BLOCK
    m = (owner[:, :, None] == np.arange(n_blk)[None, None, :]).sum(axis=1) / CMP_BLOCK
    return jnp.asarray(m, dtype=jnp.float32)


def _nsa(q, k_cmp, v_cmp, k_slc, v_slc, k_win, v_win, gates, pe_k, pe_v, ck_w1, ck_w2, cv_w1, cv_w2):
    B, H, T, dh = q.shape
    G = k_cmp.shape[1]
    grp = H // G
    kc = _compress(k_cmp, pe_k, ck_w1, ck_w2)
    vc = _compress(v_cmp, pe_v, cv_w1, cv_w2)
    n_cmp = kc.shape[2]
    n_blk = T // SEL_BLOCK
    n_sel = min(N_SELECT, n_blk)
    cmp_end = jnp.arange(n_cmp) * CMP_STRIDE + (CMP_BLOCK - 1)
    cmp_to_sel = _cmp_to_sel(n_cmp, n_blk)
    ks = k_slc.reshape(B, G, n_blk, SEL_BLOCK, dh)
    vs = v_slc.reshape(B, G, n_blk, SEL_BLOCK, dh)
    q5 = q.reshape(B, G, grp, T, dh)
    bi = jnp.arange(B)[:, None, None, None]
    gi = jnp.arange(G)[None, :, None, None]
    blk_id = jnp.arange(n_blk)
    in_blk = jnp.arange(SEL_BLOCK)
    scale = dh ** -0.5

    def step(b):
        q0 = b * NSA_Q_BLOCK
        qb = lax.dynamic_slice_in_dim(q5, q0, NSA_Q_BLOCK, axis=3)
        qpos = q0 + jnp.arange(NSA_Q_BLOCK)
        s = jnp.einsum('bghqd,bgcd->bghqc', qb, kc).astype(jnp.float32) * scale
        s = jnp.where(cmp_end[None, :] <= qpos[:, None], s, -jnp.inf)
        m = jnp.max(s, axis=-1, keepdims=True)
        m = jnp.where(jnp.isfinite(m), m, 0.0)
        p = jnp.exp(s - m)
        l = jnp.sum(p, axis=-1, keepdims=True)
        p = p / jnp.where(l > 0, l, 1.0)
        o_cmp = jnp.einsum('bghqc,bgcd->bghqd', p, vc.astype(jnp.float32))
        imp = jnp.einsum('bghqc,cn->bgqn', p, cmp_to_sel)
        cur = qpos // SEL_BLOCK
        valid = blk_id[None, :] * SEL_BLOCK <= qpos[:, None]
        forced = (blk_id[None, :] == 0) | (blk_id[None, :] == cur[:, None]) | (blk_id[None, :] == cur[:, None] - 1)
        score = jnp.where(forced, jnp.inf, jnp.where(valid, imp, -jnp.inf))
        _, idx = lax.top_k(score, n_sel)
        sk = ks[bi, gi, idx]
        sv = vs[bi, gi, idx]
        s2 = jnp.einsum('bghqd,bgqnld->bghqnl', qb, sk).astype(jnp.float32) * scale
        kpos = idx[..., None] * SEL_BLOCK + in_blk
        s2 = jnp.where((kpos <= qpos[None, None, :, None, None])[:, :, None], s2, -jnp.inf)
        p2 = jax.nn.softmax(s2.reshape(s2.shape[:4] + (n_sel * SEL_BLOCK,)), axis=-1).reshape(s2.shape)
        o_slc = jnp.einsum('bghqnl,bgqnld->bghqd', p2, sv.astype(jnp.float32))
        return o_cmp.astype(q.dtype), o_slc.astype(q.dtype)

    o_cmp, o_slc = lax.map(step, jnp.arange(T // NSA_Q_BLOCK))
    o_cmp = jnp.moveaxis(o_cmp, 0, 3).reshape(B, H, T, dh)
    o_slc = jnp.moveaxis(o_slc, 0, 3).reshape(B, H, T, dh)
    o_win, _ = _banded_attention(q, k_win, v_win, NSA_WINDOW)
    return gates[..., 0:1] * o_cmp + gates[..., 1:2] * o_slc + gates[..., 2:3] * o_win


def _even_mixer(x, cos, sin, w_in, w_o, pe_k, pe_v, ck_w1, ck_w2, cv_w1, cv_w2, sinks):
    B, T, _ = x.shape
    dh = HEAD_DIM
    G = NSA_KV_HEADS
    aq, ak, av, ag, bq, bk, bv = jnp.split(x @ w_in, np.cumsum(EVEN_WIDTHS)[:-1].tolist(), axis=-1)
    aq = _rope(aq.reshape(B, T, NSA_HEADS, dh), cos, sin).transpose(0, 2, 1, 3)
    ak = _rope(ak.reshape(B, T, 3 * G, dh), cos, sin).reshape(B, T, 3, G, dh).transpose(2, 0, 3, 1, 4)
    av = av.reshape(B, T, 3, G, dh).transpose(2, 0, 3, 1, 4)
    gates = jax.nn.sigmoid(ag.reshape(B, T, NSA_HEADS, 3).astype(jnp.float32)).transpose(0, 2, 1, 3).astype(x.dtype)
    o_a = _nsa(aq, ak[0], av[0], ak[1], av[1], ak[2], av[2], gates, pe_k, pe_v, ck_w1, ck_w2, cv_w1, cv_w2)
    bq = _rope(bq.reshape(B, T, SWA_HEADS, dh), cos, sin).transpose(0, 2, 1, 3)
    bk = _rope(bk.reshape(B, T, SWA_KV_HEADS, dh), cos, sin).transpose(0, 2, 1, 3)
    bv = bv.reshape(B, T, SWA_KV_HEADS, dh).transpose(0, 2, 1, 3)
    o_b, _ = _banded_attention(bq, bk, bv, SWA_WINDOW, sinks)
    o = jnp.concatenate([o_a, o_b], axis=1)
    return o.transpose(0, 2, 1, 3).reshape(B, T, EVEN_OUT) @ w_o


def _odd_mixer(x, cos, sin, w_in, w_o):
    B, T, _ = x.shape
    H, dh = DIL_HEADS, HEAD_DIM
    w_groups = w_in.reshape(w_in.shape[0], len(DIL_PATTERNS), DIL_QKV)
    outs, lses = [], []
    for g, (window, dil) in enumerate(DIL_PATTERNS):
        qkv = (x @ w_groups[:, g]).reshape(B, T, 3, H, dh)
        q = _rope(qkv[:, :, 0], cos, sin)
        k = _rope(qkv[:, :, 1], cos, sin)
        v = qkv[:, :, 2]
        u = T // dil
        dec = lambda t: t.reshape(B, u, dil, H, dh).transpose(0, 3, 2, 1, 4).reshape(B, H * dil, u, dh)
        o, lse = _banded_attention(dec(q), dec(k), dec(v), window // dil + 1)
        outs.append(o.reshape(B, H, dil, u, dh).transpose(0, 1, 3, 2, 4).reshape(B, H, T, dh))
        lses.append(lse.reshape(B, H, dil, u).transpose(0, 1, 3, 2).reshape(B, H, T))
    mix = jax.nn.softmax(jnp.stack(lses), axis=0)
    o = jnp.einsum('gbht,gbhtd->bhtd', mix, jnp.stack(outs).astype(jnp.float32)).astype(x.dtype)
    return o.transpose(0, 2, 1, 3).reshape(B, T, ODD_OUT) @ w_o


def _mlp(x, w1, w2):
    return jnp.square(jax.nn.relu(x @ w1)) @ w2


def setup_inputs(seed: int = 0) -> dict:
    key = jax.random.key(seed)
    ks = jax.random.split(key, 19)

    def nrm(k, shape, scale):
        return jax.random.normal(k, shape, dtype=jnp.float32) * scale

    beta = DN_BETA
    x = nrm(ks[0], (BATCH, SEQ, D_MODEL), 1.0)
    offset = jax.random.randint(ks[1], (BATCH, 1), 0, 4096, dtype=jnp.int32)
    positions = offset + jnp.arange(SEQ, dtype=jnp.int32)[None, :]
    e_col = np.concatenate([np.ones(NSA_Q), np.ones(3 * NSA_KV), np.full(3 * NSA_KV, beta), np.ones(3 * NSA_HEADS), np.ones(SWA_Q), np.ones(SWA_KV), np.full(SWA_KV, beta)]).astype(np.float32)
    o_col = np.tile(np.concatenate([np.ones(2 * DIL_HEADS * HEAD_DIM), np.full(DIL_HEADS * HEAD_DIM, beta)]), len(DIL_PATTERNS)).astype(np.float32)
    e_w_in = nrm(ks[2], (N_EVEN, D_MODEL, EVEN_IN), D_MODEL ** -0.5) * jnp.asarray(e_col)
    e_w_o = nrm(ks[3], (N_EVEN, EVEN_OUT, D_MODEL), EVEN_OUT ** -0.5 * beta)
    nsa_pe_k = nrm(ks[4], (N_EVEN, CMP_BLOCK, HEAD_DIM), 0.5)
    nsa_pe_v = nrm(ks[5], (N_EVEN, CMP_BLOCK, HEAD_DIM), 0.5)
    nsa_ck_w1 = nrm(ks[6], (N_EVEN, CMP_BLOCK * HEAD_DIM, HEAD_DIM), (CMP_BLOCK * HEAD_DIM) ** -0.5)
    nsa_ck_w2 = nrm(ks[7], (N_EVEN, HEAD_DIM, HEAD_DIM), HEAD_DIM ** -0.5)
    nsa_cv_w1 = nrm(ks[8], (N_EVEN, CMP_BLOCK * HEAD_DIM, HEAD_DIM), (CMP_BLOCK * HEAD_DIM) ** -0.5)
    nsa_cv_w2 = nrm(ks[9], (N_EVEN, HEAD_DIM, HEAD_DIM), HEAD_DIM ** -0.5)
    swa_sinks = nrm(ks[10], (N_EVEN, SWA_HEADS), 1.0)
    o_w_in = nrm(ks[11], (N_ODD, D_MODEL, ODD_IN), D_MODEL ** -0.5) * jnp.asarray(o_col)
    o_w_o = nrm(ks[12], (N_ODD, ODD_OUT, D_MODEL), ODD_OUT ** -0.5 * beta)
    ln1_g = 1.0 + nrm(ks[13], (DEPTH, D_MODEL), 0.02)
    ln1_b = nrm(ks[14], (DEPTH, D_MODEL), 0.02)
    mlp_w1 = nrm(ks[15], (DEPTH, D_MODEL, D_FF), D_MODEL ** -0.5 * beta)
    mlp_w2 = nrm(ks[16], (DEPTH, D_FF, D_MODEL), D_FF ** -0.5 * beta)
    ln2_g = 1.0 + nrm(ks[17], (DEPTH, D_MODEL), 0.02)
    ln2_b = nrm(ks[18], (DEPTH, D_MODEL), 0.02)
    return {'x': x, 'positions': positions, 'e_w_in': e_w_in, 'e_w_o': e_w_o,
            'nsa_pe_k': nsa_pe_k, 'nsa_pe_v': nsa_pe_v, 'nsa_ck_w1': nsa_ck_w1, 'nsa_ck_w2': nsa_ck_w2,
            'nsa_cv_w1': nsa_cv_w1, 'nsa_cv_w2': nsa_cv_w2, 'swa_sinks': swa_sinks,
            'o_w_in': o_w_in, 'o_w_o': o_w_o, 'ln1_g': ln1_g, 'ln1_b': ln1_b,
            'mlp_w1': mlp_w1, 'mlp_w2': mlp_w2, 'ln2_g': ln2_g, 'ln2_b': ln2_b}


def reference(x, positions, e_w_in, e_w_o, nsa_pe_k, nsa_pe_v, nsa_ck_w1, nsa_ck_w2, nsa_cv_w1, nsa_cv_w2, swa_sinks, o_w_in, o_w_o, ln1_g, ln1_b, mlp_w1, mlp_w2, ln2_g, ln2_b):
    cos, sin = _rope_tables(positions)
    for layer in range(DEPTH):
        i = layer // 2
        if layer % 2 == 0:
            y = _even_mixer(x, cos, sin, e_w_in[i], e_w_o[i], nsa_pe_k[i], nsa_pe_v[i], nsa_ck_w1[i], nsa_ck_w2[i], nsa_cv_w1[i], nsa_cv_w2[i], swa_sinks[i])
        else:
            y = _odd_mixer(x, cos, sin, o_w_in[i], o_w_o[i])
        x = _layer_norm(DN_ALPHA * x + y, ln1_g[layer], ln1_b[layer])
        x = _layer_norm(DN_ALPHA * x + _mlp(x, mlp_w1[layer], mlp_w2[layer]), ln2_g[layer], ln2_b[layer])
    return x
```

```python
import functools
import math

import numpy as np
import jax
import jax.numpy as jnp
from jax import lax
from jax.experimental import pallas as pl
from jax.experimental.pallas import tpu as pltpu

D_MODEL = 2048
DEPTH = 4
HEAD_DIM = 128
ROPE_THETA = 10000.0
LN_EPS = 1e-5

NSA_HEADS = D_MODEL // (2 * HEAD_DIM)
NSA_KV_HEADS = 2
NSA_GROUP = NSA_HEADS // NSA_KV_HEADS
CMP_BLOCK = 32
CMP_STRIDE = 16
SEL_BLOCK = 64
N_SELECT = 16
NSA_WINDOW = 512

SWA_HEADS = D_MODEL // (2 * HEAD_DIM)
SWA_KV_HEADS = 2
SWA_WINDOW = 128

DIL_HEADS = D_MODEL // HEAD_DIM
DIL_PATTERNS = ((128, 1), (512, 4), (2048, 16))

D_FF = 4 * D_MODEL
DN_ALPHA = (2 * DEPTH) ** 0.25

NSA_Q = NSA_HEADS * HEAD_DIM
NSA_KV = NSA_KV_HEADS * HEAD_DIM
SWA_Q = SWA_HEADS * HEAD_DIM
SWA_KV = SWA_KV_HEADS * HEAD_DIM
EVEN_WIDTHS = (NSA_Q, 3 * NSA_KV, 3 * NSA_KV, 3 * NSA_HEADS, SWA_Q, SWA_KV, SWA_KV)
EVEN_PROJ = NSA_Q + SWA_Q + 3 * NSA_KV + SWA_KV + 3 * NSA_KV + SWA_KV
DIL_QKV = 3 * DIL_HEADS * HEAD_DIM

QK_SCALE = HEAD_DIM ** -0.5
NEG = -1e30
VMEM_LIMIT_BYTES = 56 * 1024 * 1024

MODE_PLAIN, MODE_ROPE, MODE_ROPE_SCALED = 0, 1, 2

F32 = jnp.float32
BF16 = jnp.bfloat16


def _params(*semantics):
    return pltpu.CompilerParams(dimension_semantics=semantics, vmem_limit_bytes=VMEM_LIMIT_BYTES)


def _dot(a, b):
    return jnp.dot(a, b, preferred_element_type=F32)


def _dot_nt(a, b):
    return lax.dot_general(a, b, (((1,), (1,)), ((), ())), preferred_element_type=F32)


def _rope_table_kernel(pos_ref, inv_ref, sign_ref, cos_ref, sin_ref):
    ang = pos_ref[...].astype(F32) * inv_ref[...]
    cos_ref[...] = jnp.cos(ang)
    sin_ref[...] = jnp.sin(ang) * sign_ref[...]


def _rope_tables(positions):
    m = positions.size
    tm = min(m, 2048)
    inv = ROPE_THETA ** (-jnp.arange(0, HEAD_DIM, 2, dtype=F32) / HEAD_DIM)
    inv_full = jnp.concatenate([inv, inv]).reshape(1, HEAD_DIM)
    half = HEAD_DIM // 2
    sign = jnp.concatenate([-jnp.ones((half,), F32), jnp.ones((half,), F32)]).reshape(1, HEAD_DIM)
    row = pl.BlockSpec((tm, HEAD_DIM), lambda i: (i, 0))
    const = pl.BlockSpec((1, HEAD_DIM), lambda i: (0, 0))
    return pl.pallas_call(
        _rope_table_kernel,
        grid=(m // tm,),
        in_specs=[pl.BlockSpec((tm, 1), lambda i: (i, 0)), const, const],
        out_specs=[row, row],
        out_shape=[jax.ShapeDtypeStruct((m, HEAD_DIM), F32)] * 2,
        compiler_params=_params("parallel"),
        name="rope_tables",
    )(positions.reshape(m, 1), inv_full, sign)


def _proj_kernel(modes_ref, x_ref, w_ref, cos_ref, sin_ref, o_ref, *, tn):
    mode = modes_ref[pl.program_id(1)]
    acc = _dot(x_ref[...], w_ref[...])

    @pl.when(mode == MODE_PLAIN)
    def _():
        o_ref[...] = acc.astype(o_ref.dtype)

    @pl.when(mode != MODE_PLAIN)
    def _():
        scale = jnp.where(mode == MODE_ROPE_SCALED, QK_SCALE, 1.0).astype(F32)
        cos = cos_ref[...] * scale
        sin = sin_ref[...] * scale
        for blk in range(tn // HEAD_DIM):
            cols = slice(blk * HEAD_DIM, (blk + 1) * HEAD_DIM)
            t = acc[:, cols]
            o_ref[:, cols] = (t * cos + pltpu.roll(t, HEAD_DIM // 2, 1) * sin).astype(o_ref.dtype)


def _project(x, w, modes, cos, sin, *, tm=1024, tn=1024):
    m, k = x.shape
    n = w.shape[1]
    tm = min(tm, m)
    grid_spec = pltpu.PrefetchScalarGridSpec(
        num_scalar_prefetch=1,
        grid=(m // tm, n // tn),
        in_specs=[
            pl.BlockSpec((tm, k), lambda i, j, modes: (i, 0)),
            pl.BlockSpec((k, tn), lambda i, j, modes: (0, j)),
            pl.BlockSpec((tm, HEAD_DIM), lambda i, j, modes: (i, 0)),
            pl.BlockSpec((tm, HEAD_DIM), lambda i, j, modes: (i, 0)),
        ],
        out_specs=pl.BlockSpec((tm, tn), lambda i, j, modes: (i, j)),
    )
    return pl.pallas_call(
        functools.partial(_proj_kernel, tn=tn),
        grid_spec=grid_spec,
        out_shape=jax.ShapeDtypeStruct((m, n), BF16),
        compiler_params=_params("parallel", "arbitrary"),
        name="in_proj",
    )(modes, x, w, cos, sin)


def _gate_kernel(x_ref, w_ref, o_ref):
    o_ref[...] = jax.nn.sigmoid(_dot(x_ref[...], w_ref[...]))


def _gate_project(x, w, *, tm=1024):
    m, k = x.shape
    n = w.shape[1]
    tm = min(tm, m)
    return pl.pallas_call(
        _gate_kernel,
        grid=(m // tm,),
        in_specs=[pl.BlockSpec((tm, k), lambda i: (i, 0)), pl.BlockSpec((k, n), lambda i: (0, 0))],
        out_specs=pl.BlockSpec((tm, n), lambda i: (i, 0)),
        out_shape=jax.ShapeDtypeStruct((m, n), F32),
        compiler_params=_params("parallel"),
        name="gate_proj",
    )(x, w)


def _residual_layer_norm(res, y, g, b):
    z = DN_ALPHA * res + y
    mu = jnp.mean(z, axis=-1, keepdims=True)
    zc = z - mu
    var = jnp.mean(zc * zc, axis=-1, keepdims=True)
    return zc * lax.rsqrt(var + LN_EPS) * g + b


def _out_proj_kernel(*refs, n_in):
    x_refs, w_refs = refs[:n_in], refs[n_in:2 * n_in]
    res_ref, g_ref, b_ref, of_ref, ob_ref = refs[2 * n_in:]
    y = _dot(x_refs[0][...], w_refs[0][...])
    for x_ref, w_ref in zip(x_refs[1:], w_refs[1:]):
        y += _dot(x_ref[...], w_ref[...])
    out = _residual_layer_norm(res_ref[...], y, g_ref[...], b_ref[...])
    of_ref[...] = out
    ob_ref[...] = out.astype(BF16)


def _out_proj_ln(xs, ws, res, g, b, *, tm=512):
    m, d = res.shape
    tm = min(tm, m)
    n_in = len(xs)
    in_specs = [pl.BlockSpec((tm, x.shape[1]), lambda i: (i, 0)) for x in xs]
    in_specs += [pl.BlockSpec(w.shape, lambda i: (0, 0)) for w in ws]
    row = pl.BlockSpec((tm, d), lambda i: (i, 0))
    vec = pl.BlockSpec((1, d), lambda i: (0, 0))
    in_specs += [row, vec, vec]
    return pl.pallas_call(
        functools.partial(_out_proj_kernel, n_in=n_in),
        grid=(m // tm,),
        in_specs=in_specs,
        out_specs=[row, row],
        out_shape=[jax.ShapeDtypeStruct((m, d), F32), jax.ShapeDtypeStruct((m, d), BF16)],
        compiler_params=_params("parallel"),
        name="out_proj_ln",
    )(*xs, *ws, res, g.reshape(1, d), b.reshape(1, d))


def _mlp_kernel(xb_ref, w1_ref, w2_ref, xf_ref, g_ref, b_ref, of_ref, ob_ref, acc_ref):
    f = pl.program_id(1)
    h = _dot(xb_ref[...], w1_ref[...])
    h = jnp.square(jnp.maximum(h, 0.0)).astype(BF16)
    part = _dot(h, w2_ref[...])

    @pl.when(f == 0)
    def _():
        acc_ref[...] = part

    @pl.when(f > 0)
    def _():
        acc_ref[...] += part

    @pl.when(f == pl.num_programs(1) - 1)
    def _():
        out = _residual_layer_norm(xf_ref[...], acc_ref[...], g_ref[...], b_ref[...])
        of_ref[...] = out
        ob_ref[...] = out.astype(BF16)


def _mlp_ln(xf, xb, w1, w2, g, b, *, tm=512, tf=512):
    m, d = xf.shape
    ff = w1.shape[1]
    tm = min(tm, m)
    row = pl.BlockSpec((tm, d), lambda i, f: (i, 0))
    vec = pl.BlockSpec((1, d), lambda i, f: (0, 0))
    return pl.pallas_call(
        _mlp_kernel,
        grid=(m // tm, ff // tf),
        in_specs=[row, pl.BlockSpec((d, tf), lambda i, f: (0, f)), pl.BlockSpec((tf, d), lambda i, f: (f, 0)),
                  row, vec, vec],
        out_specs=[row, row],
        out_shape=[jax.ShapeDtypeStruct((m, d), F32), jax.ShapeDtypeStruct((m, d), BF16)],
        scratch_shapes=[pltpu.VMEM((tm, d), F32)],
        compiler_params=_params("parallel", "arbitrary"),
        name="mlp_ln",
    )(xb, w1, w2, xf, g.reshape(1, d), b.reshape(1, d))


def _banded_kernel(*refs, hq, grp, window, tq, pw, has_sink, want_lse):
    refs = list(refs)
    sink_ref = refs.pop(0) if has_sink else None
    q_ref, kc_ref, kp_ref, vc_ref, vp_ref, o_ref = refs[:6]
    lse_ref = refs[6] if want_lse else None
    step, i = pl.program_id(1), pl.program_id(2)

    row = lax.broadcasted_iota(jnp.int32, (tq, tq), 0)
    col = lax.broadcasted_iota(jnp.int32, (tq, tq), 1)
    mask_c = (col <= row) & (row - col < window)
    rowp = lax.broadcasted_iota(jnp.int32, (tq, pw), 0)
    colp = lax.broadcasted_iota(jnp.int32, (tq, pw), 1)
    mask_p = (colp - pw - rowp > -window) & (i > 0)

    for h in range(hq):
        kv = h // grp
        qcols = slice(h * HEAD_DIM, (h + 1) * HEAD_DIM)
        kcols = slice(kv * HEAD_DIM, (kv + 1) * HEAD_DIM)
        q = q_ref[:, qcols]
        sc = jnp.where(mask_c, _dot_nt(q, kc_ref[:, kcols]), NEG)
        sp = jnp.where(mask_p, _dot_nt(q, kp_ref[:, kcols]), NEG)
        m = jnp.maximum(jnp.max(sc, axis=-1, keepdims=True), jnp.max(sp, axis=-1, keepdims=True))
        if has_sink:
            sink = sink_ref[step * hq + h]
            m = jnp.maximum(m, sink)
        pc = jnp.exp(sc - m)
        pp = jnp.exp(sp - m)
        l = jnp.sum(pc, axis=-1, keepdims=True) + jnp.sum(pp, axis=-1, keepdims=True)
        if has_sink:
            l = l + jnp.exp(sink - m)
        o = _dot(pc.astype(BF16), vc_ref[:, kcols]) + _dot(pp.astype(BF16), vp_ref[:, kcols])
        o_ref[:, qcols] = (o / l).astype(o_ref.dtype)
        if want_lse:
            lse_ref[:, h:h + 1] = m + jnp.log(l)


def _banded_attention(arr, *, n_steps, hq, grp, window, q_idx, k_idx, v_idx, tq, pw,
                      sinks=None, want_lse=False):
    bsz, seq, _ = arr.shape
    tq = min(tq, seq)
    pw = min(pw, tq)
    assert pw >= window - 1 or seq <= pw, (pw, window, seq)
    assert seq % tq == 0 and tq % pw == 0
    hk = hq // grp
    ratio = tq // pw
    wq, wk = hq * HEAD_DIM, hk * HEAD_DIM
    has_sink = sinks is not None

    def spec(width, rows, idx_fn, row_fn):
        if has_sink:
            return pl.BlockSpec((None, rows, width), lambda b, s, i, sk: (b, row_fn(i), idx_fn(s)))
        return pl.BlockSpec((None, rows, width), lambda b, s, i: (b, row_fn(i), idx_fn(s)))

    cur = lambda i: i
    prev = lambda i: jnp.maximum(i * ratio - 1, 0)
    in_specs = [spec(wq, tq, q_idx, cur), spec(wk, tq, k_idx, cur), spec(wk, pw, k_idx, prev),
                spec(wk, tq, v_idx, cur), spec(wk, pw, v_idx, prev)]
    out_specs = [spec(wq, tq, lambda s: s, cur)]
    out_shape = [jax.ShapeDtypeStruct((bsz, seq, n_steps * wq), BF16)]
    if want_lse:
        out_specs.append(pl.BlockSpec((None, None, tq, hq), lambda b, s, i: (b, s, i, 0)))
        out_shape.append(jax.ShapeDtypeStruct((bsz, n_steps, seq, hq), F32))
    kern = functools.partial(_banded_kernel, hq=hq, grp=grp, window=window, tq=tq, pw=pw,
                             has_sink=has_sink, want_lse=want_lse)
    grid = (bsz, n_steps, seq // tq)
    params = _params("parallel", "parallel", "arbitrary")
    if has_sink:
        grid_spec = pltpu.PrefetchScalarGridSpec(num_scalar_prefetch=1, grid=grid, in_specs=in_specs,
                                                 out_specs=out_specs)
        out = pl.pallas_call(kern, grid_spec=grid_spec, out_shape=out_shape, compiler_params=params,
                             name="banded_attn_sink")(sinks.astype(F32), arr, arr, arr, arr, arr)
    else:
        out = pl.pallas_call(kern, grid=grid, in_specs=in_specs, out_specs=out_specs, out_shape=out_shape,
                             compiler_params=params, name="banded_attn")(arr, arr, arr, arr, arr)
    return out if want_lse else out[0]


def _mix_kernel(o0_ref, o1_ref, o2_ref, l0_ref, l1_ref, l2_ref, out_ref):
    l0, l1, l2 = l0_ref[...], l1_ref[...], l2_ref[...]
    m = jnp.maximum(jnp.maximum(l0, l1), l2)
    e0, e1, e2 = jnp.exp(l0 - m), jnp.exp(l1 - m), jnp.exp(l2 - m)
    inv = 1.0 / (e0 + e1 + e2)
    w0, w1, w2 = e0 * inv, e1 * inv, e2 * inv
    for h in range(DIL_HEADS):
        cols = slice(h * HEAD_DIM, (h + 1) * HEAD_DIM)
        out = (w0[:, h:h + 1] * o0_ref[:, cols].astype(F32)
               + w1[:, h:h + 1] * o1_ref[:, cols].astype(F32)
               + w2[:, h:h + 1] * o2_ref[:, cols].astype(F32))
        out_ref[:, cols] = out.astype(out_ref.dtype)


def _mix_groups(outs, lses, *, tm=512):
    m, d = outs[0].shape
    tm = min(tm, m)
    row = pl.BlockSpec((tm, d), lambda i: (i, 0))
    lrow = pl.BlockSpec((tm, DIL_HEADS), lambda i: (i, 0))
    return pl.pallas_call(
        _mix_kernel,
        grid=(m // tm,),
        in_specs=[row, row, row, lrow, lrow, lrow],
        out_specs=row,
        out_shape=jax.ShapeDtypeStruct((m, d), BF16),
        compiler_params=_params("parallel"),
        name="dilated_mix",
    )(*outs, *lses)


def _compress_kernel(x_ref, pe_lo_ref, pe_hi_ref, w1_lo_ref, w1_hi_ref, w2_ref, o_ref, shift_ref):
    n = x_ref.shape[0]
    x = x_ref[...].astype(F32)
    first = _dot((x + pe_lo_ref[...]).astype(BF16), w1_lo_ref[...])
    second = _dot((x + pe_hi_ref[...]).astype(BF16), w1_hi_ref[...])
    shift_ref[0:n, :] = second
    shift_ref[n:n + 8, :] = jnp.zeros((8, HEAD_DIM), F32)
    h = first + shift_ref[1:n + 1, :]
    o_ref[...] = _dot(jax.nn.gelu(h).astype(BF16), w2_ref[...]).astype(o_ref.dtype)


def _compress(rows, pe, w1, w2):
    bg, n, width = rows.shape
    half = CMP_STRIDE * HEAD_DIM
    pe_lo = pe[:CMP_STRIDE].reshape(1, half)
    pe_hi = pe[CMP_STRIDE:].reshape(1, half)
    w1_lo = w1[:half].astype(BF16)
    w1_hi = w1[half:].astype(BF16)
    const = lambda shape: pl.BlockSpec(shape, lambda i: (0, 0))
    return pl.pallas_call(
        _compress_kernel,
        grid=(bg,),
        in_specs=[pl.BlockSpec((None, n, width), lambda i: (i, 0, 0)), const((1, half)), const((1, half)),
                  const((half, HEAD_DIM)), const((half, HEAD_DIM)), const((HEAD_DIM, HEAD_DIM))],
        out_specs=pl.BlockSpec((None, n, HEAD_DIM), lambda i: (i, 0, 0)),
        out_shape=jax.ShapeDtypeStruct((bg, n, HEAD_DIM), BF16),
        scratch_shapes=[pltpu.VMEM((n + 8, HEAD_DIM), F32)],
        compiler_params=_params("parallel"),
        name="nsa_compress",
    )(rows, pe_lo, pe_hi, w1_lo, w1_hi, w2.astype(BF16))


def _nsa_kernel(q_ref, kc_ref, vc_ref, c2s_ref, ks_ref, vs_ref, ow_ref, g_ref, o_ref, *, tq, kt, n_blk, n_sel):
    grp = NSA_GROUP
    n_cmp = kc_ref.shape[0]
    q0 = pl.program_id(2) * tq
    qs = jnp.concatenate([q_ref[:, h * HEAD_DIM:(h + 1) * HEAD_DIM] for h in range(grp)], axis=0)
    qpos = q0 + lax.broadcasted_iota(jnp.int32, (tq, 1), 0)

    cmp_end = lax.broadcasted_iota(jnp.int32, (tq, n_cmp), 1) * CMP_STRIDE + (CMP_BLOCK - 1)
    cmp_bias = jnp.where(cmp_end <= qpos, 0.0, NEG)
    s = _dot_nt(qs, kc_ref[...]).reshape(grp, tq, n_cmp) + cmp_bias[None]
    m = jnp.max(s, axis=-1, keepdims=True)
    m = jnp.where(m < 0.5 * NEG, 0.0, m)
    p = jnp.exp(s - m)
    l = jnp.sum(p, axis=-1, keepdims=True)
    p = p * (1.0 / jnp.where(l > 0.0, l, 1.0))
    o_cmp = _dot(p.reshape(grp * tq, n_cmp).astype(BF16), vc_ref[...])

    p_sum = jnp.sum(p, axis=0)
    p_hi = p_sum.astype(BF16)
    p_lo = (p_sum - p_hi.astype(F32)).astype(BF16)
    imp = _dot(p_hi, c2s_ref[...]) + _dot(p_lo, c2s_ref[...])

    blk = lax.broadcasted_iota(jnp.int32, (tq, n_blk), 1)
    cur = qpos // SEL_BLOCK
    forced = (blk == 0) | (blk == cur) | (blk == cur - 1)
    score = jnp.where(forced, jnp.inf, jnp.where(blk * SEL_BLOCK <= qpos, imp, -1.0))
    blk_f = blk.astype(F32)
    not_sel = jnp.ones((tq, n_blk), F32)
    for _ in range(n_sel):
        best = jnp.max(score, axis=-1, keepdims=True)
        first = jnp.min(jnp.where(score == best, blk_f, float(n_blk)), axis=-1, keepdims=True)
        hit = blk_f == first
        not_sel = jnp.where(hit, 0.0, not_sel)
        score = jnp.where(hit, -2.0, score)
    not_sel = not_sel.astype(BF16)

    blk_row = lax.broadcasted_iota(jnp.int32, (n_blk, kt), 0)
    key_blk = lax.broadcasted_iota(jnp.int32, (n_blk, kt), 1) // SEL_BLOCK
    key_col = lax.broadcasted_iota(jnp.int32, (tq, kt), 1)

    def chunk(j, carry):
        m, l, acc = carry
        k0 = pl.multiple_of(j * kt, kt)
        expand = jnp.where(blk_row == key_blk + j * (kt // SEL_BLOCK), 1.0, 0.0).astype(BF16)
        excluded = _dot(not_sel, expand)
        allowed = (excluded < 0.5) & (key_col + k0 <= qpos)
        bias = jnp.where(allowed, 0.0, NEG)
        s = _dot_nt(qs, ks_ref[pl.ds(k0, kt), :]).reshape(grp, tq, kt) + bias[None]
        s = s.reshape(grp * tq, kt)
        m_new = jnp.maximum(m, jnp.max(s, axis=-1, keepdims=True))
        alpha = jnp.exp(m - m_new)
        p = jnp.exp(s - m_new)
        l = alpha * l + jnp.sum(p, axis=-1, keepdims=True)
        acc = alpha * acc + _dot(p.astype(BF16), vs_ref[pl.ds(k0, kt), :])
        return m_new, l, acc

    n_chunks = (q0 + tq + kt - 1) // kt
    init = (jnp.full((grp * tq, 1), NEG, F32), jnp.zeros((grp * tq, 1), F32), jnp.zeros((grp * tq, HEAD_DIM), F32))
    _, l, acc = lax.fori_loop(0, n_chunks, chunk, init)
    o_slc = acc * (1.0 / l)

    gates = g_ref[...]
    for h in range(grp):
        rows = slice(h * tq, (h + 1) * tq)
        cols = slice(h * HEAD_DIM, (h + 1) * HEAD_DIM)
        out = (gates[:, 3 * h:3 * h + 1] * o_cmp[rows]
               + gates[:, 3 * h + 1:3 * h + 2] * o_slc[rows]
               + gates[:, 3 * h + 2:3 * h + 3] * ow_ref[:, cols].astype(F32))
        o_ref[:, cols] = out.astype(o_ref.dtype)


def _cmp_to_sel(n_cmp_pad, n_blk):
    pos = np.arange(n_cmp_pad)[:, None] * CMP_STRIDE + np.arange(CMP_BLOCK)[None, :]
    owner = pos // SEL_BLOCK
    frac = (owner[:, :, None] == np.arange(n_blk)[None, None, :]).sum(axis=1) / CMP_BLOCK
    return jnp.asarray(frac, dtype=BF16)


def _nsa_attention(proj, kc, vc, o_win, gates, *, q_idx, ks_idx, vs_idx, tq=256, kt=512):
    bsz, seq, _ = proj.shape
    tq = min(tq, seq)
    kt = min(kt, seq)
    n_cmp = kc.shape[1]
    n_blk = seq // SEL_BLOCK
    n_sel = min(N_SELECT, n_blk)
    wq = NSA_GROUP * HEAD_DIM
    c2s = _cmp_to_sel(n_cmp, n_blk)
    kern = functools.partial(_nsa_kernel, tq=tq, kt=kt, n_blk=n_blk, n_sel=n_sel)
    full = lambda idx_fn: pl.BlockSpec((None, seq, HEAD_DIM), lambda b, g, i: (b, 0, idx_fn(g)))
    cmp_spec = pl.BlockSpec((None, n_cmp, HEAD_DIM), lambda b, g, i: (b * NSA_KV_HEADS + g, 0, 0))
    return pl.pallas_call(
        kern,
        grid=(bsz, NSA_KV_HEADS, seq // tq),
        in_specs=[
            pl.BlockSpec((None, tq, wq), lambda b, g, i: (b, i, q_idx(g))),
            cmp_spec, cmp_spec,
            pl.BlockSpec((n_cmp, n_blk), lambda b, g, i: (0, 0)),
            full(ks_idx), full(vs_idx),
            pl.BlockSpec((None, tq, wq), lambda b, g, i: (b, i, g)),
            pl.BlockSpec((None, tq, HEAD_DIM), lambda b, g, i: (b, i, g)),
        ],
        out_specs=pl.BlockSpec((None, tq, wq), lambda b, g, i: (b, i, g)),
        out_shape=jax.ShapeDtypeStruct((bsz, seq, NSA_KV_HEADS * wq), BF16),
        compiler_params=_params("parallel", "parallel", "arbitrary"),
        name="nsa_core",
    )(proj, kc, vc, c2s, proj, proj, o_win, gates)


_BLK_AQ, _BLK_BQ = 0, NSA_HEADS
_BLK_AK = NSA_HEADS + SWA_HEADS
_BLK_BK = _BLK_AK + 3 * NSA_KV_HEADS
_BLK_AV = _BLK_BK + SWA_KV_HEADS
_BLK_BV = _BLK_AV + 3 * NSA_KV_HEADS


def _even_weights(w_in):
    aq, ak, av, ag, bq, bk, bv = jnp.split(w_in, np.cumsum(EVEN_WIDTHS)[:-1].tolist(), axis=-1)
    w_main = jnp.concatenate([aq, bq, ak, bk, av, bv], axis=-1).astype(BF16)
    ag = ag.reshape(D_MODEL, NSA_KV_HEADS, 3 * NSA_GROUP)
    w_gate = jnp.pad(ag, ((0, 0), (0, 0), (0, HEAD_DIM - 3 * NSA_GROUP))).reshape(D_MODEL, NSA_KV_HEADS * HEAD_DIM)
    return w_main, w_gate.astype(BF16)


def _tile_modes(blocks_per_tile, block_modes):
    modes = np.asarray(block_modes, np.int32).reshape(-1, blocks_per_tile)
    assert (modes == modes[:, :1]).all(), "a projection tile must not straddle q/k/v column groups"
    return jnp.asarray(modes[:, 0])


def _even_mixer(xf, xb, cos, sin, bsz, seq, w_in, w_o, pe_k, pe_v, ck_w1, ck_w2, cv_w1, cv_w2, sinks, g, b):
    w_main, w_gate = _even_weights(w_in)
    tn = 1024
    n_q, n_k = NSA_HEADS + SWA_HEADS, 3 * NSA_KV_HEADS + SWA_KV_HEADS
    modes = _tile_modes(tn // HEAD_DIM, [MODE_ROPE_SCALED] * n_q + [MODE_ROPE] * n_k + [MODE_PLAIN] * n_k)
    proj = _project(xb, w_main, modes, cos, sin, tn=tn).reshape(bsz, seq, EVEN_PROJ)
    gates = _gate_project(xb, w_gate).reshape(bsz, seq, NSA_KV_HEADS * HEAD_DIM)

    def cmp_rows(blk0):
        t = proj[:, :, blk0 * HEAD_DIM:(blk0 + NSA_KV_HEADS) * HEAD_DIM]
        t = t.reshape(bsz, seq // CMP_STRIDE, CMP_STRIDE, NSA_KV_HEADS, HEAD_DIM).transpose(0, 3, 1, 2, 4)
        return t.reshape(bsz * NSA_KV_HEADS, seq // CMP_STRIDE, CMP_STRIDE * HEAD_DIM)

    kc = _compress(cmp_rows(_BLK_AK), pe_k, ck_w1, ck_w2)
    vc = _compress(cmp_rows(_BLK_AV), pe_v, cv_w1, cv_w2)
    o_win = _banded_attention(
        proj, n_steps=NSA_KV_HEADS, hq=NSA_GROUP, grp=NSA_GROUP, window=NSA_WINDOW,
        q_idx=lambda s: s, k_idx=lambda s: _BLK_AK + 2 * NSA_KV_HEADS + s, v_idx=lambda s: _BLK_AV + 2 * NSA_KV_HEADS + s,
        tq=512, pw=512)
    o_a = _nsa_attention(proj, kc, vc, o_win, gates, q_idx=lambda s: s,
                         ks_idx=lambda s: _BLK_AK + NSA_KV_HEADS + s, vs_idx=lambda s: _BLK_AV + NSA_KV_HEADS + s)
    swa_grp = SWA_HEADS // SWA_KV_HEADS
    o_b = _banded_attention(
        proj, n_steps=SWA_KV_HEADS, hq=swa_grp, grp=swa_grp, window=SWA_WINDOW,
        q_idx=lambda s: NSA_HEADS // swa_grp + s, k_idx=lambda s: _BLK_BK + s, v_idx=lambda s: _BLK_BV + s,
        tq=256, pw=128, sinks=sinks)
    m = bsz * seq
    w_o = w_o.astype(BF16)
    return _out_proj_ln([o_a.reshape(m, NSA_Q), o_b.reshape(m, SWA_Q)], [w_o[:NSA_Q], w_o[NSA_Q:]], xf, g, b)


def _odd_mixer(xf, xb, cos, sin, bsz, seq, w_in, w_o, g, b):
    tn = 1024
    per_group = [MODE_ROPE_SCALED] * DIL_HEADS + [MODE_ROPE] * DIL_HEADS + [MODE_PLAIN] * DIL_HEADS
    modes = _tile_modes(tn // HEAD_DIM, per_group * len(DIL_PATTERNS))
    n_cols = len(DIL_PATTERNS) * DIL_QKV
    proj = _project(xb, w_in.astype(BF16), modes, cos, sin, tn=tn)
    m = bsz * seq
    width = DIL_HEADS * HEAD_DIM
    blocks_per_row = n_cols // width
    outs, lses = [], []
    for grp_i, (window, dil) in enumerate(DIL_PATTERNS):
        view = proj.reshape(bsz, seq // dil, dil * n_cols)
        base = 3 * grp_i
        o, lse = _banded_attention(
            view, n_steps=dil, hq=DIL_HEADS, grp=1, window=window // dil + 1,
            q_idx=lambda s, base=base: s * blocks_per_row + base,
            k_idx=lambda s, base=base: s * blocks_per_row + base + 1,
            v_idx=lambda s, base=base: s * blocks_per_row + base + 2,
            tq=256, pw=128, want_lse=True)
        outs.append(o.reshape(m, width))
        lses.append(lse.transpose(0, 2, 1, 3).reshape(m, DIL_HEADS))
    mixed = _mix_groups(outs, lses)
    return _out_proj_ln([mixed], [w_o.astype(BF16)], xf, g, b)


def kernel(x, positions, e_w_in, e_w_o, nsa_pe_k, nsa_pe_v, nsa_ck_w1, nsa_ck_w2, nsa_cv_w1, nsa_cv_w2, swa_sinks, o_w_in, o_w_o, ln1_g, ln1_b, mlp_w1, mlp_w2, ln2_g, ln2_b):
    bsz, seq, d = x.shape
    m = bsz * seq
    cos, sin = _rope_tables(positions)
    xf = x.reshape(m, d)
    xb = xf.astype(BF16)
    for layer in range(DEPTH):
        i = layer // 2
        if layer % 2 == 0:
            xf, xb = _even_mixer(xf, xb, cos, sin, bsz, seq, e_w_in[i], e_w_o[i], nsa_pe_k[i], nsa_pe_v[i],
                                 nsa_ck_w1[i], nsa_ck_w2[i], nsa_cv_w1[i], nsa_cv_w2[i], swa_sinks[i],
                                 ln1_g[layer], ln1_b[layer])
        else:
            xf, xb = _odd_mixer(xf, xb, cos, sin, bsz, seq, o_w_in[i], o_w_o[i], ln1_g[layer], ln1_b[layer])
        xf, xb = _mlp_ln(xf, xb, mlp_w1[layer].astype(BF16), mlp_w2[layer].astype(BF16), ln2_g[layer], ln2_b[layer])
    return xf.reshape(bsz, seq, d)
```

```python
import functools
import math

import numpy as np
import jax
import jax.numpy as jnp
from jax import lax
from jax.experimental import pallas as pl
from jax.experimental.pallas import tpu as pltpu

D_MODEL = 2048
DEPTH = 4
HEAD_DIM = 128
ROPE_THETA = 10000.0
LN_EPS = 1e-5

NSA_HEADS = D_MODEL // (2 * HEAD_DIM)
NSA_KV_HEADS = 2
NSA_GROUP = NSA_HEADS // NSA_KV_HEADS
CMP_BLOCK = 32
CMP_STRIDE = 16
SEL_BLOCK = 64
N_SELECT = 16
NSA_WINDOW = 512

SWA_HEADS = D_MODEL // (2 * HEAD_DIM)
SWA_KV_HEADS = 2
SWA_WINDOW = 128

DIL_HEADS = D_MODEL // HEAD_DIM
DIL_PATTERNS = ((128, 1), (512, 4), (2048, 16))

D_FF = 4 * D_MODEL
DN_ALPHA = (2 * DEPTH) ** 0.25

NSA_Q = NSA_HEADS * HEAD_DIM
NSA_KV = NSA_KV_HEADS * HEAD_DIM
SWA_Q = SWA_HEADS * HEAD_DIM
SWA_KV = SWA_KV_HEADS * HEAD_DIM
EVEN_WIDTHS = (NSA_Q, 3 * NSA_KV, 3 * NSA_KV, 3 * NSA_HEADS, SWA_Q, SWA_KV, SWA_KV)
EVEN_PROJ = NSA_Q + SWA_Q + 3 * NSA_KV + SWA_KV + 3 * NSA_KV + SWA_KV
DIL_QKV = 3 * DIL_HEADS * HEAD_DIM

LOG2E = math.log2(math.e)
LN2 = math.log(2.0)
QK_SCALE = HEAD_DIM ** -0.5 * LOG2E
NEG = -1e30
VMEM_LIMIT_BYTES = 56 * 1024 * 1024

MODE_PLAIN, MODE_ROPE, MODE_ROPE_SCALED = 0, 1, 2

F32 = jnp.float32
BF16 = jnp.bfloat16


def _params(*semantics):
    return pltpu.CompilerParams(dimension_semantics=semantics, vmem_limit_bytes=VMEM_LIMIT_BYTES)


def _dot(a, b):
    return jnp.dot(a, b, preferred_element_type=F32)


def _dot_nt(a, b):
    return lax.dot_general(a, b, (((1,), (1,)), ((), ())), preferred_element_type=F32)


def _rope_table_kernel(pos_ref, inv_ref, sign_ref, cos_ref, sin_ref):
    ang = pos_ref[...].astype(F32) * inv_ref[...]
    cos_ref[...] = jnp.cos(ang)
    sin_ref[...] = jnp.sin(ang) * sign_ref[...]


def _rope_tables(positions):
    m = positions.size
    tm = min(m, 2048)
    inv = ROPE_THETA ** (-jnp.arange(0, HEAD_DIM, 2, dtype=F32) / HEAD_DIM)
    inv_full = jnp.concatenate([inv, inv]).reshape(1, HEAD_DIM)
    half = HEAD_DIM // 2
    sign = jnp.concatenate([-jnp.ones((half,), F32), jnp.ones((half,), F32)]).reshape(1, HEAD_DIM)
    row = pl.BlockSpec((tm, HEAD_DIM), lambda i: (i, 0))
    const = pl.BlockSpec((1, HEAD_DIM), lambda i: (0, 0))
    return pl.pallas_call(
        _rope_table_kernel,
        grid=(m // tm,),
        in_specs=[pl.BlockSpec((tm, 1), lambda i: (i, 0)), const, const],
        out_specs=[row, row],
        out_shape=[jax.ShapeDtypeStruct((m, HEAD_DIM), F32)] * 2,
        compiler_params=_params("parallel"),
        name="rope_tables",
    )(positions.reshape(m, 1), inv_full, sign)


def _proj_kernel(modes_ref, x_ref, w_ref, cos_ref, sin_ref, o_ref, *scratch, tn, dil):
    mode = modes_ref[pl.program_id(1)]
    acc = _dot(x_ref[...], w_ref[...])
    blocks = [slice(blk * HEAD_DIM, (blk + 1) * HEAD_DIM) for blk in range(tn // HEAD_DIM)]

    def put(blk, val):
        if dil == 1:
            o_ref[:, blocks[blk]] = val.astype(o_ref.dtype)
        else:
            scratch[0][blk] = val

    @pl.when(mode == MODE_PLAIN)
    def _():
        for blk, cols in enumerate(blocks):
            put(blk, acc[:, cols])

    @pl.when(mode != MODE_PLAIN)
    def _():
        scale = jnp.where(mode == MODE_ROPE_SCALED, QK_SCALE, 1.0).astype(F32)
        cos = cos_ref[...] * scale
        sin = sin_ref[...] * scale
        for blk, cols in enumerate(blocks):
            t = acc[:, cols]
            put(blk, t * cos + pltpu.roll(t, HEAD_DIM // 2, 1) * sin)

    if dil > 1:
        rows = o_ref.shape[1]
        for c in range(dil):
            for blk, cols in enumerate(blocks):
                o_ref[c, :, cols] = scratch[0][blk, pl.ds(c, rows, stride=dil), :].astype(o_ref.dtype)


def _project(x, w, modes, cos, sin, *, seq, dil=1, tm=1024, tn=1024):
    m, k = x.shape
    n = w.shape[1]
    tm = min(tm, seq)
    tiles_per_batch = seq // tm
    bsz = m // seq
    if dil == 1:
        out_spec = pl.BlockSpec((tm, tn), lambda i, j, modes: (i, j))
        out_shape = jax.ShapeDtypeStruct((m, n), BF16)
        scratch = []
    else:
        out_spec = pl.BlockSpec((None, dil, tm // dil, tn),
                                lambda i, j, modes: (i // tiles_per_batch, 0, i % tiles_per_batch, j))
        out_shape = jax.ShapeDtypeStruct((bsz, dil, seq // dil, n), BF16)
        scratch = [pltpu.VMEM((tn // HEAD_DIM, tm, HEAD_DIM), F32)]
    grid_spec = pltpu.PrefetchScalarGridSpec(
        num_scalar_prefetch=1,
        grid=(m // tm, n // tn),
        in_specs=[
            pl.BlockSpec((tm, k), lambda i, j, modes: (i, 0)),
            pl.BlockSpec((k, tn), lambda i, j, modes: (0, j)),
            pl.BlockSpec((tm, HEAD_DIM), lambda i, j, modes: (i, 0)),
            pl.BlockSpec((tm, HEAD_DIM), lambda i, j, modes: (i, 0)),
        ],
        out_specs=out_spec,
        scratch_shapes=scratch,
    )
    out = pl.pallas_call(
        functools.partial(_proj_kernel, tn=tn, dil=dil),
        grid_spec=grid_spec,
        out_shape=out_shape,
        compiler_params=_params("parallel", "arbitrary"),
        name="in_proj" if dil == 1 else "in_proj_dil%d" % dil,
    )(modes, x, w, cos, sin)
    return out.reshape(bsz, 1, seq, n) if dil == 1 else out


def _gate_kernel(x_ref, w_ref, o_ref):
    o_ref[...] = jax.nn.sigmoid(_dot(x_ref[...], w_ref[...]))


def _gate_project(x, w, *, tm=1024):
    m, k = x.shape
    n = w.shape[1]
    tm = min(tm, m)
    return pl.pallas_call(
        _gate_kernel,
        grid=(m // tm,),
        in_specs=[pl.BlockSpec((tm, k), lambda i: (i, 0)), pl.BlockSpec((k, n), lambda i: (0, 0))],
        out_specs=pl.BlockSpec((tm, n), lambda i: (i, 0)),
        out_shape=jax.ShapeDtypeStruct((m, n), F32),
        compiler_params=_params("parallel"),
        name="gate_proj",
    )(x, w)


def _residual_layer_norm(res, y, g, b):
    z = DN_ALPHA * res + y
    mu = jnp.mean(z, axis=-1, keepdims=True)
    zc = z - mu
    var = jnp.mean(zc * zc, axis=-1, keepdims=True)
    return zc * lax.rsqrt(var + LN_EPS) * g + b


def _out_proj_kernel(*refs, n_in):
    x_refs, w_refs = refs[:n_in], refs[n_in:2 * n_in]
    res_ref, g_ref, b_ref, of_ref, ob_ref = refs[2 * n_in:]
    y = _dot(x_refs[0][...], w_refs[0][...])
    for x_ref, w_ref in zip(x_refs[1:], w_refs[1:]):
        y += _dot(x_ref[...], w_ref[...])
    out = _residual_layer_norm(res_ref[...], y, g_ref[...], b_ref[...])
    of_ref[...] = out
    ob_ref[...] = out.astype(BF16)


def _out_proj_ln(xs, ws, res, g, b, *, tm=512):
    m, d = res.shape
    tm = min(tm, m)
    n_in = len(xs)
    in_specs = [pl.BlockSpec((tm, x.shape[1]), lambda i: (i, 0)) for x in xs]
    in_specs += [pl.BlockSpec(w.shape, lambda i: (0, 0)) for w in ws]
    row = pl.BlockSpec((tm, d), lambda i: (i, 0))
    vec = pl.BlockSpec((1, d), lambda i: (0, 0))
    in_specs += [row, vec, vec]
    return pl.pallas_call(
        functools.partial(_out_proj_kernel, n_in=n_in),
        grid=(m // tm,),
        in_specs=in_specs,
        out_specs=[row, row],
        out_shape=[jax.ShapeDtypeStruct((m, d), F32), jax.ShapeDtypeStruct((m, d), BF16)],
        compiler_params=_params("parallel"),
        name="out_proj_ln",
    )(*xs, *ws, res, g.reshape(1, d), b.reshape(1, d))


def _mlp_kernel(xb_ref, w1_ref, w2_ref, xf_ref, g_ref, b_ref, of_ref, ob_ref, acc_ref):
    f = pl.program_id(1)
    h = _dot(xb_ref[...], w1_ref[...])
    h = jnp.square(jnp.maximum(h, 0.0)).astype(BF16)
    part = _dot(h, w2_ref[...])

    @pl.when(f == 0)
    def _():
        acc_ref[...] = part

    @pl.when(f > 0)
    def _():
        acc_ref[...] += part

    @pl.when(f == pl.num_programs(1) - 1)
    def _():
        out = _residual_layer_norm(xf_ref[...], acc_ref[...], g_ref[...], b_ref[...])
        of_ref[...] = out
        ob_ref[...] = out.astype(BF16)


def _mlp_ln(xf, xb, w1, w2, g, b, *, tm=512, tf=512):
    m, d = xf.shape
    ff = w1.shape[1]
    tm = min(tm, m)
    row = pl.BlockSpec((tm, d), lambda i, f: (i, 0))
    vec = pl.BlockSpec((1, d), lambda i, f: (0, 0))
    return pl.pallas_call(
        _mlp_kernel,
        grid=(m // tm, ff // tf),
        in_specs=[row, pl.BlockSpec((d, tf), lambda i, f: (0, f)), pl.BlockSpec((tf, d), lambda i, f: (f, 0)),
                  row, vec, vec],
        out_specs=[row, row],
        out_shape=[jax.ShapeDtypeStruct((m, d), F32), jax.ShapeDtypeStruct((m, d), BF16)],
        scratch_shapes=[pltpu.VMEM((tm, d), F32)],
        compiler_params=_params("parallel", "arbitrary"),
        name="mlp_ln",
    )(xb, w1, w2, xf, g.reshape(1, d), b.reshape(1, d))


def _banded_kernel(*refs, hq, grp, window, tq, pw, has_sink, want_lse):
    refs = list(refs)
    sink_ref = refs.pop(0) if has_sink else None
    q_ref, kc_ref, kp_ref, vc_ref, vp_ref, o_ref = refs[:6]
    lse_ref = refs[6] if want_lse else None
    step, i = pl.program_id(1), pl.program_id(2)

    row = lax.broadcasted_iota(jnp.int32, (tq, tq), 0)
    col = lax.broadcasted_iota(jnp.int32, (tq, tq), 1)
    mask_c = (col <= row) & (row - col < window)
    rowp = lax.broadcasted_iota(jnp.int32, (tq, pw), 0)
    colp = lax.broadcasted_iota(jnp.int32, (tq, pw), 1)
    mask_p = (colp - pw - rowp > -window) & (i > 0)

    for h in range(hq):
        kv = h // grp
        qcols = slice(h * HEAD_DIM, (h + 1) * HEAD_DIM)
        kcols = slice(kv * HEAD_DIM, (kv + 1) * HEAD_DIM)
        q = q_ref[:, qcols]
        sc = jnp.where(mask_c, _dot_nt(q, kc_ref[:, kcols]), NEG)
        sp = jnp.where(mask_p, _dot_nt(q, kp_ref[:, kcols]), NEG)
        m = jnp.maximum(jnp.max(sc, axis=-1, keepdims=True), jnp.max(sp, axis=-1, keepdims=True))
        if has_sink:
            sink = sink_ref[step * hq + h] * LOG2E
            m = jnp.maximum(m, sink)
        pc = jnp.exp2(sc - m)
        pp = jnp.exp2(sp - m)
        l = jnp.sum(pc, axis=-1, keepdims=True) + jnp.sum(pp, axis=-1, keepdims=True)
        if has_sink:
            l = l + jnp.exp2(sink - m)
        o = _dot(pc.astype(BF16), vc_ref[:, kcols]) + _dot(pp.astype(BF16), vp_ref[:, kcols])
        o_ref[:, qcols] = (o / l).astype(o_ref.dtype)
        if want_lse:
            lse_ref[:, h:h + 1] = (m + jnp.log2(l)) * LN2


def _banded_attention(arr, *, n_steps, hq, grp, window, q_at, k_at, v_at, o_at, o_dims, tq, pw,
                      sinks=None, want_lse=False):
    bsz, _, seq, _ = arr.shape
    tq = min(tq, seq)
    pw = min(pw, tq)
    assert pw >= window - 1 or seq <= pw, (pw, window, seq)
    assert seq % tq == 0 and tq % pw == 0
    hk = hq // grp
    ratio = tq // pw
    wq, wk = hq * HEAD_DIM, hk * HEAD_DIM
    has_sink = sinks is not None

    def spec(width, rows, at_fn, row_fn):
        def index(b, s, i, *_):
            plane, col = at_fn(s)
            return (b, plane, row_fn(i), col)
        return pl.BlockSpec((None, None, rows, width), index)

    cur = lambda i: i
    prev = lambda i: jnp.maximum(i * ratio - 1, 0)
    in_specs = [spec(wq, tq, q_at, cur), spec(wk, tq, k_at, cur), spec(wk, pw, k_at, prev),
                spec(wk, tq, v_at, cur), spec(wk, pw, v_at, prev)]
    out_specs = [spec(wq, tq, o_at, cur)]
    out_shape = [jax.ShapeDtypeStruct((bsz, o_dims[0], seq, o_dims[1]), BF16)]
    if want_lse:
        out_specs.append(pl.BlockSpec((None, None, tq, hq), lambda b, s, i: (b, s, i, 0)))
        out_shape.append(jax.ShapeDtypeStruct((bsz, n_steps, seq, hq), F32))
    kern = functools.partial(_banded_kernel, hq=hq, grp=grp, window=window, tq=tq, pw=pw,
                             has_sink=has_sink, want_lse=want_lse)
    grid = (bsz, n_steps, seq // tq)
    params = _params("parallel", "parallel", "arbitrary")
    if has_sink:
        grid_spec = pltpu.PrefetchScalarGridSpec(num_scalar_prefetch=1, grid=grid, in_specs=in_specs,
                                                 out_specs=out_specs)
        out = pl.pallas_call(kern, grid_spec=grid_spec, out_shape=out_shape, compiler_params=params,
                             name="banded_attn_sink")(sinks.astype(F32), arr, arr, arr, arr, arr)
    else:
        out = pl.pallas_call(kern, grid=grid, in_specs=in_specs, out_specs=out_specs, out_shape=out_shape,
                             compiler_params=params, name="banded_attn")(arr, arr, arr, arr, arr)
    return out if want_lse else out[0]


def _mix_kernel(o0_ref, o1_ref, o2_ref, l0_ref, l1_ref, l2_ref, out_ref, t1_ref, t2_ref):
    for src, dst in ((o1_ref, t1_ref), (o2_ref, t2_ref)):
        dil, rows = src.shape[0], src.shape[1]
        for c in range(dil):
            for h in range(DIL_HEADS):
                dst[h, pl.ds(c, rows, stride=dil), :] = src[c, :, h * HEAD_DIM:(h + 1) * HEAD_DIM].astype(F32)
    l0, l1, l2 = l0_ref[...], l1_ref[...], l2_ref[...]
    m = jnp.maximum(jnp.maximum(l0, l1), l2)
    e0, e1, e2 = jnp.exp(l0 - m), jnp.exp(l1 - m), jnp.exp(l2 - m)
    inv = 1.0 / (e0 + e1 + e2)
    w0, w1, w2 = e0 * inv, e1 * inv, e2 * inv
    for h in range(DIL_HEADS):
        cols = slice(h * HEAD_DIM, (h + 1) * HEAD_DIM)
        out = (w0[:, h:h + 1] * o0_ref[:, cols].astype(F32)
               + w1[:, h:h + 1] * t1_ref[h]
               + w2[:, h:h + 1] * t2_ref[h])
        out_ref[:, cols] = out.astype(out_ref.dtype)


def _mix_groups(outs, lses, *, seq, tm=512):
    bsz, _, _, d = outs[0].shape
    tm = min(tm, seq)
    tiles_per_batch = seq // tm

    def o_spec(dil):
        return pl.BlockSpec((None, dil, tm // dil, d), lambda i: (i // tiles_per_batch, 0, i % tiles_per_batch, 0))

    row = pl.BlockSpec((tm, d), lambda i: (i, 0))
    lrow = pl.BlockSpec((tm, DIL_HEADS), lambda i: (i, 0))
    dils = [o.shape[1] for o in outs]
    assert dils[0] == 1
    return pl.pallas_call(
        _mix_kernel,
        grid=(bsz * tiles_per_batch,),
        in_specs=[row, o_spec(dils[1]), o_spec(dils[2]), lrow, lrow, lrow],
        out_specs=row,
        out_shape=jax.ShapeDtypeStruct((bsz * seq, d), BF16),
        scratch_shapes=[pltpu.VMEM((DIL_HEADS, tm, HEAD_DIM), F32)] * 2,
        compiler_params=_params("parallel"),
        name="dilated_mix",
    )(outs[0].reshape(bsz * seq, d), outs[1], outs[2], *lses)


def _compress_kernel(x_ref, pe_lo_ref, pe_hi_ref, w1_lo_ref, w1_hi_ref, w2_ref, o_ref, shift_ref):
    n = x_ref.shape[0]
    x = x_ref[...].astype(F32)
    first = _dot((x + pe_lo_ref[...]).astype(BF16), w1_lo_ref[...])
    second = _dot((x + pe_hi_ref[...]).astype(BF16), w1_hi_ref[...])
    shift_ref[0:n, :] = second
    shift_ref[n:n + 8, :] = jnp.zeros((8, HEAD_DIM), F32)
    h = first + shift_ref[1:n + 1, :]
    o_ref[...] = _dot(jax.nn.gelu(h).astype(BF16), w2_ref[...]).astype(o_ref.dtype)


def _compress(rows, pe, w1, w2):
    bg, n, width = rows.shape
    half = CMP_STRIDE * HEAD_DIM
    pe_lo = pe[:CMP_STRIDE].reshape(1, half)
    pe_hi = pe[CMP_STRIDE:].reshape(1, half)
    w1_lo = w1[:half].astype(BF16)
    w1_hi = w1[half:].astype(BF16)
    const = lambda shape: pl.BlockSpec(shape, lambda i: (0, 0))
    return pl.pallas_call(
        _compress_kernel,
        grid=(bg,),
        in_specs=[pl.BlockSpec((None, n, width), lambda i: (i, 0, 0)), const((1, half)), const((1, half)),
                  const((half, HEAD_DIM)), const((half, HEAD_DIM)), const((HEAD_DIM, HEAD_DIM))],
        out_specs=pl.BlockSpec((None, n, HEAD_DIM), lambda i: (i, 0, 0)),
        out_shape=jax.ShapeDtypeStruct((bg, n, HEAD_DIM), BF16),
        scratch_shapes=[pltpu.VMEM((n + 8, HEAD_DIM), F32)],
        compiler_params=_params("parallel"),
        name="nsa_compress",
    )(rows, pe_lo, pe_hi, w1_lo, w1_hi, w2.astype(BF16))


def _nsa_kernel(q_ref, kc_ref, vc_ref, c2s_ref, ks_ref, vs_ref, ow_ref, g_ref, o_ref, kaug_ref, qaug_ref,
                *, tq, kt, n_blk, n_sel):
    grp = NSA_GROUP
    n_cmp = kc_ref.shape[0]
    seq = ks_ref.shape[0]
    n_halves = qaug_ref.shape[0]
    chunks_per_half = HEAD_DIM * SEL_BLOCK // kt
    q0 = pl.program_id(2) * tq

    @pl.when(pl.program_id(2) == 0)
    def _():
        def fill(r, carry):
            r0 = pl.multiple_of(r * kt, kt)
            blk_of_row = (lax.broadcasted_iota(jnp.int32, (kt, HEAD_DIM), 0) + r0) // SEL_BLOCK
            lane = lax.broadcasted_iota(jnp.int32, (kt, HEAD_DIM), 1)
            kaug_ref[pl.ds(r0, kt), 0:HEAD_DIM] = ks_ref[pl.ds(r0, kt), :]
            kaug_ref[pl.ds(r0, kt), HEAD_DIM:2 * HEAD_DIM] = jnp.where(
                lane == blk_of_row % HEAD_DIM, NEG, 0.0).astype(kaug_ref.dtype)
            return carry
        lax.fori_loop(0, seq // kt, fill, 0)

    qs = jnp.concatenate([q_ref[:, h * HEAD_DIM:(h + 1) * HEAD_DIM] for h in range(grp)], axis=0)
    qpos = q0 + lax.broadcasted_iota(jnp.int32, (tq, 1), 0)

    cmp_end = lax.broadcasted_iota(jnp.int32, (tq, n_cmp), 1) * CMP_STRIDE + (CMP_BLOCK - 1)
    cmp_bias = jnp.where(cmp_end <= qpos, 0.0, NEG)
    s = _dot_nt(qs, kc_ref[...]).reshape(grp, tq, n_cmp) + cmp_bias[None]
    m = jnp.max(s, axis=-1, keepdims=True)
    m = jnp.where(m < 0.5 * NEG, 0.0, m)
    p = jnp.exp2(s - m)
    l = jnp.sum(p, axis=-1, keepdims=True)
    p = p * (1.0 / jnp.where(l > 0.0, l, 1.0))
    o_cmp = _dot(p.reshape(grp * tq, n_cmp).astype(BF16), vc_ref[...])

    p_sum = jnp.sum(p, axis=0)
    p_hi = p_sum.astype(BF16)
    p_lo = (p_sum - p_hi.astype(F32)).astype(BF16)
    imp = _dot(p_hi, c2s_ref[...]) + _dot(p_lo, c2s_ref[...])

    blk = lax.broadcasted_iota(jnp.int32, (tq, n_blk), 1)
    cur = qpos // SEL_BLOCK
    forced = (blk == 0) | (blk == cur) | (blk == cur - 1)
    score = jnp.where(forced, jnp.inf, jnp.where(blk * SEL_BLOCK <= qpos, imp, -1.0))
    blk_f = blk.astype(F32)
    not_sel = jnp.ones((tq, n_blk), F32)
    for _ in range(n_sel):
        best = jnp.max(score, axis=-1, keepdims=True)
        first = jnp.min(jnp.where(score == best, blk_f, float(n_blk)), axis=-1, keepdims=True)
        hit = blk_f == first
        not_sel = jnp.where(hit, 0.0, not_sel)
        score = jnp.where(hit, -2.0, score)
    if n_blk < n_halves * HEAD_DIM:
        not_sel = jnp.concatenate([not_sel, jnp.zeros((tq, n_halves * HEAD_DIM - n_blk), F32)], axis=1)
    for half in range(n_halves):
        flags = not_sel[:, half * HEAD_DIM:(half + 1) * HEAD_DIM].astype(qaug_ref.dtype)
        qaug_ref[half, :, 0:HEAD_DIM] = qs
        qaug_ref[half, :, HEAD_DIM:2 * HEAD_DIM] = jnp.concatenate([flags] * grp, axis=0)

    key_col = lax.broadcasted_iota(jnp.int32, (tq, kt), 1)

    def chunk(j, carry, causal):
        m, l, acc = carry
        k0 = pl.multiple_of(j * kt, kt)
        s = _dot_nt(qaug_ref[j // chunks_per_half], kaug_ref[pl.ds(k0, kt), :])
        if causal:
            bias = jnp.where(key_col + k0 <= qpos, 0.0, NEG)
            s = (s.reshape(grp, tq, kt) + bias[None]).reshape(grp * tq, kt)
        m_new = jnp.maximum(m, jnp.max(s, axis=-1, keepdims=True))
        alpha = jnp.exp2(m - m_new)
        p = jnp.exp2(s - m_new)
        l = alpha * l + jnp.sum(p, axis=-1, keepdims=True)
        acc = alpha * acc + _dot(p.astype(BF16), vs_ref[pl.ds(k0, kt), :])
        return m_new, l, acc

    n_interior = q0 // kt
    init = (jnp.full((grp * tq, 1), NEG, F32), jnp.zeros((grp * tq, 1), F32), jnp.zeros((grp * tq, HEAD_DIM), F32))
    carry = lax.fori_loop(0, n_interior, functools.partial(chunk, causal=False), init)
    _, l, acc = chunk(n_interior, carry, causal=True)
    o_slc = acc * (1.0 / l)

    gates = g_ref[...]
    for h in range(grp):
        rows = slice(h * tq, (h + 1) * tq)
        cols = slice(h * HEAD_DIM, (h + 1) * HEAD_DIM)
        out = (gates[:, 3 * h:3 * h + 1] * o_cmp[rows]
               + gates[:, 3 * h + 1:3 * h + 2] * o_slc[rows]
               + gates[:, 3 * h + 2:3 * h + 3] * ow_ref[:, cols].astype(F32))
        o_ref[:, cols] = out.astype(o_ref.dtype)


def _cmp_to_sel(n_cmp_pad, n_blk):
    pos = np.arange(n_cmp_pad)[:, None] * CMP_STRIDE + np.arange(CMP_BLOCK)[None, :]
    owner = pos // SEL_BLOCK
    frac = (owner[:, :, None] == np.arange(n_blk)[None, None, :]).sum(axis=1) / CMP_BLOCK
    return jnp.asarray(frac, dtype=BF16)


def _nsa_attention(proj, kc, vc, o_win, gates, *, q_idx, ks_idx, vs_idx, tq=256, kt=512):
    bsz, seq, _ = proj.shape
    tq = min(tq, seq)
    kt = min(kt, seq)
    assert kt % tq == 0 and (HEAD_DIM * SEL_BLOCK) % kt == 0 and seq % kt == 0
    n_cmp = kc.shape[1]
    n_blk = seq // SEL_BLOCK
    n_sel = min(N_SELECT, n_blk)
    n_halves = -(-n_blk // HEAD_DIM)
    wq = NSA_GROUP * HEAD_DIM
    c2s = _cmp_to_sel(n_cmp, n_blk)
    kern = functools.partial(_nsa_kernel, tq=tq, kt=kt, n_blk=n_blk, n_sel=n_sel)
    full = lambda idx_fn: pl.BlockSpec((None, seq, HEAD_DIM), lambda b, g, i: (b, 0, idx_fn(g)))
    cmp_spec = pl.BlockSpec((None, n_cmp, HEAD_DIM), lambda b, g, i: (b * NSA_KV_HEADS + g, 0, 0))
    return pl.pallas_call(
        kern,
        grid=(bsz, NSA_KV_HEADS, seq // tq),
        in_specs=[
            pl.BlockSpec((None, tq, wq), lambda b, g, i: (b, i, q_idx(g))),
            cmp_spec, cmp_spec,
            pl.BlockSpec((n_cmp, n_blk), lambda b, g, i: (0, 0)),
            full(ks_idx), full(vs_idx),
            pl.BlockSpec((None, tq, wq), lambda b, g, i: (b, i, g)),
            pl.BlockSpec((None, tq, HEAD_DIM), lambda b, g, i: (b, i, g)),
        ],
        out_specs=pl.BlockSpec((None, tq, wq), lambda b, g, i: (b, i, g)),
        out_shape=jax.ShapeDtypeStruct((bsz, seq, NSA_KV_HEADS * wq), BF16),
        scratch_shapes=[pltpu.VMEM((seq, 2 * HEAD_DIM), BF16),
                        pltpu.VMEM((n_halves, NSA_GROUP * tq, 2 * HEAD_DIM), BF16)],
        compiler_params=_params("parallel", "parallel", "arbitrary"),
        name="nsa_core",
    )(proj, kc, vc, c2s, proj, proj, o_win, gates)


_BLK_AQ, _BLK_BQ = 0, NSA_HEADS
_BLK_AK = NSA_HEADS + SWA_HEADS
_BLK_BK = _BLK_AK + 3 * NSA_KV_HEADS
_BLK_AV = _BLK_BK + SWA_KV_HEADS
_BLK_BV = _BLK_AV + 3 * NSA_KV_HEADS


def _even_weights(w_in):
    aq, ak, av, ag, bq, bk, bv = jnp.split(w_in, np.cumsum(EVEN_WIDTHS)[:-1].tolist(), axis=-1)
    w_main = jnp.concatenate([aq, bq, ak, bk, av, bv], axis=-1).astype(BF16)
    ag = ag.reshape(D_MODEL, NSA_KV_HEADS, 3 * NSA_GROUP)
    w_gate = jnp.pad(ag, ((0, 0), (0, 0), (0, HEAD_DIM - 3 * NSA_GROUP))).reshape(D_MODEL, NSA_KV_HEADS * HEAD_DIM)
    return w_main, w_gate.astype(BF16)


def _tile_modes(blocks_per_tile, block_modes):
    modes = np.asarray(block_modes, np.int32).reshape(-1, blocks_per_tile)
    assert (modes == modes[:, :1]).all(), "a projection tile must not straddle q/k/v column groups"
    return jnp.asarray(modes[:, 0])


def _even_mixer(xf, xb, cos, sin, bsz, seq, w_in, w_o, pe_k, pe_v, ck_w1, ck_w2, cv_w1, cv_w2, sinks, g, b):
    w_main, w_gate = _even_weights(w_in)
    tn = 1024
    n_q, n_k = NSA_HEADS + SWA_HEADS, 3 * NSA_KV_HEADS + SWA_KV_HEADS
    modes = _tile_modes(tn // HEAD_DIM, [MODE_ROPE_SCALED] * n_q + [MODE_ROPE] * n_k + [MODE_PLAIN] * n_k)
    proj4 = _project(xb, w_main, modes, cos, sin, seq=seq, tn=tn)
    proj = proj4.reshape(bsz, seq, EVEN_PROJ)
    gates = _gate_project(xb, w_gate).reshape(bsz, seq, NSA_KV_HEADS * HEAD_DIM)

    def cmp_rows(blk0):
        t = proj[:, :, blk0 * HEAD_DIM:(blk0 + NSA_KV_HEADS) * HEAD_DIM]
        t = t.reshape(bsz, seq // CMP_STRIDE, CMP_STRIDE, NSA_KV_HEADS, HEAD_DIM).transpose(0, 3, 1, 2, 4)
        return t.reshape(bsz * NSA_KV_HEADS, seq // CMP_STRIDE, CMP_STRIDE * HEAD_DIM)

    kc = _compress(cmp_rows(_BLK_AK), pe_k, ck_w1, ck_w2)
    vc = _compress(cmp_rows(_BLK_AV), pe_v, cv_w1, cv_w2)
    o_win = _banded_attention(
        proj4, n_steps=NSA_KV_HEADS, hq=NSA_GROUP, grp=NSA_GROUP, window=NSA_WINDOW,
        q_at=lambda s: (0, s), k_at=lambda s: (0, _BLK_AK + 2 * NSA_KV_HEADS + s),
        v_at=lambda s: (0, _BLK_AV + 2 * NSA_KV_HEADS + s), o_at=lambda s: (0, s), o_dims=(1, NSA_Q),
        tq=512, pw=512).reshape(bsz, seq, NSA_Q)
    o_a = _nsa_attention(proj, kc, vc, o_win, gates, q_idx=lambda s: s,
                         ks_idx=lambda s: _BLK_AK + NSA_KV_HEADS + s, vs_idx=lambda s: _BLK_AV + NSA_KV_HEADS + s)
    swa_grp = SWA_HEADS // SWA_KV_HEADS
    o_b = _banded_attention(
        proj4, n_steps=SWA_KV_HEADS, hq=swa_grp, grp=swa_grp, window=SWA_WINDOW,
        q_at=lambda s: (0, NSA_HEADS // swa_grp + s), k_at=lambda s: (0, _BLK_BK + s),
        v_at=lambda s: (0, _BLK_BV + s), o_at=lambda s: (0, s), o_dims=(1, SWA_Q),
        tq=256, pw=128, sinks=sinks)
    m = bsz * seq
    w_o = w_o.astype(BF16)
    return _out_proj_ln([o_a.reshape(m, NSA_Q), o_b.reshape(m, SWA_Q)], [w_o[:NSA_Q], w_o[NSA_Q:]], xf, g, b)


def _odd_mixer(xf, xb, cos, sin, bsz, seq, w_in, w_o, g, b):
    tn = 1024
    modes = _tile_modes(tn // HEAD_DIM, [MODE_ROPE_SCALED] * DIL_HEADS + [MODE_ROPE] * DIL_HEADS + [MODE_PLAIN] * DIL_HEADS)
    m = bsz * seq
    width = DIL_HEADS * HEAD_DIM
    w_in = w_in.astype(BF16)
    outs, lses = [], []
    for grp_i, (window, dil) in enumerate(DIL_PATTERNS):
        qkv = _project(xb, w_in[:, grp_i * DIL_QKV:(grp_i + 1) * DIL_QKV], modes, cos, sin, seq=seq, dil=dil, tn=tn)
        o, lse = _banded_attention(
            qkv, n_steps=dil, hq=DIL_HEADS, grp=1, window=window // dil + 1,
            q_at=lambda s: (s, 0), k_at=lambda s: (s, 1), v_at=lambda s: (s, 2), o_at=lambda s: (s, 0),
            o_dims=(dil, width), tq=256, pw=128, want_lse=True)
        outs.append(o)
        lses.append(lse.transpose(0, 2, 1, 3).reshape(m, DIL_HEADS))
    mixed = _mix_groups(outs, lses, seq=seq)
    return _out_proj_ln([mixed], [w_o.astype(BF16)], xf, g, b)


def kernel(x, positions, e_w_in, e_w_o, nsa_pe_k, nsa_pe_v, nsa_ck_w1, nsa_ck_w2, nsa_cv_w1, nsa_cv_w2, swa_sinks, o_w_in, o_w_o, ln1_g, ln1_b, mlp_w1, mlp_w2, ln2_g, ln2_b):
    bsz, seq, d = x.shape
    m = bsz * seq
    cos, sin = _rope_tables(positions)
    xf = x.reshape(m, d)
    xb = xf.astype(BF16)
    for layer in range(DEPTH):
        i = layer // 2
        if layer % 2 == 0:
            xf, xb = _even_mixer(xf, xb, cos, sin, bsz, seq, e_w_in[i], e_w_o[i], nsa_pe_k[i], nsa_pe_v[i],
                                 nsa_ck_w1[i], nsa_ck_w2[i], nsa_cv_w1[i], nsa_cv_w2[i], swa_sinks[i],
                                 ln1_g[layer], ln1_b[layer])
        else:
            xf, xb = _odd_mixer(xf, xb, cos, sin, bsz, seq, o_w_in[i], o_w_o[i], ln1_g[layer], ln1_b[layer])
        xf, xb = _mlp_ln(xf, xb, mlp_w1[layer].astype(BF16), mlp_w2[layer].astype(BF16), ln2_g[layer], ln2_b[layer])
    return xf.reshape(bsz, seq, d)
```

```python
import functools
import math

import numpy as np
import jax
import jax.numpy as jnp
from jax import lax
from jax.experimental import pallas as pl
from jax.experimental.pallas import tpu as pltpu

D_MODEL = 2048
DEPTH = 4
HEAD_DIM = 128
ROPE_THETA = 10000.0
LN_EPS = 1e-5

NSA_HEADS = D_MODEL // (2 * HEAD_DIM)
NSA_KV_HEADS = 2
NSA_GROUP = NSA_HEADS // NSA_KV_HEADS
CMP_BLOCK = 32
CMP_STRIDE = 16
SEL_BLOCK = 64
N_SELECT = 16
NSA_WINDOW = 512

SWA_HEADS = D_MODEL // (2 * HEAD_DIM)
SWA_KV_HEADS = 2
SWA_WINDOW = 128

DIL_HEADS = D_MODEL // HEAD_DIM
DIL_PATTERNS = ((128, 1), (512, 4), (2048, 16))

D_FF = 4 * D_MODEL
DN_ALPHA = (2 * DEPTH) ** 0.25

NSA_Q = NSA_HEADS * HEAD_DIM
NSA_KV = NSA_KV_HEADS * HEAD_DIM
SWA_Q = SWA_HEADS * HEAD_DIM
SWA_KV = SWA_KV_HEADS * HEAD_DIM
EVEN_WIDTHS = (NSA_Q, 3 * NSA_KV, 3 * NSA_KV, 3 * NSA_HEADS, SWA_Q, SWA_KV, SWA_KV)
EVEN_PROJ = NSA_Q + SWA_Q + 3 * NSA_KV + SWA_KV + 3 * NSA_KV + SWA_KV
DIL_QKV = 3 * DIL_HEADS * HEAD_DIM

LOG2E = math.log2(math.e)
LN2 = math.log(2.0)
QK_SCALE = HEAD_DIM ** -0.5 * LOG2E
NEG = -1e30
SOFTMAX_ROWS = 32
BANDED_SOFTMAX_ROWS = 32
VMEM_LIMIT_BYTES = 56 * 1024 * 1024

MODE_PLAIN, MODE_ROPE, MODE_ROPE_SCALED = 0, 1, 2

F32 = jnp.float32
BF16 = jnp.bfloat16


def _params(*semantics):
    return pltpu.CompilerParams(dimension_semantics=semantics, vmem_limit_bytes=VMEM_LIMIT_BYTES)


def _dot(a, b):
    return jnp.dot(a, b, preferred_element_type=F32)


def _dot_nt(a, b):
    return lax.dot_general(a, b, (((1,), (1,)), ((), ())), preferred_element_type=F32)


def _rope_table_kernel(pos_ref, inv_ref, sign_ref, cos_ref, sin_ref):
    ang = pos_ref[...].astype(F32) * inv_ref[...]
    cos_ref[...] = jnp.cos(ang)
    sin_ref[...] = jnp.sin(ang) * sign_ref[...]


def _rope_tables(positions):
    m = positions.size
    tm = min(m, 2048)
    inv = ROPE_THETA ** (-jnp.arange(0, HEAD_DIM, 2, dtype=F32) / HEAD_DIM)
    inv_full = jnp.concatenate([inv, inv]).reshape(1, HEAD_DIM)
    half = HEAD_DIM // 2
    sign = jnp.concatenate([-jnp.ones((half,), F32), jnp.ones((half,), F32)]).reshape(1, HEAD_DIM)
    row = pl.BlockSpec((tm, HEAD_DIM), lambda i: (i, 0))
    const = pl.BlockSpec((1, HEAD_DIM), lambda i: (0, 0))
    return pl.pallas_call(
        _rope_table_kernel,
        grid=(m // tm,),
        in_specs=[pl.BlockSpec((tm, 1), lambda i: (i, 0)), const, const],
        out_specs=[row, row],
        out_shape=[jax.ShapeDtypeStruct((m, HEAD_DIM), F32)] * 2,
        compiler_params=_params("parallel"),
        name="rope_tables",
    )(positions.reshape(m, 1), inv_full, sign)


def _proj_kernel(modes_ref, x_ref, w_ref, cos_ref, sin_ref, o_ref, *scratch, tn, dil):
    mode = modes_ref[pl.program_id(1)]
    acc = _dot(x_ref[...], w_ref[...])
    blocks = [slice(blk * HEAD_DIM, (blk + 1) * HEAD_DIM) for blk in range(tn // HEAD_DIM)]

    def put(blk, val):
        if dil == 1:
            o_ref[:, blocks[blk]] = val.astype(o_ref.dtype)
        else:
            scratch[0][blk] = val

    @pl.when(mode == MODE_PLAIN)
    def _():
        for blk, cols in enumerate(blocks):
            put(blk, acc[:, cols])

    @pl.when(mode != MODE_PLAIN)
    def _():
        scale = jnp.where(mode == MODE_ROPE_SCALED, QK_SCALE, 1.0).astype(F32)
        cos = cos_ref[...] * scale
        sin = sin_ref[...] * scale
        for blk, cols in enumerate(blocks):
            t = acc[:, cols]
            put(blk, t * cos + pltpu.roll(t, HEAD_DIM // 2, 1) * sin)

    if dil > 1:
        rows = o_ref.shape[1]
        for c in range(dil):
            for blk, cols in enumerate(blocks):
                o_ref[c, :, cols] = scratch[0][blk, pl.ds(c, rows, stride=dil), :].astype(o_ref.dtype)


def _project(x, w, modes, cos, sin, *, seq, dil=1, tm=1024, tn=1024):
    m, k = x.shape
    n = w.shape[1]
    tm = min(tm, seq)
    tiles_per_batch = seq // tm
    bsz = m // seq
    if dil == 1:
        out_spec = pl.BlockSpec((tm, tn), lambda i, j, modes: (i, j))
        out_shape = jax.ShapeDtypeStruct((m, n), BF16)
        scratch = []
    else:
        out_spec = pl.BlockSpec((None, dil, tm // dil, tn),
                                lambda i, j, modes: (i // tiles_per_batch, 0, i % tiles_per_batch, j))
        out_shape = jax.ShapeDtypeStruct((bsz, dil, seq // dil, n), BF16)
        scratch = [pltpu.VMEM((tn // HEAD_DIM, tm, HEAD_DIM), F32)]
    grid_spec = pltpu.PrefetchScalarGridSpec(
        num_scalar_prefetch=1,
        grid=(m // tm, n // tn),
        in_specs=[
            pl.BlockSpec((tm, k), lambda i, j, modes: (i, 0)),
            pl.BlockSpec((k, tn), lambda i, j, modes: (0, j)),
            pl.BlockSpec((tm, HEAD_DIM), lambda i, j, modes: (i, 0)),
            pl.BlockSpec((tm, HEAD_DIM), lambda i, j, modes: (i, 0)),
        ],
        out_specs=out_spec,
        scratch_shapes=scratch,
    )
    out = pl.pallas_call(
        functools.partial(_proj_kernel, tn=tn, dil=dil),
        grid_spec=grid_spec,
        out_shape=out_shape,
        compiler_params=_params("parallel", "arbitrary"),
        name="in_proj" if dil == 1 else "in_proj_dil%d" % dil,
    )(modes, x, w, cos, sin)
    return out.reshape(bsz, 1, seq, n) if dil == 1 else out


def _gate_kernel(x_ref, w_ref, o_ref):
    o_ref[...] = jax.nn.sigmoid(_dot(x_ref[...], w_ref[...]))


def _gate_project(x, w, *, tm=1024):
    m, k = x.shape
    n = w.shape[1]
    tm = min(tm, m)
    return pl.pallas_call(
        _gate_kernel,
        grid=(m // tm,),
        in_specs=[pl.BlockSpec((tm, k), lambda i: (i, 0)), pl.BlockSpec((k, n), lambda i: (0, 0))],
        out_specs=pl.BlockSpec((tm, n), lambda i: (i, 0)),
        out_shape=jax.ShapeDtypeStruct((m, n), F32),
        compiler_params=_params("parallel"),
        name="gate_proj",
    )(x, w)


def _residual_layer_norm(res, y, g, b):
    z = DN_ALPHA * res + y
    mu = jnp.mean(z, axis=-1, keepdims=True)
    zc = z - mu
    var = jnp.mean(zc * zc, axis=-1, keepdims=True)
    return zc * lax.rsqrt(var + LN_EPS) * g + b


def _out_proj_kernel(*refs, n_in):
    x_refs, w_refs = refs[:n_in], refs[n_in:2 * n_in]
    res_ref, g_ref, b_ref, of_ref, ob_ref = refs[2 * n_in:]
    y = _dot(x_refs[0][...], w_refs[0][...])
    for x_ref, w_ref in zip(x_refs[1:], w_refs[1:]):
        y += _dot(x_ref[...], w_ref[...])
    out = _residual_layer_norm(res_ref[...], y, g_ref[...], b_ref[...])
    of_ref[...] = out
    ob_ref[...] = out.astype(BF16)


def _out_proj_ln(xs, ws, res, g, b, *, tm=512):
    m, d = res.shape
    tm = min(tm, m)
    n_in = len(xs)
    in_specs = [pl.BlockSpec((tm, x.shape[1]), lambda i: (i, 0)) for x in xs]
    in_specs += [pl.BlockSpec(w.shape, lambda i: (0, 0)) for w in ws]
    row = pl.BlockSpec((tm, d), lambda i: (i, 0))
    vec = pl.BlockSpec((1, d), lambda i: (0, 0))
    in_specs += [row, vec, vec]
    return pl.pallas_call(
        functools.partial(_out_proj_kernel, n_in=n_in),
        grid=(m // tm,),
        in_specs=in_specs,
        out_specs=[row, row],
        out_shape=[jax.ShapeDtypeStruct((m, d), F32), jax.ShapeDtypeStruct((m, d), BF16)],
        compiler_params=_params("parallel"),
        name="out_proj_ln",
    )(*xs, *ws, res, g.reshape(1, d), b.reshape(1, d))


def _mlp_kernel(xb_ref, w1_ref, w2_ref, xf_ref, g_ref, b_ref, of_ref, ob_ref, acc_ref):
    f = pl.program_id(1)
    h = _dot(xb_ref[...], w1_ref[...])
    h = jnp.square(jnp.maximum(h, 0.0)).astype(BF16)
    part = _dot(h, w2_ref[...])

    @pl.when(f == 0)
    def _():
        acc_ref[...] = part

    @pl.when(f > 0)
    def _():
        acc_ref[...] += part

    @pl.when(f == pl.num_programs(1) - 1)
    def _():
        out = _residual_layer_norm(xf_ref[...], acc_ref[...], g_ref[...], b_ref[...])
        of_ref[...] = out
        ob_ref[...] = out.astype(BF16)


def _mlp_ln(xf, xb, w1, w2, g, b, *, tm=512, tf=1024):
    m, d = xf.shape
    ff = w1.shape[1]
    tm = min(tm, m)
    row = pl.BlockSpec((tm, d), lambda i, f: (i, 0))
    vec = pl.BlockSpec((1, d), lambda i, f: (0, 0))
    return pl.pallas_call(
        _mlp_kernel,
        grid=(m // tm, ff // tf),
        in_specs=[row, pl.BlockSpec((d, tf), lambda i, f: (0, f)), pl.BlockSpec((tf, d), lambda i, f: (f, 0)),
                  row, vec, vec],
        out_specs=[row, row],
        out_shape=[jax.ShapeDtypeStruct((m, d), F32), jax.ShapeDtypeStruct((m, d), BF16)],
        scratch_shapes=[pltpu.VMEM((tm, d), F32)],
        compiler_params=_params("parallel", "arbitrary"),
        name="mlp_ln",
    )(xb, w1, w2, xf, g.reshape(1, d), b.reshape(1, d))


def _banded_kernel(*refs, hq, grp, window, tq, pw, has_sink, want_lse):
    refs = list(refs)
    sink_ref = refs.pop(0) if has_sink else None
    q_ref, kc_ref, kp_ref, vc_ref, vp_ref, o_ref = refs[:6]
    lse_ref = refs[6] if want_lse else None
    bias_ref = refs[-1]
    score_refs = refs[-1 - 3 * hq:-1 - 2 * hq]
    prob_refs = refs[-1 - 2 * hq:-1 - hq]
    linv_refs = refs[-1 - hq:-1]
    step, i = pl.program_id(1), pl.program_id(2)
    span = pw + tq
    rows_blk = min(BANDED_SOFTMAX_ROWS, tq)

    row = lax.broadcasted_iota(jnp.int32, (tq, span), 0)
    rel = lax.broadcasted_iota(jnp.int32, (tq, span), 1) - pw - row
    mask = (rel <= 0) & (rel > -window) & ((rel + row >= 0) | (i > 0))
    bias_ref[...] = jnp.where(mask, 0.0, NEG)

    @pl.when(i == 0)
    def _():
        for h in range(hq):
            prob_refs[h][...] = jnp.zeros(prob_refs[h].shape, prob_refs[h].dtype)

    for h in range(hq):
        kv = h // grp
        q = q_ref[:, h * HEAD_DIM:(h + 1) * HEAD_DIM]
        kcols = slice(kv * HEAD_DIM, (kv + 1) * HEAD_DIM)
        score_refs[h][:, :pw] = _dot_nt(q, kp_ref[:, kcols])
        score_refs[h][:, pw:] = _dot_nt(q, kc_ref[:, kcols])

    for h in range(hq):
        kv = h // grp
        qcols = slice(h * HEAD_DIM, (h + 1) * HEAD_DIM)
        kcols = slice(kv * HEAD_DIM, (kv + 1) * HEAD_DIM)
        if has_sink:
            sink = sink_ref[step * hq + h] * LOG2E
        for rb in range(tq // rows_blk):
            r0 = rb * rows_blk
            rows = slice(r0, r0 + rows_blk)
            cols = slice(max(0, r0 + pw - (window - 1)) // HEAD_DIM * HEAD_DIM,
                         -(-(r0 + rows_blk + pw) // HEAD_DIM) * HEAD_DIM)
            s = score_refs[h][rows, cols] + bias_ref[rows, cols]
            m = jnp.max(s, axis=-1, keepdims=True)
            if has_sink:
                m = jnp.maximum(m, sink)
            p = jnp.exp2(s - m)
            l = jnp.sum(p, axis=-1, keepdims=True)
            if has_sink:
                l = l + jnp.exp2(sink - m)
            prob_refs[h][rows, cols] = p.astype(BF16)
            linv_refs[h][rows, :] = jnp.broadcast_to(1.0 / l, (rows_blk, HEAD_DIM))
            if want_lse:
                lse_ref[rows, h:h + 1] = (m + jnp.log2(l)) * LN2
        o = _dot(prob_refs[h][:, :pw], vp_ref[:, kcols]) + _dot(prob_refs[h][:, pw:], vc_ref[:, kcols])
        o_ref[:, qcols] = (o * linv_refs[h][...]).astype(o_ref.dtype)


def _banded_attention(arr, *, n_steps, hq, grp, window, q_at, k_at, v_at, o_at, o_dims, tq, pw,
                      sinks=None, want_lse=False):
    bsz, _, seq, _ = arr.shape
    tq = min(tq, seq)
    pw = min(pw, tq)
    assert pw >= window - 1 or seq <= pw, (pw, window, seq)
    assert seq % tq == 0 and tq % pw == 0
    hk = hq // grp
    ratio = tq // pw
    wq, wk = hq * HEAD_DIM, hk * HEAD_DIM
    has_sink = sinks is not None

    def spec(width, rows, at_fn, row_fn):
        def index(b, s, i, *_):
            plane, col = at_fn(s)
            return (b, plane, row_fn(i), col)
        return pl.BlockSpec((None, None, rows, width), index)

    cur = lambda i: i
    prev = lambda i: jnp.maximum(i * ratio - 1, 0)
    in_specs = [spec(wq, tq, q_at, cur), spec(wk, tq, k_at, cur), spec(wk, pw, k_at, prev),
                spec(wk, tq, v_at, cur), spec(wk, pw, v_at, prev)]
    out_specs = [spec(wq, tq, o_at, cur)]
    out_shape = [jax.ShapeDtypeStruct((bsz, o_dims[0], seq, o_dims[1]), BF16)]
    if want_lse:
        out_specs.append(pl.BlockSpec((None, None, tq, hq), lambda b, s, i: (b, s, i, 0)))
        out_shape.append(jax.ShapeDtypeStruct((bsz, n_steps, seq, hq), F32))
    kern = functools.partial(_banded_kernel, hq=hq, grp=grp, window=window, tq=tq, pw=pw,
                             has_sink=has_sink, want_lse=want_lse)
    grid = (bsz, n_steps, seq // tq)
    params = _params("parallel", "parallel", "arbitrary")
    scratch = ([pltpu.VMEM((tq, pw + tq), F32)] * hq + [pltpu.VMEM((tq, pw + tq), BF16)] * hq
               + [pltpu.VMEM((tq, HEAD_DIM), F32)] * hq + [pltpu.VMEM((tq, pw + tq), F32)])
    if has_sink:
        grid_spec = pltpu.PrefetchScalarGridSpec(num_scalar_prefetch=1, grid=grid, in_specs=in_specs,
                                                 out_specs=out_specs, scratch_shapes=scratch)
        out = pl.pallas_call(kern, grid_spec=grid_spec, out_shape=out_shape, compiler_params=params,
                             name="banded_attn_sink")(sinks.astype(F32), arr, arr, arr, arr, arr)
    else:
        out = pl.pallas_call(kern, grid=grid, in_specs=in_specs, out_specs=out_specs, out_shape=out_shape,
                             scratch_shapes=scratch, compiler_params=params, name="banded_attn")(arr, arr, arr, arr, arr)
    return out if want_lse else out[0]


def _mix_kernel(o0_ref, o1_ref, o2_ref, l0_ref, l1_ref, l2_ref, out_ref, t1_ref, t2_ref):
    for src, dst in ((o1_ref, t1_ref), (o2_ref, t2_ref)):
        dil, rows = src.shape[0], src.shape[1]
        for c in range(dil):
            for h in range(DIL_HEADS):
                dst[h, pl.ds(c, rows, stride=dil), :] = src[c, :, h * HEAD_DIM:(h + 1) * HEAD_DIM].astype(F32)
    l0, l1, l2 = l0_ref[...], l1_ref[...], l2_ref[...]
    m = jnp.maximum(jnp.maximum(l0, l1), l2)
    e0, e1, e2 = jnp.exp(l0 - m), jnp.exp(l1 - m), jnp.exp(l2 - m)
    inv = 1.0 / (e0 + e1 + e2)
    w0, w1, w2 = e0 * inv, e1 * inv, e2 * inv
    for h in range(DIL_HEADS):
        cols = slice(h * HEAD_DIM, (h + 1) * HEAD_DIM)
        out = (w0[:, h:h + 1] * o0_ref[:, cols].astype(F32)
               + w1[:, h:h + 1] * t1_ref[h]
               + w2[:, h:h + 1] * t2_ref[h])
        out_ref[:, cols] = out.astype(out_ref.dtype)


def _mix_groups(outs, lses, *, seq, tm=512):
    bsz, _, _, d = outs[0].shape
    tm = min(tm, seq)
    tiles_per_batch = seq // tm

    def o_spec(dil):
        return pl.BlockSpec((None, dil, tm // dil, d), lambda i: (i // tiles_per_batch, 0, i % tiles_per_batch, 0))

    row = pl.BlockSpec((tm, d), lambda i: (i, 0))
    lrow = pl.BlockSpec((tm, DIL_HEADS), lambda i: (i, 0))
    dils = [o.shape[1] for o in outs]
    assert dils[0] == 1
    return pl.pallas_call(
        _mix_kernel,
        grid=(bsz * tiles_per_batch,),
        in_specs=[row, o_spec(dils[1]), o_spec(dils[2]), lrow, lrow, lrow],
        out_specs=row,
        out_shape=jax.ShapeDtypeStruct((bsz * seq, d), BF16),
        scratch_shapes=[pltpu.VMEM((DIL_HEADS, tm, HEAD_DIM), F32)] * 2,
        compiler_params=_params("parallel"),
        name="dilated_mix",
    )(outs[0].reshape(bsz * seq, d), outs[1], outs[2], *lses)


def _compress_kernel(x_ref, pe_lo_ref, pe_hi_ref, w1_lo_ref, w1_hi_ref, w2_ref, o_ref, shift_ref):
    n = x_ref.shape[0]
    x = x_ref[...].astype(F32)
    first = _dot((x + pe_lo_ref[...]).astype(BF16), w1_lo_ref[...])
    second = _dot((x + pe_hi_ref[...]).astype(BF16), w1_hi_ref[...])
    shift_ref[0:n, :] = second
    shift_ref[n:n + 8, :] = jnp.zeros((8, HEAD_DIM), F32)
    h = first + shift_ref[1:n + 1, :]
    o_ref[...] = _dot(jax.nn.gelu(h).astype(BF16), w2_ref[...]).astype(o_ref.dtype)


def _compress(rows, pe, w1, w2):
    bg, n, width = rows.shape
    half = CMP_STRIDE * HEAD_DIM
    pe_lo = pe[:CMP_STRIDE].reshape(1, half)
    pe_hi = pe[CMP_STRIDE:].reshape(1, half)
    w1_lo = w1[:half].astype(BF16)
    w1_hi = w1[half:].astype(BF16)
    const = lambda shape: pl.BlockSpec(shape, lambda i: (0, 0))
    return pl.pallas_call(
        _compress_kernel,
        grid=(bg,),
        in_specs=[pl.BlockSpec((None, n, width), lambda i: (i, 0, 0)), const((1, half)), const((1, half)),
                  const((half, HEAD_DIM)), const((half, HEAD_DIM)), const((HEAD_DIM, HEAD_DIM))],
        out_specs=pl.BlockSpec((None, n, HEAD_DIM), lambda i: (i, 0, 0)),
        out_shape=jax.ShapeDtypeStruct((bg, n, HEAD_DIM), BF16),
        scratch_shapes=[pltpu.VMEM((n + 8, HEAD_DIM), F32)],
        compiler_params=_params("parallel"),
        name="nsa_compress",
    )(rows, pe_lo, pe_hi, w1_lo, w1_hi, w2.astype(BF16))


def _nsa_kernel(q_ref, kc_ref, vc_ref, c2s_ref, ks_ref, vs_ref, ow_ref, g_ref, o_ref,
                kaug_ref, qaug_ref, s0_ref, s1_ref, p0_ref, p1_ref, a0_ref, a1_ref, m_ref, l_ref, acc_ref,
                *, tq, kt, n_blk, n_sel):
    grp = NSA_GROUP
    n_cmp = kc_ref.shape[0]
    seq = ks_ref.shape[0]
    n_halves = qaug_ref.shape[0]
    chunks_per_half = HEAD_DIM * SEL_BLOCK // kt
    q0 = pl.program_id(2) * tq

    @pl.when(pl.program_id(2) == 0)
    def _():
        def fill(r, carry):
            r0 = pl.multiple_of(r * kt, kt)
            blk_of_row = (lax.broadcasted_iota(jnp.int32, (kt, HEAD_DIM), 0) + r0) // SEL_BLOCK
            lane = lax.broadcasted_iota(jnp.int32, (kt, HEAD_DIM), 1)
            kaug_ref[pl.ds(r0, kt), 0:HEAD_DIM] = ks_ref[pl.ds(r0, kt), :]
            kaug_ref[pl.ds(r0, kt), HEAD_DIM:2 * HEAD_DIM] = jnp.where(
                lane == blk_of_row % HEAD_DIM, NEG, 0.0).astype(kaug_ref.dtype)
            return carry
        lax.fori_loop(0, seq // kt, fill, 0)

    qs = jnp.concatenate([q_ref[:, h * HEAD_DIM:(h + 1) * HEAD_DIM] for h in range(grp)], axis=0)
    qpos = q0 + lax.broadcasted_iota(jnp.int32, (tq, 1), 0)

    cmp_end = lax.broadcasted_iota(jnp.int32, (tq, n_cmp), 1) * CMP_STRIDE + (CMP_BLOCK - 1)
    cmp_bias = jnp.where(cmp_end <= qpos, 0.0, NEG)
    s = _dot_nt(qs, kc_ref[...]).reshape(grp, tq, n_cmp) + cmp_bias[None]
    m = jnp.max(s, axis=-1, keepdims=True)
    m = jnp.where(m < 0.5 * NEG, 0.0, m)
    p = jnp.exp2(s - m)
    l = jnp.sum(p, axis=-1, keepdims=True)
    p = p * (1.0 / jnp.where(l > 0.0, l, 1.0))
    o_cmp = _dot(p.reshape(grp * tq, n_cmp).astype(BF16), vc_ref[...])

    p_sum = jnp.sum(p, axis=0)
    p_hi = p_sum.astype(BF16)
    p_lo = (p_sum - p_hi.astype(F32)).astype(BF16)
    imp = _dot_nt(c2s_ref[...], p_hi) + _dot_nt(c2s_ref[...], p_lo)

    n_pad = c2s_ref.shape[0]
    blk = lax.broadcasted_iota(jnp.int32, (n_pad, tq), 0)
    qpos_lane = q0 + lax.broadcasted_iota(jnp.int32, (1, tq), 1)
    cur = qpos_lane // SEL_BLOCK
    forced = (blk == 0) | (blk == cur) | (blk == cur - 1)
    score = jnp.where(forced, jnp.inf, jnp.where(blk * SEL_BLOCK <= qpos_lane, imp, -1.0))
    blk_f = blk.astype(F32)
    not_sel = jnp.ones((n_pad, tq), F32)
    for _ in range(n_sel):
        best = jnp.max(score, axis=0, keepdims=True)
        first = jnp.min(jnp.where(score == best, blk_f, float(n_pad)), axis=0, keepdims=True)
        hit = blk_f == first
        not_sel = jnp.where(hit, 0.0, not_sel)
        score = jnp.where(hit, -2.0, score)
    not_sel = not_sel.T
    for half in range(n_halves):
        flags = not_sel[:, half * HEAD_DIM:(half + 1) * HEAD_DIM].astype(qaug_ref.dtype)
        qaug_ref[half, :, 0:HEAD_DIM] = qs
        qaug_ref[half, :, HEAD_DIM:2 * HEAD_DIM] = jnp.concatenate([flags] * grp, axis=0)

    rows_blk = SOFTMAX_ROWS
    bufs = ((s0_ref, p0_ref, a0_ref), (s1_ref, p1_ref, a1_ref))
    m_ref[...] = jnp.full(m_ref.shape, NEG, F32)
    l_ref[...] = jnp.zeros(l_ref.shape, F32)
    acc_ref[...] = jnp.zeros(acc_ref.shape, F32)

    def scores(j, slot):
        k0 = pl.multiple_of(j * kt, kt)
        bufs[slot][0][...] = _dot_nt(qaug_ref[j // chunks_per_half], kaug_ref[pl.ds(k0, kt), :])

    def softmax(j, slot, causal):
        s_buf, p_buf, a_buf = bufs[slot]
        k0 = j * kt
        for rb in range(grp * tq // rows_blk):
            rows = slice(rb * rows_blk, (rb + 1) * rows_blk)
            sb = s_buf[rows, :]
            if causal:
                row_pos = q0 + (rb * rows_blk) % tq + lax.broadcasted_iota(jnp.int32, (rows_blk, 1), 0)
                key_pos = k0 + lax.broadcasted_iota(jnp.int32, (rows_blk, kt), 1)
                sb = jnp.where(key_pos <= row_pos, sb, NEG)
            m_prev = m_ref[rows, :]
            m_new = jnp.maximum(m_prev, jnp.max(sb, axis=-1, keepdims=True))
            alpha = jnp.exp2(m_prev - m_new)
            p = jnp.exp2(sb - jnp.concatenate([m_new] * (kt // HEAD_DIM), axis=1))
            l_ref[rows, :] = alpha * l_ref[rows, :] + jnp.sum(p, axis=-1, keepdims=True)
            m_ref[rows, :] = m_new
            a_buf[rows, :] = alpha
            p_buf[rows, :] = p.astype(p_buf.dtype)

    def weighted_values(j, slot):
        _, p_buf, a_buf = bufs[slot]
        k0 = pl.multiple_of(j * kt, kt)
        acc_ref[...] = a_buf[...] * acc_ref[...] + _dot(p_buf[...], vs_ref[pl.ds(k0, kt), :])

    n_interior = q0 // kt
    scores(0, 0)

    def interior_pair(jj, carry):
        j = 2 * jj
        scores(j + 1, 1)
        softmax(j, 0, causal=False)
        weighted_values(j, 0)
        scores(j + 2, 0)
        softmax(j + 1, 1, causal=False)
        weighted_values(j + 1, 1)
        return carry

    lax.fori_loop(0, n_interior // 2, interior_pair, 0)

    @pl.when(n_interior % 2 == 0)
    def _():
        softmax(n_interior, 0, causal=True)
        weighted_values(n_interior, 0)

    @pl.when(n_interior % 2 == 1)
    def _():
        scores(n_interior, 1)
        softmax(n_interior - 1, 0, causal=False)
        weighted_values(n_interior - 1, 0)
        softmax(n_interior, 1, causal=True)
        weighted_values(n_interior, 1)

    o_slc = acc_ref[...] * (1.0 / l_ref[...])

    gates = g_ref[...]
    for h in range(grp):
        rows = slice(h * tq, (h + 1) * tq)
        cols = slice(h * HEAD_DIM, (h + 1) * HEAD_DIM)
        out = (gates[:, 3 * h:3 * h + 1] * o_cmp[rows]
               + gates[:, 3 * h + 1:3 * h + 2] * o_slc[rows]
               + gates[:, 3 * h + 2:3 * h + 3] * ow_ref[:, cols].astype(F32))
        o_ref[:, cols] = out.astype(o_ref.dtype)


def _sel_from_cmp(n_cmp_pad, n_blk_pad):
    pos = np.arange(n_cmp_pad)[:, None] * CMP_STRIDE + np.arange(CMP_BLOCK)[None, :]
    owner = pos // SEL_BLOCK
    frac = (owner[:, :, None] == np.arange(n_blk_pad)[None, None, :]).sum(axis=1) / CMP_BLOCK
    return jnp.asarray(frac.T, dtype=BF16)


def _nsa_attention(proj, kc, vc, o_win, gates, *, q_idx, ks_idx, vs_idx, tq=256, kt=512):
    bsz, seq, _ = proj.shape
    tq = min(tq, seq)
    kt = min(kt, seq)
    assert kt % tq == 0 and (HEAD_DIM * SEL_BLOCK) % kt == 0 and seq % kt == 0
    n_cmp = kc.shape[1]
    n_blk = seq // SEL_BLOCK
    n_sel = min(N_SELECT, n_blk)
    n_halves = -(-n_blk // HEAD_DIM)
    wq = NSA_GROUP * HEAD_DIM
    rows = NSA_GROUP * tq
    c2s = _sel_from_cmp(n_cmp, n_halves * HEAD_DIM)
    kern = functools.partial(_nsa_kernel, tq=tq, kt=kt, n_blk=n_blk, n_sel=n_sel)
    full = lambda idx_fn: pl.BlockSpec((None, seq, HEAD_DIM), lambda b, g, i: (b, 0, idx_fn(g)))
    cmp_spec = pl.BlockSpec((None, n_cmp, HEAD_DIM), lambda b, g, i: (b * NSA_KV_HEADS + g, 0, 0))
    return pl.pallas_call(
        kern,
        grid=(bsz, NSA_KV_HEADS, seq // tq),
        in_specs=[
            pl.BlockSpec((None, tq, wq), lambda b, g, i: (b, i, q_idx(g))),
            cmp_spec, cmp_spec,
            pl.BlockSpec(c2s.shape, lambda b, g, i: (0, 0)),
            full(ks_idx), full(vs_idx),
            pl.BlockSpec((None, tq, wq), lambda b, g, i: (b, i, g)),
            pl.BlockSpec((None, tq, HEAD_DIM), lambda b, g, i: (b, i, g)),
        ],
        out_specs=pl.BlockSpec((None, tq, wq), lambda b, g, i: (b, i, g)),
        out_shape=jax.ShapeDtypeStruct((bsz, seq, NSA_KV_HEADS * wq), BF16),
        scratch_shapes=[pltpu.VMEM((seq, 2 * HEAD_DIM), BF16),
                        pltpu.VMEM((n_halves, rows, 2 * HEAD_DIM), BF16),
                        pltpu.VMEM((rows, kt), F32), pltpu.VMEM((rows, kt), F32),
                        pltpu.VMEM((rows, kt), BF16), pltpu.VMEM((rows, kt), BF16)]
                       + [pltpu.VMEM((rows, HEAD_DIM), F32)] * 5,
        compiler_params=_params("parallel", "parallel", "arbitrary"),
        name="nsa_core",
    )(proj, kc, vc, c2s, proj, proj, o_win, gates)


_BLK_AQ, _BLK_BQ = 0, NSA_HEADS
_BLK_AK = NSA_HEADS + SWA_HEADS
_BLK_BK = _BLK_AK + 3 * NSA_KV_HEADS
_BLK_AV = _BLK_BK + SWA_KV_HEADS
_BLK_BV = _BLK_AV + 3 * NSA_KV_HEADS


def _even_weights(w_in):
    aq, ak, av, ag, bq, bk, bv = jnp.split(w_in, np.cumsum(EVEN_WIDTHS)[:-1].tolist(), axis=-1)
    w_main = jnp.concatenate([aq, bq, ak, bk, av, bv], axis=-1).astype(BF16)
    ag = ag.reshape(D_MODEL, NSA_KV_HEADS, 3 * NSA_GROUP)
    w_gate = jnp.pad(ag, ((0, 0), (0, 0), (0, HEAD_DIM - 3 * NSA_GROUP))).reshape(D_MODEL, NSA_KV_HEADS * HEAD_DIM)
    return w_main, w_gate.astype(BF16)


def _tile_modes(blocks_per_tile, block_modes):
    modes = np.asarray(block_modes, np.int32).reshape(-1, blocks_per_tile)
    assert (modes == modes[:, :1]).all(), "a projection tile must not straddle q/k/v column groups"
    return jnp.asarray(modes[:, 0])


def _even_mixer(xf, xb, cos, sin, bsz, seq, w_in, w_o, pe_k, pe_v, ck_w1, ck_w2, cv_w1, cv_w2, sinks, g, b):
    w_main, w_gate = _even_weights(w_in)
    tn = 1024
    n_q, n_k = NSA_HEADS + SWA_HEADS, 3 * NSA_KV_HEADS + SWA_KV_HEADS
    modes = _tile_modes(tn // HEAD_DIM, [MODE_ROPE_SCALED] * n_q + [MODE_ROPE] * n_k + [MODE_PLAIN] * n_k)
    proj4 = _project(xb, w_main, modes, cos, sin, seq=seq, tn=tn)
    proj = proj4.reshape(bsz, seq, EVEN_PROJ)
    gates = _gate_project(xb, w_gate).reshape(bsz, seq, NSA_KV_HEADS * HEAD_DIM)

    def cmp_rows(blk0):
        t = proj[:, :, blk0 * HEAD_DIM:(blk0 + NSA_KV_HEADS) * HEAD_DIM]
        t = t.reshape(bsz, seq // CMP_STRIDE, CMP_STRIDE, NSA_KV_HEADS, HEAD_DIM).transpose(0, 3, 1, 2, 4)
        return t.reshape(bsz * NSA_KV_HEADS, seq // CMP_STRIDE, CMP_STRIDE * HEAD_DIM)

    kc = _compress(cmp_rows(_BLK_AK), pe_k, ck_w1, ck_w2)
    vc = _compress(cmp_rows(_BLK_AV), pe_v, cv_w1, cv_w2)
    o_win = _banded_attention(
        proj4, n_steps=NSA_KV_HEADS, hq=NSA_GROUP, grp=NSA_GROUP, window=NSA_WINDOW,
        q_at=lambda s: (0, s), k_at=lambda s: (0, _BLK_AK + 2 * NSA_KV_HEADS + s),
        v_at=lambda s: (0, _BLK_AV + 2 * NSA_KV_HEADS + s), o_at=lambda s: (0, s), o_dims=(1, NSA_Q),
        tq=512, pw=512).reshape(bsz, seq, NSA_Q)
    o_a = _nsa_attention(proj, kc, vc, o_win, gates, q_idx=lambda s: s,
                         ks_idx=lambda s: _BLK_AK + NSA_KV_HEADS + s, vs_idx=lambda s: _BLK_AV + NSA_KV_HEADS + s)
    swa_grp = SWA_HEADS // SWA_KV_HEADS
    o_b = _banded_attention(
        proj4, n_steps=SWA_KV_HEADS, hq=swa_grp, grp=swa_grp, window=SWA_WINDOW,
        q_at=lambda s: (0, NSA_HEADS // swa_grp + s), k_at=lambda s: (0, _BLK_BK + s),
        v_at=lambda s: (0, _BLK_BV + s), o_at=lambda s: (0, s), o_dims=(1, SWA_Q),
        tq=256, pw=128, sinks=sinks)
    m = bsz * seq
    w_o = w_o.astype(BF16)
    return _out_proj_ln([o_a.reshape(m, NSA_Q), o_b.reshape(m, SWA_Q)], [w_o[:NSA_Q], w_o[NSA_Q:]], xf, g, b)


def _odd_mixer(xf, xb, cos, sin, bsz, seq, w_in, w_o, g, b):
    tn = 1024
    modes = _tile_modes(tn // HEAD_DIM, [MODE_ROPE_SCALED] * DIL_HEADS + [MODE_ROPE] * DIL_HEADS + [MODE_PLAIN] * DIL_HEADS)
    m = bsz * seq
    width = DIL_HEADS * HEAD_DIM
    w_in = w_in.astype(BF16)
    outs, lses = [], []
    for grp_i, (window, dil) in enumerate(DIL_PATTERNS):
        qkv = _project(xb, w_in[:, grp_i * DIL_QKV:(grp_i + 1) * DIL_QKV], modes, cos, sin, seq=seq, dil=dil, tn=tn)
        o, lse = _banded_attention(
            qkv, n_steps=dil, hq=DIL_HEADS, grp=1, window=window // dil + 1,
            q_at=lambda s: (s, 0), k_at=lambda s: (s, 1), v_at=lambda s: (s, 2), o_at=lambda s: (s, 0),
            o_dims=(dil, width), tq=256, pw=128, want_lse=True)
        outs.append(o)
        lses.append(lse.transpose(0, 2, 1, 3).reshape(m, DIL_HEADS))
    mixed = _mix_groups(outs, lses, seq=seq)
    return _out_proj_ln([mixed], [w_o.astype(BF16)], xf, g, b)


def kernel(x, positions, e_w_in, e_w_o, nsa_pe_k, nsa_pe_v, nsa_ck_w1, nsa_ck_w2, nsa_cv_w1, nsa_cv_w2, swa_sinks, o_w_in, o_w_o, ln1_g, ln1_b, mlp_w1, mlp_w2, ln2_g, ln2_b):
    bsz, seq, d = x.shape
    m = bsz * seq
    cos, sin = _rope_tables(positions)
    xf = x.reshape(m, d)
    xb = xf.astype(BF16)
    for layer in range(DEPTH):
        i = layer // 2
        if layer % 2 == 0:
            xf, xb = _even_mixer(xf, xb, cos, sin, bsz, seq, e_w_in[i], e_w_o[i], nsa_pe_k[i], nsa_pe_v[i],
                                 nsa_ck_w1[i], nsa_ck_w2[i], nsa_cv_w1[i], nsa_cv_w2[i], swa_sinks[i],
                                 ln1_g[layer], ln1_b[layer])
        else:
            xf, xb = _odd_mixer(xf, xb, cos, sin, bsz, seq, o_w_in[i], o_w_o[i], ln1_g[layer], ln1_b[layer])
        xf, xb = _mlp_ln(xf, xb, mlp_w1[layer].astype(BF16), mlp_w2[layer].astype(BF16), ln2_g[layer], ln2_b[layer])
    return xf.reshape(bsz, seq, d)
```

```python
import functools
import math

import numpy as np
import jax
import jax.numpy as jnp
from jax import lax
from jax.experimental import pallas as pl
from jax.experimental.pallas import tpu as pltpu

D_MODEL = 2048
DEPTH = 4
HEAD_DIM = 128
ROPE_THETA = 10000.0
LN_EPS = 1e-5

NSA_HEADS = D_MODEL // (2 * HEAD_DIM)
NSA_KV_HEADS = 2
NSA_GROUP = NSA_HEADS // NSA_KV_HEADS
CMP_BLOCK = 32
CMP_STRIDE = 16
SEL_BLOCK = 64
N_SELECT = 16
NSA_WINDOW = 512

SWA_HEADS = D_MODEL // (2 * HEAD_DIM)
SWA_KV_HEADS = 2
SWA_WINDOW = 128

DIL_HEADS = D_MODEL // HEAD_DIM
DIL_PATTERNS = ((128, 1), (512, 4), (2048, 16))

D_FF = 4 * D_MODEL
DN_ALPHA = (2 * DEPTH) ** 0.25

NSA_Q = NSA_HEADS * HEAD_DIM
NSA_KV = NSA_KV_HEADS * HEAD_DIM
SWA_Q = SWA_HEADS * HEAD_DIM
SWA_KV = SWA_KV_HEADS * HEAD_DIM
EVEN_WIDTHS = (NSA_Q, 3 * NSA_KV, 3 * NSA_KV, 3 * NSA_HEADS, SWA_Q, SWA_KV, SWA_KV)
EVEN_PROJ = NSA_Q + SWA_Q + 3 * NSA_KV + SWA_KV + 3 * NSA_KV + SWA_KV
DIL_QKV = 3 * DIL_HEADS * HEAD_DIM

LOG2E = math.log2(math.e)
LN2 = math.log(2.0)
QK_SCALE = HEAD_DIM ** -0.5 * LOG2E
NEG = -1e30
SOFTMAX_ROWS = 32
BANDED_SOFTMAX_ROWS = 32
PROJ_SUB_ROWS = 256
VMEM_LIMIT_BYTES = 56 * 1024 * 1024

MODE_PLAIN, MODE_ROPE, MODE_ROPE_SCALED = 0, 1, 2

F32 = jnp.float32
BF16 = jnp.bfloat16


def _params(*semantics):
    return pltpu.CompilerParams(dimension_semantics=semantics, vmem_limit_bytes=VMEM_LIMIT_BYTES)


def _dot(a, b):
    return jnp.dot(a, b, preferred_element_type=F32)


def _dot_nt(a, b):
    return lax.dot_general(a, b, (((1,), (1,)), ((), ())), preferred_element_type=F32)


def _rope_table_kernel(pos_ref, inv_ref, sign_ref, cos_ref, sin_ref):
    ang = pos_ref[...].astype(F32) * inv_ref[...]
    cos_ref[...] = jnp.cos(ang)
    sin_ref[...] = jnp.sin(ang) * sign_ref[...]


def _rope_tables(positions):
    m = positions.size
    tm = min(m, 2048)
    inv = ROPE_THETA ** (-jnp.arange(0, HEAD_DIM, 2, dtype=F32) / HEAD_DIM)
    inv_full = jnp.concatenate([inv, inv]).reshape(1, HEAD_DIM)
    half = HEAD_DIM // 2
    sign = jnp.concatenate([-jnp.ones((half,), F32), jnp.ones((half,), F32)]).reshape(1, HEAD_DIM)
    row = pl.BlockSpec((tm, HEAD_DIM), lambda i: (i, 0))
    const = pl.BlockSpec((1, HEAD_DIM), lambda i: (0, 0))
    return pl.pallas_call(
        _rope_table_kernel,
        grid=(m // tm,),
        in_specs=[pl.BlockSpec((tm, 1), lambda i: (i, 0)), const, const],
        out_specs=[row, row],
        out_shape=[jax.ShapeDtypeStruct((m, HEAD_DIM), F32)] * 2,
        compiler_params=_params("parallel"),
        name="rope_tables",
    )(positions.reshape(m, 1), inv_full, sign)


def _proj_kernel(modes_ref, x_ref, w_ref, cos_ref, sin_ref, o_ref, *scratch, tn, dil):
    mode = modes_ref[pl.program_id(1)]
    tm = x_ref.shape[0]
    sub = min(PROJ_SUB_ROWS, tm)
    blocks = [slice(blk * HEAD_DIM, (blk + 1) * HEAD_DIM) for blk in range(tn // HEAD_DIM)]
    is_rope = mode != MODE_PLAIN
    scale = jnp.where(mode == MODE_ROPE_SCALED, QK_SCALE, 1.0).astype(F32)

    def epilogue(r, acc):
        rows = slice(r * sub, (r + 1) * sub)
        cos = jnp.where(is_rope, cos_ref[rows, :] * scale, 1.0)
        sin = jnp.where(is_rope, sin_ref[rows, :] * scale, 0.0)
        for blk, cols in enumerate(blocks):
            t = acc[:, cols]
            val = t * cos + pltpu.roll(t, HEAD_DIM // 2, 1) * sin
            if dil == 1:
                o_ref[rows, cols] = val.astype(o_ref.dtype)
            else:
                scratch[0][blk, rows, :] = val
        if dil > 1:
            n = sub // dil
            for c in range(dil):
                for blk, cols in enumerate(blocks):
                    o_ref[c, r * n:(r + 1) * n, cols] = scratch[0][
                        blk, pl.ds(r * sub + c, n, stride=dil), :].astype(o_ref.dtype)

    pending = None
    for r in range(tm // sub):
        acc = _dot(x_ref[r * sub:(r + 1) * sub, :], w_ref[...])
        if pending is not None:
            epilogue(*pending)
        pending = (r, acc)
    epilogue(*pending)


def _project(x, w, modes, cos, sin, *, seq, dil=1, tm=1024, tn=1024):
    m, k = x.shape
    n = w.shape[1]
    tm = min(tm, seq)
    tiles_per_batch = seq // tm
    bsz = m // seq
    if dil == 1:
        out_spec = pl.BlockSpec((tm, tn), lambda i, j, modes: (i, j))
        out_shape = jax.ShapeDtypeStruct((m, n), BF16)
        scratch = []
    else:
        out_spec = pl.BlockSpec((None, dil, tm // dil, tn),
                                lambda i, j, modes: (i // tiles_per_batch, 0, i % tiles_per_batch, j))
        out_shape = jax.ShapeDtypeStruct((bsz, dil, seq // dil, n), BF16)
        scratch = [pltpu.VMEM((tn // HEAD_DIM, tm, HEAD_DIM), F32)]
    grid_spec = pltpu.PrefetchScalarGridSpec(
        num_scalar_prefetch=1,
        grid=(m // tm, n // tn),
        in_specs=[
            pl.BlockSpec((tm, k), lambda i, j, modes: (i, 0)),
            pl.BlockSpec((k, tn), lambda i, j, modes: (0, j)),
            pl.BlockSpec((tm, HEAD_DIM), lambda i, j, modes: (i, 0)),
            pl.BlockSpec((tm, HEAD_DIM), lambda i, j, modes: (i, 0)),
        ],
        out_specs=out_spec,
        scratch_shapes=scratch,
    )
    out = pl.pallas_call(
        functools.partial(_proj_kernel, tn=tn, dil=dil),
        grid_spec=grid_spec,
        out_shape=out_shape,
        compiler_params=_params("parallel", "arbitrary"),
        name="in_proj" if dil == 1 else "in_proj_dil%d" % dil,
    )(modes, x, w, cos, sin)
    return out.reshape(bsz, 1, seq, n) if dil == 1 else out


def _gate_kernel(x_ref, w_ref, o_ref):
    o_ref[...] = jax.nn.sigmoid(_dot(x_ref[...], w_ref[...]))


def _gate_project(x, w, *, tm=1024):
    m, k = x.shape
    n = w.shape[1]
    tm = min(tm, m)
    return pl.pallas_call(
        _gate_kernel,
        grid=(m // tm,),
        in_specs=[pl.BlockSpec((tm, k), lambda i: (i, 0)), pl.BlockSpec((k, n), lambda i: (0, 0))],
        out_specs=pl.BlockSpec((tm, n), lambda i: (i, 0)),
        out_shape=jax.ShapeDtypeStruct((m, n), F32),
        compiler_params=_params("parallel"),
        name="gate_proj",
    )(x, w)


def _residual_layer_norm(res, y, g, b):
    z = DN_ALPHA * res + y
    mu = jnp.mean(z, axis=-1, keepdims=True)
    zc = z - mu
    var = jnp.mean(zc * zc, axis=-1, keepdims=True)
    return zc * lax.rsqrt(var + LN_EPS) * g + b


def _out_proj_kernel(*refs, n_in):
    x_refs, w_refs = refs[:n_in], refs[n_in:2 * n_in]
    res_ref, g_ref, b_ref, of_ref, ob_ref = refs[2 * n_in:]
    y = _dot(x_refs[0][...], w_refs[0][...])
    for x_ref, w_ref in zip(x_refs[1:], w_refs[1:]):
        y += _dot(x_ref[...], w_ref[...])
    out = _residual_layer_norm(res_ref[...], y, g_ref[...], b_ref[...])
    of_ref[...] = out
    ob_ref[...] = out.astype(BF16)


def _out_proj_ln(xs, ws, res, g, b, *, tm=512):
    m, d = res.shape
    tm = min(tm, m)
    n_in = len(xs)
    in_specs = [pl.BlockSpec((tm, x.shape[1]), lambda i: (i, 0)) for x in xs]
    in_specs += [pl.BlockSpec(w.shape, lambda i: (0, 0)) for w in ws]
    row = pl.BlockSpec((tm, d), lambda i: (i, 0))
    vec = pl.BlockSpec((1, d), lambda i: (0, 0))
    in_specs += [row, vec, vec]
    return pl.pallas_call(
        functools.partial(_out_proj_kernel, n_in=n_in),
        grid=(m // tm,),
        in_specs=in_specs,
        out_specs=[row, row],
        out_shape=[jax.ShapeDtypeStruct((m, d), F32), jax.ShapeDtypeStruct((m, d), BF16)],
        compiler_params=_params("parallel"),
        name="out_proj_ln",
    )(*xs, *ws, res, g.reshape(1, d), b.reshape(1, d))


def _mlp_kernel(xb_ref, w1_ref, w2_ref, xf_ref, g_ref, b_ref, of_ref, ob_ref, acc_ref):
    f = pl.program_id(1)
    tm = xb_ref.shape[0]
    sub = min(PROJ_SUB_ROWS, tm)
    @pl.when(f == 0)
    def _():
        acc_ref[...] = jnp.zeros(acc_ref.shape, F32)

    hidden = [None] * (tm // sub)
    for r in range(tm // sub + 1):
        if r < tm // sub:
            hidden[r] = _dot(xb_ref[r * sub:(r + 1) * sub, :], w1_ref[...])
        if r > 0:
            rows = slice((r - 1) * sub, r * sub)
            act = jnp.square(jnp.maximum(hidden[r - 1], 0.0)).astype(BF16)
            acc_ref[rows, :] += _dot(act, w2_ref[...])

    @pl.when(f == pl.num_programs(1) - 1)
    def _():
        out = _residual_layer_norm(xf_ref[...], acc_ref[...], g_ref[...], b_ref[...])
        of_ref[...] = out
        ob_ref[...] = out.astype(BF16)


def _mlp_ln(xf, xb, w1, w2, g, b, *, tm=512, tf=1024):
    m, d = xf.shape
    ff = w1.shape[1]
    tm = min(tm, m)
    row = pl.BlockSpec((tm, d), lambda i, f: (i, 0))
    vec = pl.BlockSpec((1, d), lambda i, f: (0, 0))
    return pl.pallas_call(
        _mlp_kernel,
        grid=(m // tm, ff // tf),
        in_specs=[row, pl.BlockSpec((d, tf), lambda i, f: (0, f)), pl.BlockSpec((tf, d), lambda i, f: (f, 0)),
                  row, vec, vec],
        out_specs=[row, row],
        out_shape=[jax.ShapeDtypeStruct((m, d), F32), jax.ShapeDtypeStruct((m, d), BF16)],
        scratch_shapes=[pltpu.VMEM((tm, d), F32)],
        compiler_params=_params("parallel", "arbitrary"),
        name="mlp_ln",
    )(xb, w1, w2, xf, g.reshape(1, d), b.reshape(1, d))


def _banded_kernel(*refs, hq, grp, window, tq, pw, has_sink, want_lse):
    refs = list(refs)
    sink_ref = refs.pop(0) if has_sink else None
    q_ref, kc_ref, kp_ref, vc_ref, vp_ref, o_ref = refs[:6]
    lse_ref = refs[6] if want_lse else None
    bias_ref = refs[-1]
    score_refs = refs[-1 - 3 * hq:-1 - 2 * hq]
    prob_refs = refs[-1 - 2 * hq:-1 - hq]
    linv_refs = refs[-1 - hq:-1]
    step, i = pl.program_id(1), pl.program_id(2)
    span = pw + tq
    rows_blk = min(BANDED_SOFTMAX_ROWS, tq)

    row = lax.broadcasted_iota(jnp.int32, (tq, span), 0)
    rel = lax.broadcasted_iota(jnp.int32, (tq, span), 1) - pw - row
    mask = (rel <= 0) & (rel > -window) & ((rel + row >= 0) | (i > 0))
    bias_ref[...] = jnp.where(mask, 0.0, NEG)

    @pl.when(i == 0)
    def _():
        for h in range(hq):
            prob_refs[h][...] = jnp.zeros(prob_refs[h].shape, prob_refs[h].dtype)

    for h in range(hq):
        kv = h // grp
        q = q_ref[:, h * HEAD_DIM:(h + 1) * HEAD_DIM]
        kcols = slice(kv * HEAD_DIM, (kv + 1) * HEAD_DIM)
        score_refs[h][:, :pw] = _dot_nt(q, kp_ref[:, kcols])
        score_refs[h][:, pw:] = _dot_nt(q, kc_ref[:, kcols])

    for h in range(hq):
        kv = h // grp
        qcols = slice(h * HEAD_DIM, (h + 1) * HEAD_DIM)
        kcols = slice(kv * HEAD_DIM, (kv + 1) * HEAD_DIM)
        if has_sink:
            sink = sink_ref[step * hq + h] * LOG2E
        for rb in range(tq // rows_blk):
            r0 = rb * rows_blk
            rows = slice(r0, r0 + rows_blk)
            cols = slice(max(0, r0 + pw - (window - 1)) // HEAD_DIM * HEAD_DIM,
                         -(-(r0 + rows_blk + pw) // HEAD_DIM) * HEAD_DIM)
            s = score_refs[h][rows, cols] + bias_ref[rows, cols]
            m = jnp.max(s, axis=-1, keepdims=True)
            if has_sink:
                m = jnp.maximum(m, sink)
            p = jnp.exp2(s - m)
            l = jnp.sum(p, axis=-1, keepdims=True)
            if has_sink:
                l = l + jnp.exp2(sink - m)
            prob_refs[h][rows, cols] = p.astype(BF16)
            linv_refs[h][rows, :] = jnp.broadcast_to(1.0 / l, (rows_blk, HEAD_DIM))
            if want_lse:
                lse_ref[rows, h:h + 1] = (m + jnp.log2(l)) * LN2
        o = _dot(prob_refs[h][:, :pw], vp_ref[:, kcols]) + _dot(prob_refs[h][:, pw:], vc_ref[:, kcols])
        o_ref[:, qcols] = (o * linv_refs[h][...]).astype(o_ref.dtype)


def _banded_attention(arr, *, n_steps, hq, grp, window, q_at, k_at, v_at, o_at, o_dims, tq, pw,
                      sinks=None, want_lse=False):
    bsz, _, seq, _ = arr.shape
    tq = min(tq, seq)
    pw = min(pw, tq)
    assert pw >= window - 1 or seq <= pw, (pw, window, seq)
    assert seq % tq == 0 and tq % pw == 0
    hk = hq // grp
    ratio = tq // pw
    wq, wk = hq * HEAD_DIM, hk * HEAD_DIM
    has_sink = sinks is not None

    def spec(width, rows, at_fn, row_fn):
        def index(b, s, i, *_):
            plane, col = at_fn(s)
            return (b, plane, row_fn(i), col)
        return pl.BlockSpec((None, None, rows, width), index)

    cur = lambda i: i
    prev = lambda i: jnp.maximum(i * ratio - 1, 0)
    in_specs = [spec(wq, tq, q_at, cur), spec(wk, tq, k_at, cur), spec(wk, pw, k_at, prev),
                spec(wk, tq, v_at, cur), spec(wk, pw, v_at, prev)]
    out_specs = [spec(wq, tq, o_at, cur)]
    out_shape = [jax.ShapeDtypeStruct((bsz, o_dims[0], seq, o_dims[1]), BF16)]
    if want_lse:
        out_specs.append(pl.BlockSpec((None, None, tq, hq), lambda b, s, i: (b, s, i, 0)))
        out_shape.append(jax.ShapeDtypeStruct((bsz, n_steps, seq, hq), F32))
    kern = functools.partial(_banded_kernel, hq=hq, grp=grp, window=window, tq=tq, pw=pw,
                             has_sink=has_sink, want_lse=want_lse)
    grid = (bsz, n_steps, seq // tq)
    params = _params("parallel", "parallel", "arbitrary")
    scratch = ([pltpu.VMEM((tq, pw + tq), F32)] * hq + [pltpu.VMEM((tq, pw + tq), BF16)] * hq
               + [pltpu.VMEM((tq, HEAD_DIM), F32)] * hq + [pltpu.VMEM((tq, pw + tq), F32)])
    if has_sink:
        grid_spec = pltpu.PrefetchScalarGridSpec(num_scalar_prefetch=1, grid=grid, in_specs=in_specs,
                                                 out_specs=out_specs, scratch_shapes=scratch)
        out = pl.pallas_call(kern, grid_spec=grid_spec, out_shape=out_shape, compiler_params=params,
                             name="banded_attn_sink")(sinks.astype(F32), arr, arr, arr, arr, arr)
    else:
        out = pl.pallas_call(kern, grid=grid, in_specs=in_specs, out_specs=out_specs, out_shape=out_shape,
                             scratch_shapes=scratch, compiler_params=params, name="banded_attn")(arr, arr, arr, arr, arr)
    return out if want_lse else out[0]


def _mix_kernel(o0_ref, o1_ref, o2_ref, l0_ref, l1_ref, l2_ref, out_ref, t1_ref, t2_ref):
    for src, dst in ((o1_ref, t1_ref), (o2_ref, t2_ref)):
        dil, rows = src.shape[0], src.shape[1]
        for c in range(dil):
            for h in range(DIL_HEADS):
                dst[h, pl.ds(c, rows, stride=dil), :] = src[c, :, h * HEAD_DIM:(h + 1) * HEAD_DIM].astype(F32)
    l0, l1, l2 = l0_ref[...], l1_ref[...], l2_ref[...]
    m = jnp.maximum(jnp.maximum(l0, l1), l2)
    e0, e1, e2 = jnp.exp(l0 - m), jnp.exp(l1 - m), jnp.exp(l2 - m)
    inv = 1.0 / (e0 + e1 + e2)
    w0, w1, w2 = e0 * inv, e1 * inv, e2 * inv
    for h in range(DIL_HEADS):
        cols = slice(h * HEAD_DIM, (h + 1) * HEAD_DIM)
        out = (w0[:, h:h + 1] * o0_ref[:, cols].astype(F32)
               + w1[:, h:h + 1] * t1_ref[h]
               + w2[:, h:h + 1] * t2_ref[h])
        out_ref[:, cols] = out.astype(out_ref.dtype)


def _mix_groups(outs, lses, *, seq, tm=512):
    bsz, _, _, d = outs[0].shape
    tm = min(tm, seq)
    tiles_per_batch = seq // tm

    def o_spec(dil):
        return pl.BlockSpec((None, dil, tm // dil, d), lambda i: (i // tiles_per_batch, 0, i % tiles_per_batch, 0))

    row = pl.BlockSpec((tm, d), lambda i: (i, 0))
    lrow = pl.BlockSpec((tm, DIL_HEADS), lambda i: (i, 0))
    dils = [o.shape[1] for o in outs]
    assert dils[0] == 1
    return pl.pallas_call(
        _mix_kernel,
        grid=(bsz * tiles_per_batch,),
        in_specs=[row, o_spec(dils[1]), o_spec(dils[2]), lrow, lrow, lrow],
        out_specs=row,
        out_shape=jax.ShapeDtypeStruct((bsz * seq, d), BF16),
        scratch_shapes=[pltpu.VMEM((DIL_HEADS, tm, HEAD_DIM), F32)] * 2,
        compiler_params=_params("parallel"),
        name="dilated_mix",
    )(outs[0].reshape(bsz * seq, d), outs[1], outs[2], *lses)


def _compress_kernel(x_ref, pe_lo_ref, pe_hi_ref, w1_lo_ref, w1_hi_ref, w2_ref, o_ref, shift_ref):
    n = x_ref.shape[0]
    x = x_ref[...].astype(F32)
    first = _dot((x + pe_lo_ref[...]).astype(BF16), w1_lo_ref[...])
    second = _dot((x + pe_hi_ref[...]).astype(BF16), w1_hi_ref[...])
    shift_ref[0:n, :] = second
    shift_ref[n:n + 8, :] = jnp.zeros((8, HEAD_DIM), F32)
    h = first + shift_ref[1:n + 1, :]
    o_ref[...] = _dot(jax.nn.gelu(h).astype(BF16), w2_ref[...]).astype(o_ref.dtype)


def _compress(rows, pe, w1, w2):
    bg, n, width = rows.shape
    half = CMP_STRIDE * HEAD_DIM
    pe_lo = pe[:CMP_STRIDE].reshape(1, half)
    pe_hi = pe[CMP_STRIDE:].reshape(1, half)
    w1_lo = w1[:half].astype(BF16)
    w1_hi = w1[half:].astype(BF16)
    const = lambda shape: pl.BlockSpec(shape, lambda i: (0, 0))
    return pl.pallas_call(
        _compress_kernel,
        grid=(bg,),
        in_specs=[pl.BlockSpec((None, n, width), lambda i: (i, 0, 0)), const((1, half)), const((1, half)),
                  const((half, HEAD_DIM)), const((half, HEAD_DIM)), const((HEAD_DIM, HEAD_DIM))],
        out_specs=pl.BlockSpec((None, n, HEAD_DIM), lambda i: (i, 0, 0)),
        out_shape=jax.ShapeDtypeStruct((bg, n, HEAD_DIM), BF16),
        scratch_shapes=[pltpu.VMEM((n + 8, HEAD_DIM), F32)],
        compiler_params=_params("parallel"),
        name="nsa_compress",
    )(rows, pe_lo, pe_hi, w1_lo, w1_hi, w2.astype(BF16))


def _nsa_kernel(q_ref, kc_ref, vc_ref, c2s_ref, ks_ref, vs_ref, ow_ref, g_ref, o_ref,
                kaug_ref, qaug_ref, s0_ref, s1_ref, p0_ref, p1_ref, a0_ref, a1_ref, m_ref, l_ref, acc_ref,
                *, tq, kt, n_blk, n_sel):
    grp = NSA_GROUP
    n_cmp = kc_ref.shape[0]
    seq = ks_ref.shape[0]
    n_halves = qaug_ref.shape[0]
    chunks_per_half = HEAD_DIM * SEL_BLOCK // kt
    q0 = pl.program_id(2) * tq

    @pl.when(pl.program_id(2) == 0)
    def _():
        def fill(r, carry):
            r0 = pl.multiple_of(r * kt, kt)
            blk_of_row = (lax.broadcasted_iota(jnp.int32, (kt, HEAD_DIM), 0) + r0) // SEL_BLOCK
            lane = lax.broadcasted_iota(jnp.int32, (kt, HEAD_DIM), 1)
            kaug_ref[pl.ds(r0, kt), 0:HEAD_DIM] = ks_ref[pl.ds(r0, kt), :]
            kaug_ref[pl.ds(r0, kt), HEAD_DIM:2 * HEAD_DIM] = jnp.where(
                lane == blk_of_row % HEAD_DIM, NEG, 0.0).astype(kaug_ref.dtype)
            return carry
        lax.fori_loop(0, seq // kt, fill, 0)

    qs = jnp.concatenate([q_ref[:, h * HEAD_DIM:(h + 1) * HEAD_DIM] for h in range(grp)], axis=0)
    qpos = q0 + lax.broadcasted_iota(jnp.int32, (tq, 1), 0)

    cmp_end = lax.broadcasted_iota(jnp.int32, (tq, n_cmp), 1) * CMP_STRIDE + (CMP_BLOCK - 1)
    cmp_bias = jnp.where(cmp_end <= qpos, 0.0, NEG)
    s = _dot_nt(qs, kc_ref[...]).reshape(grp, tq, n_cmp) + cmp_bias[None]
    m = jnp.max(s, axis=-1, keepdims=True)
    m = jnp.where(m < 0.5 * NEG, 0.0, m)
    p = jnp.exp2(s - m)
    l = jnp.sum(p, axis=-1, keepdims=True)
    p = p * (1.0 / jnp.where(l > 0.0, l, 1.0))
    o_cmp = _dot(p.reshape(grp * tq, n_cmp).astype(BF16), vc_ref[...])

    p_sum = jnp.sum(p, axis=0)
    p_hi = p_sum.astype(BF16)
    p_lo = (p_sum - p_hi.astype(F32)).astype(BF16)
    imp = _dot_nt(c2s_ref[...], p_hi) + _dot_nt(c2s_ref[...], p_lo)

    n_pad = c2s_ref.shape[0]
    blk = lax.broadcasted_iota(jnp.int32, (n_pad, tq), 0)
    qpos_lane = q0 + lax.broadcasted_iota(jnp.int32, (1, tq), 1)
    cur = qpos_lane // SEL_BLOCK
    forced = (blk == 0) | (blk == cur) | (blk == cur - 1)
    score = jnp.where(forced, jnp.inf, jnp.where(blk * SEL_BLOCK <= qpos_lane, imp, -1.0))
    blk_f = blk.astype(F32)
    not_sel = jnp.ones((n_pad, tq), F32)
    for _ in range(n_sel):
        best = jnp.max(score, axis=0, keepdims=True)
        first = jnp.min(jnp.where(score == best, blk_f, float(n_pad)), axis=0, keepdims=True)
        hit = blk_f == first
        not_sel = jnp.where(hit, 0.0, not_sel)
        score = jnp.where(hit, -2.0, score)
    not_sel = not_sel.T
    for half in range(n_halves):
        flags = not_sel[:, half * HEAD_DIM:(half + 1) * HEAD_DIM].astype(qaug_ref.dtype)
        qaug_ref[half, :, 0:HEAD_DIM] = qs
        qaug_ref[half, :, HEAD_DIM:2 * HEAD_DIM] = jnp.concatenate([flags] * grp, axis=0)

    rows_blk = SOFTMAX_ROWS
    bufs = ((s0_ref, p0_ref, a0_ref), (s1_ref, p1_ref, a1_ref))
    m_ref[...] = jnp.full(m_ref.shape, NEG, F32)
    l_ref[...] = jnp.zeros(l_ref.shape, F32)
    acc_ref[...] = jnp.zeros(acc_ref.shape, F32)

    def scores(j, slot):
        k0 = pl.multiple_of(j * kt, kt)
        bufs[slot][0][...] = _dot_nt(qaug_ref[j // chunks_per_half], kaug_ref[pl.ds(k0, kt), :])

    def softmax(j, slot, causal):
        s_buf, p_buf, a_buf = bufs[slot]
        k0 = j * kt
        for rb in range(grp * tq // rows_blk):
            rows = slice(rb * rows_blk, (rb + 1) * rows_blk)
            sb = s_buf[rows, :]
            if causal:
                row_pos = q0 + (rb * rows_blk) % tq + lax.broadcasted_iota(jnp.int32, (rows_blk, 1), 0)
                key_pos = k0 + lax.broadcasted_iota(jnp.int32, (rows_blk, kt), 1)
                sb = jnp.where(key_pos <= row_pos, sb, NEG)
            m_prev = m_ref[rows, :]
            m_new = jnp.maximum(m_prev, jnp.max(sb, axis=-1, keepdims=True))
            alpha = jnp.exp2(m_prev - m_new)
            p = jnp.exp2(sb - jnp.concatenate([m_new] * (kt // HEAD_DIM), axis=1))
            l_ref[rows, :] = alpha * l_ref[rows, :] + jnp.sum(p, axis=-1, keepdims=True)
            m_ref[rows, :] = m_new
            a_buf[rows, :] = alpha
            p_buf[rows, :] = p.astype(p_buf.dtype)

    def weighted_values(j, slot):
        _, p_buf, a_buf = bufs[slot]
        k0 = pl.multiple_of(j * kt, kt)
        acc_ref[...] = a_buf[...] * acc_ref[...] + _dot(p_buf[...], vs_ref[pl.ds(k0, kt), :])

    n_interior = q0 // kt
    scores(0, 0)

    def interior_pair(jj, carry):
        j = 2 * jj
        scores(j + 1, 1)
        softmax(j, 0, causal=False)
        weighted_values(j, 0)
        scores(j + 2, 0)
        softmax(j + 1, 1, causal=False)
        weighted_values(j + 1, 1)
        return carry

    lax.fori_loop(0, n_interior // 2, interior_pair, 0)

    @pl.when(n_interior % 2 == 0)
    def _():
        softmax(n_interior, 0, causal=True)
        weighted_values(n_interior, 0)

    @pl.when(n_interior % 2 == 1)
    def _():
        scores(n_interior, 1)
        softmax(n_interior - 1, 0, causal=False)
        weighted_values(n_interior - 1, 0)
        softmax(n_interior, 1, causal=True)
        weighted_values(n_interior, 1)

    o_slc = acc_ref[...] * (1.0 / l_ref[...])

    gates = g_ref[...]
    for h in range(grp):
        rows = slice(h * tq, (h + 1) * tq)
        cols = slice(h * HEAD_DIM, (h + 1) * HEAD_DIM)
        out = (gates[:, 3 * h:3 * h + 1] * o_cmp[rows]
               + gates[:, 3 * h + 1:3 * h + 2] * o_slc[rows]
               + gates[:, 3 * h + 2:3 * h + 3] * ow_ref[:, cols].astype(F32))
        o_ref[:, cols] = out.astype(o_ref.dtype)


def _sel_from_cmp(n_cmp_pad, n_blk_pad):
    pos = np.arange(n_cmp_pad)[:, None] * CMP_STRIDE + np.arange(CMP_BLOCK)[None, :]
    owner = pos // SEL_BLOCK
    frac = (owner[:, :, None] == np.arange(n_blk_pad)[None, None, :]).sum(axis=1) / CMP_BLOCK
    return jnp.asarray(frac.T, dtype=BF16)


def _nsa_attention(proj, kc, vc, o_win, gates, *, q_idx, ks_idx, vs_idx, tq=256, kt=512):
    bsz, seq, _ = proj.shape
    tq = min(tq, seq)
    kt = min(kt, seq)
    assert kt % tq == 0 and (HEAD_DIM * SEL_BLOCK) % kt == 0 and seq % kt == 0
    n_cmp = kc.shape[1]
    n_blk = seq // SEL_BLOCK
    n_sel = min(N_SELECT, n_blk)
    n_halves = -(-n_blk // HEAD_DIM)
    wq = NSA_GROUP * HEAD_DIM
    rows = NSA_GROUP * tq
    c2s = _sel_from_cmp(n_cmp, n_halves * HEAD_DIM)
    kern = functools.partial(_nsa_kernel, tq=tq, kt=kt, n_blk=n_blk, n_sel=n_sel)
    full = lambda idx_fn: pl.BlockSpec((None, seq, HEAD_DIM), lambda b, g, i: (b, 0, idx_fn(g)))
    cmp_spec = pl.BlockSpec((None, n_cmp, HEAD_DIM), lambda b, g, i: (b * NSA_KV_HEADS + g, 0, 0))
    return pl.pallas_call(
        kern,
        grid=(bsz, NSA_KV_HEADS, seq // tq),
        in_specs=[
            pl.BlockSpec((None, tq, wq), lambda b, g, i: (b, i, q_idx(g))),
            cmp_spec, cmp_spec,
            pl.BlockSpec(c2s.shape, lambda b, g, i: (0, 0)),
            full(ks_idx), full(vs_idx),
            pl.BlockSpec((None, tq, wq), lambda b, g, i: (b, i, g)),
            pl.BlockSpec((None, tq, HEAD_DIM), lambda b, g, i: (b, i, g)),
        ],
        out_specs=pl.BlockSpec((None, tq, wq), lambda b, g, i: (b, i, g)),
        out_shape=jax.ShapeDtypeStruct((bsz, seq, NSA_KV_HEADS * wq), BF16),
        scratch_shapes=[pltpu.VMEM((seq, 2 * HEAD_DIM), BF16),
                        pltpu.VMEM((n_halves, rows, 2 * HEAD_DIM), BF16),
                        pltpu.VMEM((rows, kt), F32), pltpu.VMEM((rows, kt), F32),
                        pltpu.VMEM((rows, kt), BF16), pltpu.VMEM((rows, kt), BF16)]
                       + [pltpu.VMEM((rows, HEAD_DIM), F32)] * 5,
        compiler_params=_params("parallel", "parallel", "arbitrary"),
        name="nsa_core",
    )(proj, kc, vc, c2s, proj, proj, o_win, gates)


_BLK_AQ, _BLK_BQ = 0, NSA_HEADS
_BLK_AK = NSA_HEADS + SWA_HEADS
_BLK_BK = _BLK_AK + 3 * NSA_KV_HEADS
_BLK_AV = _BLK_BK + SWA_KV_HEADS
_BLK_BV = _BLK_AV + 3 * NSA_KV_HEADS


def _even_weights(w_in):
    aq, ak, av, ag, bq, bk, bv = jnp.split(w_in, np.cumsum(EVEN_WIDTHS)[:-1].tolist(), axis=-1)
    w_main = jnp.concatenate([aq, bq, ak, bk, av, bv], axis=-1).astype(BF16)
    ag = ag.reshape(D_MODEL, NSA_KV_HEADS, 3 * NSA_GROUP)
    w_gate = jnp.pad(ag, ((0, 0), (0, 0), (0, HEAD_DIM - 3 * NSA_GROUP))).reshape(D_MODEL, NSA_KV_HEADS * HEAD_DIM)
    return w_main, w_gate.astype(BF16)


def _tile_modes(blocks_per_tile, block_modes):
    modes = np.asarray(block_modes, np.int32).reshape(-1, blocks_per_tile)
    assert (modes == modes[:, :1]).all(), "a projection tile must not straddle q/k/v column groups"
    return jnp.asarray(modes[:, 0])


def _even_mixer(xf, xb, cos, sin, bsz, seq, w_in, w_o, pe_k, pe_v, ck_w1, ck_w2, cv_w1, cv_w2, sinks, g, b):
    w_main, w_gate = _even_weights(w_in)
    tn = 1024
    n_q, n_k = NSA_HEADS + SWA_HEADS, 3 * NSA_KV_HEADS + SWA_KV_HEADS
    modes = _tile_modes(tn // HEAD_DIM, [MODE_ROPE_SCALED] * n_q + [MODE_ROPE] * n_k + [MODE_PLAIN] * n_k)
    proj4 = _project(xb, w_main, modes, cos, sin, seq=seq, tn=tn)
    proj = proj4.reshape(bsz, seq, EVEN_PROJ)
    gates = _gate_project(xb, w_gate).reshape(bsz, seq, NSA_KV_HEADS * HEAD_DIM)

    def cmp_rows(blk0):
        t = proj[:, :, blk0 * HEAD_DIM:(blk0 + NSA_KV_HEADS) * HEAD_DIM]
        t = t.reshape(bsz, seq // CMP_STRIDE, CMP_STRIDE, NSA_KV_HEADS, HEAD_DIM).transpose(0, 3, 1, 2, 4)
        return t.reshape(bsz * NSA_KV_HEADS, seq // CMP_STRIDE, CMP_STRIDE * HEAD_DIM)

    kc = _compress(cmp_rows(_BLK_AK), pe_k, ck_w1, ck_w2)
    vc = _compress(cmp_rows(_BLK_AV), pe_v, cv_w1, cv_w2)
    o_win = _banded_attention(
        proj4, n_steps=NSA_KV_HEADS, hq=NSA_GROUP, grp=NSA_GROUP, window=NSA_WINDOW,
        q_at=lambda s: (0, s), k_at=lambda s: (0, _BLK_AK + 2 * NSA_KV_HEADS + s),
        v_at=lambda s: (0, _BLK_AV + 2 * NSA_KV_HEADS + s), o_at=lambda s: (0, s), o_dims=(1, NSA_Q),
        tq=512, pw=512).reshape(bsz, seq, NSA_Q)
    o_a = _nsa_attention(proj, kc, vc, o_win, gates, q_idx=lambda s: s,
                         ks_idx=lambda s: _BLK_AK + NSA_KV_HEADS + s, vs_idx=lambda s: _BLK_AV + NSA_KV_HEADS + s)
    swa_grp = SWA_HEADS // SWA_KV_HEADS
    o_b = _banded_attention(
        proj4, n_steps=SWA_KV_HEADS, hq=swa_grp, grp=swa_grp, window=SWA_WINDOW,
        q_at=lambda s: (0, NSA_HEADS // swa_grp + s), k_at=lambda s: (0, _BLK_BK + s),
        v_at=lambda s: (0, _BLK_BV + s), o_at=lambda s: (0, s), o_dims=(1, SWA_Q),
        tq=256, pw=128, sinks=sinks)
    m = bsz * seq
    w_o = w_o.astype(BF16)
    return _out_proj_ln([o_a.reshape(m, NSA_Q), o_b.reshape(m, SWA_Q)], [w_o[:NSA_Q], w_o[NSA_Q:]], xf, g, b)


def _odd_mixer(xf, xb, cos, sin, bsz, seq, w_in, w_o, g, b):
    tn = 1024
    modes = _tile_modes(tn // HEAD_DIM, [MODE_ROPE_SCALED] * DIL_HEADS + [MODE_ROPE] * DIL_HEADS + [MODE_PLAIN] * DIL_HEADS)
    m = bsz * seq
    width = DIL_HEADS * HEAD_DIM
    outs, lses = [], []
    for grp_i, (window, dil) in enumerate(DIL_PATTERNS):
        w_grp = w_in[:, grp_i * DIL_QKV:(grp_i + 1) * DIL_QKV].astype(BF16)
        qkv = _project(xb, w_grp, modes, cos, sin, seq=seq, dil=dil, tn=tn)
        o, lse = _banded_attention(
            qkv, n_steps=dil, hq=DIL_HEADS, grp=1, window=window // dil + 1,
            q_at=lambda s: (s, 0), k_at=lambda s: (s, 1), v_at=lambda s: (s, 2), o_at=lambda s: (s, 0),
            o_dims=(dil, width), tq=256, pw=128, want_lse=True)
        outs.append(o)
        lses.append(lse.transpose(0, 2, 1, 3).reshape(m, DIL_HEADS))
    mixed = _mix_groups(outs, lses, seq=seq)
    return _out_proj_ln([mixed], [w_o.astype(BF16)], xf, g, b)


def kernel(x, positions, e_w_in, e_w_o, nsa_pe_k, nsa_pe_v, nsa_ck_w1, nsa_ck_w2, nsa_cv_w1, nsa_cv_w2, swa_sinks, o_w_in, o_w_o, ln1_g, ln1_b, mlp_w1, mlp_w2, ln2_g, ln2_b):
    bsz, seq, d = x.shape
    m = bsz * seq
    cos, sin = _rope_tables(positions)
    xf = x.reshape(m, d)
    xb = xf.astype(BF16)
    for layer in range(DEPTH):
        i = layer // 2
        if layer % 2 == 0:
            xf, xb = _even_mixer(xf, xb, cos, sin, bsz, seq, e_w_in[i], e_w_o[i], nsa_pe_k[i], nsa_pe_v[i],
                                 nsa_ck_w1[i], nsa_ck_w2[i], nsa_cv_w1[i], nsa_cv_w2[i], swa_sinks[i],
                                 ln1_g[layer], ln1_b[layer])
        else:
            xf, xb = _odd_mixer(xf, xb, cos, sin, bsz, seq, o_w_in[i], o_w_o[i], ln1_g[layer], ln1_b[layer])
        xf, xb = _mlp_ln(xf, xb, mlp_w1[layer].astype(BF16), mlp_w2[layer].astype(BF16), ln2_g[layer], ln2_b[layer])
    return xf.reshape(bsz, seq, d)
```

```python
import functools
import math

import numpy as np
import jax
import jax.numpy as jnp
from jax import lax
from jax.experimental import pallas as pl
from jax.experimental.pallas import tpu as pltpu

D_MODEL = 2048
DEPTH = 4
HEAD_DIM = 128
ROPE_THETA = 10000.0
LN_EPS = 1e-5

NSA_HEADS = D_MODEL // (2 * HEAD_DIM)
NSA_KV_HEADS = 2
NSA_GROUP = NSA_HEADS // NSA_KV_HEADS
CMP_BLOCK = 32
CMP_STRIDE = 16
SEL_BLOCK = 64
N_SELECT = 16
NSA_WINDOW = 512

SWA_HEADS = D_MODEL // (2 * HEAD_DIM)
SWA_KV_HEADS = 2
SWA_WINDOW = 128

DIL_HEADS = D_MODEL // HEAD_DIM
DIL_PATTERNS = ((128, 1), (512, 4), (2048, 16))

D_FF = 4 * D_MODEL
DN_ALPHA = (2 * DEPTH) ** 0.25

NSA_Q = NSA_HEADS * HEAD_DIM
NSA_KV = NSA_KV_HEADS * HEAD_DIM
SWA_Q = SWA_HEADS * HEAD_DIM
SWA_KV = SWA_KV_HEADS * HEAD_DIM
EVEN_WIDTHS = (NSA_Q, 3 * NSA_KV, 3 * NSA_KV, 3 * NSA_HEADS, SWA_Q, SWA_KV, SWA_KV)
EVEN_PROJ = NSA_Q + SWA_Q + 3 * NSA_KV + SWA_KV + 3 * NSA_KV + SWA_KV
DIL_QKV = 3 * DIL_HEADS * HEAD_DIM

LOG2E = math.log2(math.e)
LN2 = math.log(2.0)
QK_SCALE = HEAD_DIM ** -0.5 * LOG2E
NEG = -1e30
SOFTMAX_ROWS = 32
BANDED_SOFTMAX_ROWS = 32
NSA_CHUNK_UNROLL = 4
PROJ_SUB_ROWS = 256
VMEM_LIMIT_BYTES = 56 * 1024 * 1024

MODE_PLAIN, MODE_ROPE, MODE_ROPE_SCALED = 0, 1, 2

F32 = jnp.float32
BF16 = jnp.bfloat16


def _params(*semantics):
    return pltpu.CompilerParams(dimension_semantics=semantics, vmem_limit_bytes=VMEM_LIMIT_BYTES)


def _dot(a, b):
    return jnp.dot(a, b, preferred_element_type=F32)


def _dot_nt(a, b):
    return lax.dot_general(a, b, (((1,), (1,)), ((), ())), preferred_element_type=F32)


def _rope_table_kernel(pos_ref, inv_ref, sign_ref, cos_ref, sin_ref):
    ang = pos_ref[...].astype(F32) * inv_ref[...]
    cos_ref[...] = jnp.cos(ang)
    sin_ref[...] = jnp.sin(ang) * sign_ref[...]


def _rope_tables(positions):
    m = positions.size
    tm = min(m, 2048)
    inv = ROPE_THETA ** (-jnp.arange(0, HEAD_DIM, 2, dtype=F32) / HEAD_DIM)
    inv_full = jnp.concatenate([inv, inv]).reshape(1, HEAD_DIM)
    half = HEAD_DIM // 2
    sign = jnp.concatenate([-jnp.ones((half,), F32), jnp.ones((half,), F32)]).reshape(1, HEAD_DIM)
    row = pl.BlockSpec((tm, HEAD_DIM), lambda i: (i, 0))
    const = pl.BlockSpec((1, HEAD_DIM), lambda i: (0, 0))
    return pl.pallas_call(
        _rope_table_kernel,
        grid=(m // tm,),
        in_specs=[pl.BlockSpec((tm, 1), lambda i: (i, 0)), const, const],
        out_specs=[row, row],
        out_shape=[jax.ShapeDtypeStruct((m, HEAD_DIM), F32)] * 2,
        compiler_params=_params("parallel"),
        name="rope_tables",
    )(positions.reshape(m, 1), inv_full, sign)


def _proj_kernel(modes_ref, x_ref, w_ref, cos_ref, sin_ref, o_ref, *scratch, tn, dil):
    mode = modes_ref[pl.program_id(1)]
    tm = x_ref.shape[0]
    sub = min(PROJ_SUB_ROWS, tm)
    blocks = [slice(blk * HEAD_DIM, (blk + 1) * HEAD_DIM) for blk in range(tn // HEAD_DIM)]
    is_rope = mode != MODE_PLAIN
    scale = jnp.where(mode == MODE_ROPE_SCALED, QK_SCALE, 1.0).astype(F32)

    def epilogue(r, acc):
        rows = slice(r * sub, (r + 1) * sub)
        cos = jnp.where(is_rope, cos_ref[rows, :] * scale, 1.0)
        sin = jnp.where(is_rope, sin_ref[rows, :] * scale, 0.0)
        for blk, cols in enumerate(blocks):
            t = acc[:, cols]
            val = t * cos + pltpu.roll(t, HEAD_DIM // 2, 1) * sin
            if dil == 1:
                o_ref[rows, cols] = val.astype(o_ref.dtype)
            else:
                scratch[0][blk, rows, :] = val
        if dil > 1:
            n = sub // dil
            for c in range(dil):
                for blk, cols in enumerate(blocks):
                    o_ref[c, r * n:(r + 1) * n, cols] = scratch[0][
                        blk, pl.ds(r * sub + c, n, stride=dil), :].astype(o_ref.dtype)

    pending = None
    for r in range(tm // sub):
        acc = _dot(x_ref[r * sub:(r + 1) * sub, :], w_ref[...])
        if pending is not None:
            epilogue(*pending)
        pending = (r, acc)
    epilogue(*pending)


def _project(x, w, modes, cos, sin, *, seq, dil=1, tm=1024, tn=1024):
    m, k = x.shape
    n = w.shape[1]
    tm = min(tm, seq)
    tiles_per_batch = seq // tm
    bsz = m // seq
    if dil == 1:
        out_spec = pl.BlockSpec((tm, tn), lambda i, j, modes: (i, j))
        out_shape = jax.ShapeDtypeStruct((m, n), BF16)
        scratch = []
    else:
        out_spec = pl.BlockSpec((None, dil, tm // dil, tn),
                                lambda i, j, modes: (i // tiles_per_batch, 0, i % tiles_per_batch, j))
        out_shape = jax.ShapeDtypeStruct((bsz, dil, seq // dil, n), BF16)
        scratch = [pltpu.VMEM((tn // HEAD_DIM, tm, HEAD_DIM), F32)]
    grid_spec = pltpu.PrefetchScalarGridSpec(
        num_scalar_prefetch=1,
        grid=(m // tm, n // tn),
        in_specs=[
            pl.BlockSpec((tm, k), lambda i, j, modes: (i, 0)),
            pl.BlockSpec((k, tn), lambda i, j, modes: (0, j)),
            pl.BlockSpec((tm, HEAD_DIM), lambda i, j, modes: (i, 0)),
            pl.BlockSpec((tm, HEAD_DIM), lambda i, j, modes: (i, 0)),
        ],
        out_specs=out_spec,
        scratch_shapes=scratch,
    )
    out = pl.pallas_call(
        functools.partial(_proj_kernel, tn=tn, dil=dil),
        grid_spec=grid_spec,
        out_shape=out_shape,
        compiler_params=_params("parallel", "arbitrary"),
        name="in_proj" if dil == 1 else "in_proj_dil%d" % dil,
    )(modes, x, w, cos, sin)
    return out.reshape(bsz, 1, seq, n) if dil == 1 else out


def _gate_kernel(x_ref, w_ref, o_ref):
    o_ref[...] = jax.nn.sigmoid(_dot(x_ref[...], w_ref[...]))


def _gate_project(x, w, *, tm=1024):
    m, k = x.shape
    n = w.shape[1]
    tm = min(tm, m)
    return pl.pallas_call(
        _gate_kernel,
        grid=(m // tm,),
        in_specs=[pl.BlockSpec((tm, k), lambda i: (i, 0)), pl.BlockSpec((k, n), lambda i: (0, 0))],
        out_specs=pl.BlockSpec((tm, n), lambda i: (i, 0)),
        out_shape=jax.ShapeDtypeStruct((m, n), F32),
        compiler_params=_params("parallel"),
        name="gate_proj",
    )(x, w)


def _residual_layer_norm(res, y, g, b):
    z = DN_ALPHA * res + y
    mu = jnp.mean(z, axis=-1, keepdims=True)
    zc = z - mu
    var = jnp.mean(zc * zc, axis=-1, keepdims=True)
    return zc * lax.rsqrt(var + LN_EPS) * g + b


def _out_proj_kernel(*refs, n_in):
    x_refs, w_refs = refs[:n_in], refs[n_in:2 * n_in]
    res_ref, g_ref, b_ref, of_ref, ob_ref = refs[2 * n_in:]
    tm = res_ref.shape[0]
    sub = min(PROJ_SUB_ROWS, tm)

    def finish(r, y):
        rows = slice(r * sub, (r + 1) * sub)
        out = _residual_layer_norm(res_ref[rows, :], y, g_ref[...], b_ref[...])
        of_ref[rows, :] = out
        ob_ref[rows, :] = out.astype(BF16)

    pending = None
    for r in range(tm // sub):
        rows = slice(r * sub, (r + 1) * sub)
        y = _dot(x_refs[0][rows, :], w_refs[0][...])
        for x_ref, w_ref in zip(x_refs[1:], w_refs[1:]):
            y += _dot(x_ref[rows, :], w_ref[...])
        if pending is not None:
            finish(*pending)
        pending = (r, y)
    finish(*pending)


def _out_proj_ln(xs, ws, res, g, b, *, tm=512):
    m, d = res.shape
    tm = min(tm, m)
    n_in = len(xs)
    in_specs = [pl.BlockSpec((tm, x.shape[1]), lambda i: (i, 0)) for x in xs]
    in_specs += [pl.BlockSpec(w.shape, lambda i: (0, 0)) for w in ws]
    row = pl.BlockSpec((tm, d), lambda i: (i, 0))
    vec = pl.BlockSpec((1, d), lambda i: (0, 0))
    in_specs += [row, vec, vec]
    return pl.pallas_call(
        functools.partial(_out_proj_kernel, n_in=n_in),
        grid=(m // tm,),
        in_specs=in_specs,
        out_specs=[row, row],
        out_shape=[jax.ShapeDtypeStruct((m, d), F32), jax.ShapeDtypeStruct((m, d), BF16)],
        compiler_params=_params("parallel"),
        name="out_proj_ln",
    )(*xs, *ws, res, g.reshape(1, d), b.reshape(1, d))


def _mlp_kernel(xb_ref, w1_ref, w2_ref, xf_ref, g_ref, b_ref, of_ref, ob_ref, acc_ref):
    f = pl.program_id(1)
    tm = xb_ref.shape[0]
    sub = min(PROJ_SUB_ROWS, tm)
    @pl.when(f == 0)
    def _():
        acc_ref[...] = jnp.zeros(acc_ref.shape, F32)

    hidden = [None] * (tm // sub)
    for r in range(tm // sub + 1):
        if r < tm // sub:
            hidden[r] = _dot(xb_ref[r * sub:(r + 1) * sub, :], w1_ref[...])
        if r > 0:
            rows = slice((r - 1) * sub, r * sub)
            act = jnp.square(jnp.maximum(hidden[r - 1], 0.0)).astype(BF16)
            acc_ref[rows, :] += _dot(act, w2_ref[...])

    @pl.when(f == pl.num_programs(1) - 1)
    def _():
        out = _residual_layer_norm(xf_ref[...], acc_ref[...], g_ref[...], b_ref[...])
        of_ref[...] = out
        ob_ref[...] = out.astype(BF16)


def _mlp_ln(xf, xb, w1, w2, g, b, *, tm=512, tf=1024):
    m, d = xf.shape
    ff = w1.shape[1]
    tm = min(tm, m)
    row = pl.BlockSpec((tm, d), lambda i, f: (i, 0))
    vec = pl.BlockSpec((1, d), lambda i, f: (0, 0))
    return pl.pallas_call(
        _mlp_kernel,
        grid=(m // tm, ff // tf),
        in_specs=[row, pl.BlockSpec((d, tf), lambda i, f: (0, f)), pl.BlockSpec((tf, d), lambda i, f: (f, 0)),
                  row, vec, vec],
        out_specs=[row, row],
        out_shape=[jax.ShapeDtypeStruct((m, d), F32), jax.ShapeDtypeStruct((m, d), BF16)],
        scratch_shapes=[pltpu.VMEM((tm, d), F32)],
        compiler_params=_params("parallel", "arbitrary"),
        name="mlp_ln",
    )(xb, w1, w2, xf, g.reshape(1, d), b.reshape(1, d))


def _banded_kernel(*refs, hq, grp, window, tq, pw, has_sink, want_lse):
    refs = list(refs)
    sink_ref = refs.pop(0) if has_sink else None
    q_ref, kc_ref, kp_ref, vc_ref, vp_ref, o_ref = refs[:6]
    lse_ref = refs[6] if want_lse else None
    bias_ref = refs[-1]
    score_refs = refs[-1 - 3 * hq:-1 - 2 * hq]
    prob_refs = refs[-1 - 2 * hq:-1 - hq]
    linv_refs = refs[-1 - hq:-1]
    step, i = pl.program_id(1), pl.program_id(2)
    span = pw + tq
    rows_blk = min(BANDED_SOFTMAX_ROWS, tq)

    row = lax.broadcasted_iota(jnp.int32, (tq, span), 0)
    rel = lax.broadcasted_iota(jnp.int32, (tq, span), 1) - pw - row
    mask = (rel <= 0) & (rel > -window) & ((rel + row >= 0) | (i > 0))
    bias_ref[...] = jnp.where(mask, 0.0, NEG)

    @pl.when(i == 0)
    def _():
        for h in range(hq):
            prob_refs[h][...] = jnp.zeros(prob_refs[h].shape, prob_refs[h].dtype)

    for h in range(hq):
        kv = h // grp
        q = q_ref[:, h * HEAD_DIM:(h + 1) * HEAD_DIM]
        kcols = slice(kv * HEAD_DIM, (kv + 1) * HEAD_DIM)
        score_refs[h][:, :pw] = _dot_nt(q, kp_ref[:, kcols])
        score_refs[h][:, pw:] = _dot_nt(q, kc_ref[:, kcols])

    for h in range(hq):
        kv = h // grp
        qcols = slice(h * HEAD_DIM, (h + 1) * HEAD_DIM)
        kcols = slice(kv * HEAD_DIM, (kv + 1) * HEAD_DIM)
        if has_sink:
            sink = sink_ref[step * hq + h] * LOG2E
        for rb in range(tq // rows_blk):
            r0 = rb * rows_blk
            rows = slice(r0, r0 + rows_blk)
            cols = slice(max(0, r0 + pw - (window - 1)) // HEAD_DIM * HEAD_DIM,
                         -(-(r0 + rows_blk + pw) // HEAD_DIM) * HEAD_DIM)
            s = score_refs[h][rows, cols] + bias_ref[rows, cols]
            m = jnp.max(s, axis=-1, keepdims=True)
            if has_sink:
                m = jnp.maximum(m, sink)
            p = jnp.exp2(s - m)
            l = jnp.sum(p, axis=-1, keepdims=True)
            if has_sink:
                l = l + jnp.exp2(sink - m)
            prob_refs[h][rows, cols] = p.astype(BF16)
            linv_refs[h][rows, :] = jnp.broadcast_to(1.0 / l, (rows_blk, HEAD_DIM))
            if want_lse:
                lse_ref[rows, h:h + 1] = (m + jnp.log2(l)) * LN2
        o = _dot(prob_refs[h][:, :pw], vp_ref[:, kcols]) + _dot(prob_refs[h][:, pw:], vc_ref[:, kcols])
        o_ref[:, qcols] = (o * linv_refs[h][...]).astype(o_ref.dtype)


def _banded_attention(arr, *, n_steps, hq, grp, window, q_at, k_at, v_at, o_at, o_dims, tq, pw,
                      sinks=None, want_lse=False):
    bsz, _, seq, _ = arr.shape
    tq = min(tq, seq)
    pw = min(pw, tq)
    assert pw >= window - 1 or seq <= pw, (pw, window, seq)
    assert seq % tq == 0 and tq % pw == 0
    hk = hq // grp
    ratio = tq // pw
    wq, wk = hq * HEAD_DIM, hk * HEAD_DIM
    has_sink = sinks is not None

    def spec(width, rows, at_fn, row_fn):
        def index(b, s, i, *_):
            plane, col = at_fn(s)
            return (b, plane, row_fn(i), col)
        return pl.BlockSpec((None, None, rows, width), index)

    cur = lambda i: i
    prev = lambda i: jnp.maximum(i * ratio - 1, 0)
    in_specs = [spec(wq, tq, q_at, cur), spec(wk, tq, k_at, cur), spec(wk, pw, k_at, prev),
                spec(wk, tq, v_at, cur), spec(wk, pw, v_at, prev)]
    out_specs = [spec(wq, tq, o_at, cur)]
    out_shape = [jax.ShapeDtypeStruct((bsz, o_dims[0], seq, o_dims[1]), BF16)]
    if want_lse:
        out_specs.append(pl.BlockSpec((None, None, tq, hq), lambda b, s, i: (b, s, i, 0)))
        out_shape.append(jax.ShapeDtypeStruct((bsz, n_steps, seq, hq), F32))
    kern = functools.partial(_banded_kernel, hq=hq, grp=grp, window=window, tq=tq, pw=pw,
                             has_sink=has_sink, want_lse=want_lse)
    grid = (bsz, n_steps, seq // tq)
    params = _params("parallel", "parallel", "arbitrary")
    scratch = ([pltpu.VMEM((tq, pw + tq), F32)] * hq + [pltpu.VMEM((tq, pw + tq), BF16)] * hq
               + [pltpu.VMEM((tq, HEAD_DIM), F32)] * hq + [pltpu.VMEM((tq, pw + tq), F32)])
    if has_sink:
        grid_spec = pltpu.PrefetchScalarGridSpec(num_scalar_prefetch=1, grid=grid, in_specs=in_specs,
                                                 out_specs=out_specs, scratch_shapes=scratch)
        out = pl.pallas_call(kern, grid_spec=grid_spec, out_shape=out_shape, compiler_params=params,
                             name="banded_attn_sink")(sinks.astype(F32), arr, arr, arr, arr, arr)
    else:
        out = pl.pallas_call(kern, grid=grid, in_specs=in_specs, out_specs=out_specs, out_shape=out_shape,
                             scratch_shapes=scratch, compiler_params=params, name="banded_attn")(arr, arr, arr, arr, arr)
    return out if want_lse else out[0]


def _mix_kernel(o0_ref, o1_ref, o2_ref, l0_ref, l1_ref, l2_ref, out_ref, t1_ref, t2_ref):
    for src, dst in ((o1_ref, t1_ref), (o2_ref, t2_ref)):
        dil, rows = src.shape[0], src.shape[1]
        for c in range(dil):
            for h in range(DIL_HEADS):
                dst[h, pl.ds(c, rows, stride=dil), :] = src[c, :, h * HEAD_DIM:(h + 1) * HEAD_DIM].astype(F32)
    l0, l1, l2 = l0_ref[...], l1_ref[...], l2_ref[...]
    m = jnp.maximum(jnp.maximum(l0, l1), l2)
    e0, e1, e2 = jnp.exp(l0 - m), jnp.exp(l1 - m), jnp.exp(l2 - m)
    inv = 1.0 / (e0 + e1 + e2)
    w0, w1, w2 = e0 * inv, e1 * inv, e2 * inv
    for h in range(DIL_HEADS):
        cols = slice(h * HEAD_DIM, (h + 1) * HEAD_DIM)
        out = (w0[:, h:h + 1] * o0_ref[:, cols].astype(F32)
               + w1[:, h:h + 1] * t1_ref[h]
               + w2[:, h:h + 1] * t2_ref[h])
        out_ref[:, cols] = out.astype(out_ref.dtype)


def _mix_groups(outs, lses, *, seq, tm=512):
    bsz, _, _, d = outs[0].shape
    tm = min(tm, seq)
    tiles_per_batch = seq // tm

    def o_spec(dil):
        return pl.BlockSpec((None, dil, tm // dil, d), lambda i: (i // tiles_per_batch, 0, i % tiles_per_batch, 0))

    row = pl.BlockSpec((tm, d), lambda i: (i, 0))
    lrow = pl.BlockSpec((tm, DIL_HEADS), lambda i: (i, 0))
    dils = [o.shape[1] for o in outs]
    assert dils[0] == 1
    return pl.pallas_call(
        _mix_kernel,
        grid=(bsz * tiles_per_batch,),
        in_specs=[row, o_spec(dils[1]), o_spec(dils[2]), lrow, lrow, lrow],
        out_specs=row,
        out_shape=jax.ShapeDtypeStruct((bsz * seq, d), BF16),
        scratch_shapes=[pltpu.VMEM((DIL_HEADS, tm, HEAD_DIM), F32)] * 2,
        compiler_params=_params("parallel"),
        name="dilated_mix",
    )(outs[0].reshape(bsz * seq, d), outs[1], outs[2], *lses)


def _compress_kernel(x_ref, pe_lo_ref, pe_hi_ref, w1_lo_ref, w1_hi_ref, w2_ref, o_ref, shift_ref):
    n = x_ref.shape[0]
    x = x_ref[...].astype(F32)
    first = _dot((x + pe_lo_ref[...]).astype(BF16), w1_lo_ref[...])
    second = _dot((x + pe_hi_ref[...]).astype(BF16), w1_hi_ref[...])
    shift_ref[0:n, :] = second
    shift_ref[n:n + 8, :] = jnp.zeros((8, HEAD_DIM), F32)
    h = first + shift_ref[1:n + 1, :]
    o_ref[...] = _dot(jax.nn.gelu(h).astype(BF16), w2_ref[...]).astype(o_ref.dtype)


def _compress(rows, pe, w1, w2):
    bg, n, width = rows.shape
    half = CMP_STRIDE * HEAD_DIM
    pe_lo = pe[:CMP_STRIDE].reshape(1, half)
    pe_hi = pe[CMP_STRIDE:].reshape(1, half)
    w1_lo = w1[:half].astype(BF16)
    w1_hi = w1[half:].astype(BF16)
    const = lambda shape: pl.BlockSpec(shape, lambda i: (0, 0))
    return pl.pallas_call(
        _compress_kernel,
        grid=(bg,),
        in_specs=[pl.BlockSpec((None, n, width), lambda i: (i, 0, 0)), const((1, half)), const((1, half)),
                  const((half, HEAD_DIM)), const((half, HEAD_DIM)), const((HEAD_DIM, HEAD_DIM))],
        out_specs=pl.BlockSpec((None, n, HEAD_DIM), lambda i: (i, 0, 0)),
        out_shape=jax.ShapeDtypeStruct((bg, n, HEAD_DIM), BF16),
        scratch_shapes=[pltpu.VMEM((n + 8, HEAD_DIM), F32)],
        compiler_params=_params("parallel"),
        name="nsa_compress",
    )(rows, pe_lo, pe_hi, w1_lo, w1_hi, w2.astype(BF16))


def _nsa_kernel(q_ref, kc_ref, vc_ref, c2s_ref, ks_ref, vs_ref, ow_ref, g_ref, o_ref,
                kaug_ref, qaug_ref, s0_ref, s1_ref, p0_ref, p1_ref, a0_ref, a1_ref, m_ref, l_ref, acc_ref,
                *, tq, kt, n_blk, n_sel):
    grp = NSA_GROUP
    n_cmp = kc_ref.shape[0]
    seq = ks_ref.shape[0]
    n_halves = qaug_ref.shape[0]
    chunks_per_half = HEAD_DIM * SEL_BLOCK // kt
    q0 = pl.program_id(2) * tq

    @pl.when(pl.program_id(2) == 0)
    def _():
        def fill(r, carry):
            r0 = pl.multiple_of(r * kt, kt)
            blk_of_row = (lax.broadcasted_iota(jnp.int32, (kt, HEAD_DIM), 0) + r0) // SEL_BLOCK
            lane = lax.broadcasted_iota(jnp.int32, (kt, HEAD_DIM), 1)
            kaug_ref[pl.ds(r0, kt), 0:HEAD_DIM] = ks_ref[pl.ds(r0, kt), :]
            kaug_ref[pl.ds(r0, kt), HEAD_DIM:2 * HEAD_DIM] = jnp.where(
                lane == blk_of_row % HEAD_DIM, NEG, 0.0).astype(kaug_ref.dtype)
            return carry
        lax.fori_loop(0, seq // kt, fill, 0)

    qs = jnp.concatenate([q_ref[:, h * HEAD_DIM:(h + 1) * HEAD_DIM] for h in range(grp)], axis=0)
    qpos = q0 + lax.broadcasted_iota(jnp.int32, (tq, 1), 0)

    cmp_end = lax.broadcasted_iota(jnp.int32, (tq, n_cmp), 1) * CMP_STRIDE + (CMP_BLOCK - 1)
    cmp_bias = jnp.where(cmp_end <= qpos, 0.0, NEG)
    s = _dot_nt(qs, kc_ref[...]).reshape(grp, tq, n_cmp) + cmp_bias[None]
    m = jnp.max(s, axis=-1, keepdims=True)
    m = jnp.where(m < 0.5 * NEG, 0.0, m)
    p = jnp.exp2(s - m)
    l = jnp.sum(p, axis=-1, keepdims=True)
    p = p * (1.0 / jnp.where(l > 0.0, l, 1.0))
    o_cmp = _dot(p.reshape(grp * tq, n_cmp).astype(BF16), vc_ref[...])

    p_sum = jnp.sum(p, axis=0)
    p_hi = p_sum.astype(BF16)
    p_lo = (p_sum - p_hi.astype(F32)).astype(BF16)
    imp = _dot_nt(c2s_ref[...], p_hi) + _dot_nt(c2s_ref[...], p_lo)

    n_pad = c2s_ref.shape[0]
    blk = lax.broadcasted_iota(jnp.int32, (n_pad, tq), 0)
    qpos_lane = q0 + lax.broadcasted_iota(jnp.int32, (1, tq), 1)
    cur = qpos_lane // SEL_BLOCK
    forced = (blk == 0) | (blk == cur) | (blk == cur - 1)
    score = jnp.where(forced, jnp.inf, jnp.where(blk * SEL_BLOCK <= qpos_lane, imp, -1.0))
    blk_f = blk.astype(F32)
    not_sel = jnp.ones((n_pad, tq), F32)
    for _ in range(n_sel):
        best = jnp.max(score, axis=0, keepdims=True)
        first = jnp.min(jnp.where(score == best, blk_f, float(n_pad)), axis=0, keepdims=True)
        hit = blk_f == first
        not_sel = jnp.where(hit, 0.0, not_sel)
        score = jnp.where(hit, -2.0, score)
    not_sel = not_sel.T
    for half in range(n_halves):
        flags = not_sel[:, half * HEAD_DIM:(half + 1) * HEAD_DIM].astype(qaug_ref.dtype)
        qaug_ref[half, :, 0:HEAD_DIM] = qs
        qaug_ref[half, :, HEAD_DIM:2 * HEAD_DIM] = jnp.concatenate([flags] * grp, axis=0)

    rows_blk = SOFTMAX_ROWS
    bufs = ((s0_ref, p0_ref, a0_ref), (s1_ref, p1_ref, a1_ref))
    m_ref[...] = jnp.full(m_ref.shape, NEG, F32)
    l_ref[...] = jnp.zeros(l_ref.shape, F32)
    acc_ref[...] = jnp.zeros(acc_ref.shape, F32)

    def scores(j, slot):
        k0 = pl.multiple_of(j * kt, kt)
        bufs[slot][0][...] = _dot_nt(qaug_ref[j // chunks_per_half], kaug_ref[pl.ds(k0, kt), :])

    def softmax(j, slot, causal):
        s_buf, p_buf, a_buf = bufs[slot]
        k0 = j * kt
        for rb in range(grp * tq // rows_blk):
            rows = slice(rb * rows_blk, (rb + 1) * rows_blk)
            sb = s_buf[rows, :]
            if causal:
                row_pos = q0 + (rb * rows_blk) % tq + lax.broadcasted_iota(jnp.int32, (rows_blk, 1), 0)
                key_pos = k0 + lax.broadcasted_iota(jnp.int32, (rows_blk, kt), 1)
                sb = jnp.where(key_pos <= row_pos, sb, NEG)
            m_prev = m_ref[rows, :]
            m_new = jnp.maximum(m_prev, jnp.max(sb, axis=-1, keepdims=True))
            alpha = jnp.exp2(m_prev - m_new)
            p = jnp.exp2(sb - jnp.concatenate([m_new] * (kt // HEAD_DIM), axis=1))
            l_ref[rows, :] = alpha * l_ref[rows, :] + jnp.sum(p, axis=-1, keepdims=True)
            m_ref[rows, :] = m_new
            a_buf[rows, :] = alpha
            p_buf[rows, :] = p.astype(p_buf.dtype)

    def weighted_values(j, slot):
        _, p_buf, a_buf = bufs[slot]
        k0 = pl.multiple_of(j * kt, kt)
        acc_ref[...] = a_buf[...] * acc_ref[...] + _dot(p_buf[...], vs_ref[pl.ds(k0, kt), :])

    n_interior = q0 // kt
    scores(0, 0)

    def interior_run(j0, count):
        for c in range(count):
            scores(j0 + c + 1, (c + 1) % 2)
            softmax(j0 + c, c % 2, causal=False)
            weighted_values(j0 + c, c % 2)

    unroll = NSA_CHUNK_UNROLL
    n_runs = n_interior // unroll

    def interior_body(r, carry):
        interior_run(r * unroll, unroll)
        return carry

    lax.fori_loop(0, n_runs, interior_body, 0)
    for count in (c for c in (4, 2) if c < unroll):
        @pl.when((n_interior % (2 * count)) // count == 1)
        def _(count=count):
            interior_run(n_interior // (2 * count) * (2 * count), count)

    @pl.when(n_interior % 2 == 0)
    def _():
        softmax(n_interior, 0, causal=True)
        weighted_values(n_interior, 0)

    @pl.when(n_interior % 2 == 1)
    def _():
        scores(n_interior, 1)
        softmax(n_interior - 1, 0, causal=False)
        weighted_values(n_interior - 1, 0)
        softmax(n_interior, 1, causal=True)
        weighted_values(n_interior, 1)

    o_slc = acc_ref[...] * (1.0 / l_ref[...])

    gates = g_ref[...]
    for h in range(grp):
        rows = slice(h * tq, (h + 1) * tq)
        cols = slice(h * HEAD_DIM, (h + 1) * HEAD_DIM)
        out = (gates[:, 3 * h:3 * h + 1] * o_cmp[rows]
               + gates[:, 3 * h + 1:3 * h + 2] * o_slc[rows]
               + gates[:, 3 * h + 2:3 * h + 3] * ow_ref[:, cols].astype(F32))
        o_ref[:, cols] = out.astype(o_ref.dtype)


def _sel_from_cmp(n_cmp_pad, n_blk_pad):
    pos = np.arange(n_cmp_pad)[:, None] * CMP_STRIDE + np.arange(CMP_BLOCK)[None, :]
    owner = pos // SEL_BLOCK
    frac = (owner[:, :, None] == np.arange(n_blk_pad)[None, None, :]).sum(axis=1) / CMP_BLOCK
    return jnp.asarray(frac.T, dtype=BF16)


def _nsa_attention(proj, kc, vc, o_win, gates, *, q_idx, ks_idx, vs_idx, tq=256, kt=512):
    bsz, seq, _ = proj.shape
    tq = min(tq, seq)
    kt = min(kt, seq)
    assert kt % tq == 0 and (HEAD_DIM * SEL_BLOCK) % kt == 0 and seq % kt == 0
    n_cmp = kc.shape[1]
    n_blk = seq // SEL_BLOCK
    n_sel = min(N_SELECT, n_blk)
    n_halves = -(-n_blk // HEAD_DIM)
    wq = NSA_GROUP * HEAD_DIM
    rows = NSA_GROUP * tq
    c2s = _sel_from_cmp(n_cmp, n_halves * HEAD_DIM)
    kern = functools.partial(_nsa_kernel, tq=tq, kt=kt, n_blk=n_blk, n_sel=n_sel)
    full = lambda idx_fn: pl.BlockSpec((None, seq, HEAD_DIM), lambda b, g, i: (b, 0, idx_fn(g)))
    cmp_spec = pl.BlockSpec((None, n_cmp, HEAD_DIM), lambda b, g, i: (b * NSA_KV_HEADS + g, 0, 0))
    return pl.pallas_call(
        kern,
        grid=(bsz, NSA_KV_HEADS, seq // tq),
        in_specs=[
            pl.BlockSpec((None, tq, wq), lambda b, g, i: (b, i, q_idx(g))),
            cmp_spec, cmp_spec,
            pl.BlockSpec(c2s.shape, lambda b, g, i: (0, 0)),
            full(ks_idx), full(vs_idx),
            pl.BlockSpec((None, tq, wq), lambda b, g, i: (b, i, g)),
            pl.BlockSpec((None, tq, HEAD_DIM), lambda b, g, i: (b, i, g)),
        ],
        out_specs=pl.BlockSpec((None, tq, wq), lambda b, g, i: (b, i, g)),
        out_shape=jax.ShapeDtypeStruct((bsz, seq, NSA_KV_HEADS * wq), BF16),
        scratch_shapes=[pltpu.VMEM((seq, 2 * HEAD_DIM), BF16),
                        pltpu.VMEM((n_halves, rows, 2 * HEAD_DIM), BF16),
                        pltpu.VMEM((rows, kt), F32), pltpu.VMEM((rows, kt), F32),
                        pltpu.VMEM((rows, kt), BF16), pltpu.VMEM((rows, kt), BF16)]
                       + [pltpu.VMEM((rows, HEAD_DIM), F32)] * 5,
        compiler_params=_params("parallel", "parallel", "arbitrary"),
        name="nsa_core",
    )(proj, kc, vc, c2s, proj, proj, o_win, gates)


_BLK_AQ, _BLK_BQ = 0, NSA_HEADS
_BLK_AK = NSA_HEADS + SWA_HEADS
_BLK_BK = _BLK_AK + 3 * NSA_KV_HEADS
_BLK_AV = _BLK_BK + SWA_KV_HEADS
_BLK_BV = _BLK_AV + 3 * NSA_KV_HEADS


def _even_weights(w_in):
    aq, ak, av, ag, bq, bk, bv = jnp.split(w_in, np.cumsum(EVEN_WIDTHS)[:-1].tolist(), axis=-1)
    w_main = jnp.concatenate([aq, bq, ak, bk, av, bv], axis=-1).astype(BF16)
    ag = ag.reshape(D_MODEL, NSA_KV_HEADS, 3 * NSA_GROUP)
    w_gate = jnp.pad(ag, ((0, 0), (0, 0), (0, HEAD_DIM - 3 * NSA_GROUP))).reshape(D_MODEL, NSA_KV_HEADS * HEAD_DIM)
    return w_main, w_gate.astype(BF16)


def _tile_modes(blocks_per_tile, block_modes):
    modes = np.asarray(block_modes, np.int32).reshape(-1, blocks_per_tile)
    assert (modes == modes[:, :1]).all(), "a projection tile must not straddle q/k/v column groups"
    return jnp.asarray(modes[:, 0])


def _even_mixer(xf, xb, cos, sin, bsz, seq, w_in, w_o, pe_k, pe_v, ck_w1, ck_w2, cv_w1, cv_w2, sinks, g, b):
    w_main, w_gate = _even_weights(w_in)
    tn = 1024
    n_q, n_k = NSA_HEADS + SWA_HEADS, 3 * NSA_KV_HEADS + SWA_KV_HEADS
    modes = _tile_modes(tn // HEAD_DIM, [MODE_ROPE_SCALED] * n_q + [MODE_ROPE] * n_k + [MODE_PLAIN] * n_k)
    proj4 = _project(xb, w_main, modes, cos, sin, seq=seq, tn=tn)
    proj = proj4.reshape(bsz, seq, EVEN_PROJ)
    gates = _gate_project(xb, w_gate).reshape(bsz, seq, NSA_KV_HEADS * HEAD_DIM)

    def cmp_rows(blk0):
        t = proj[:, :, blk0 * HEAD_DIM:(blk0 + NSA_KV_HEADS) * HEAD_DIM]
        t = t.reshape(bsz, seq // CMP_STRIDE, CMP_STRIDE, NSA_KV_HEADS, HEAD_DIM).transpose(0, 3, 1, 2, 4)
        return t.reshape(bsz * NSA_KV_HEADS, seq // CMP_STRIDE, CMP_STRIDE * HEAD_DIM)

    kc = _compress(cmp_rows(_BLK_AK), pe_k, ck_w1, ck_w2)
    vc = _compress(cmp_rows(_BLK_AV), pe_v, cv_w1, cv_w2)
    o_win = _banded_attention(
        proj4, n_steps=NSA_KV_HEADS, hq=NSA_GROUP, grp=NSA_GROUP, window=NSA_WINDOW,
        q_at=lambda s: (0, s), k_at=lambda s: (0, _BLK_AK + 2 * NSA_KV_HEADS + s),
        v_at=lambda s: (0, _BLK_AV + 2 * NSA_KV_HEADS + s), o_at=lambda s: (0, s), o_dims=(1, NSA_Q),
        tq=512, pw=512).reshape(bsz, seq, NSA_Q)
    o_a = _nsa_attention(proj, kc, vc, o_win, gates, q_idx=lambda s: s,
                         ks_idx=lambda s: _BLK_AK + NSA_KV_HEADS + s, vs_idx=lambda s: _BLK_AV + NSA_KV_HEADS + s)
    swa_grp = SWA_HEADS // SWA_KV_HEADS
    o_b = _banded_attention(
        proj4, n_steps=SWA_KV_HEADS, hq=swa_grp, grp=swa_grp, window=SWA_WINDOW,
        q_at=lambda s: (0, NSA_HEADS // swa_grp + s), k_at=lambda s: (0, _BLK_BK + s),
        v_at=lambda s: (0, _BLK_BV + s), o_at=lambda s: (0, s), o_dims=(1, SWA_Q),
        tq=256, pw=128, sinks=sinks)
    m = bsz * seq
    w_o = w_o.astype(BF16)
    return _out_proj_ln([o_a.reshape(m, NSA_Q), o_b.reshape(m, SWA_Q)], [w_o[:NSA_Q], w_o[NSA_Q:]], xf, g, b)


def _odd_mixer(xf, xb, cos, sin, bsz, seq, w_in, w_o, g, b):
    tn = 1024
    modes = _tile_modes(tn // HEAD_DIM, [MODE_ROPE_SCALED] * DIL_HEADS + [MODE_ROPE] * DIL_HEADS + [MODE_PLAIN] * DIL_HEADS)
    m = bsz * seq
    width = DIL_HEADS * HEAD_DIM
    outs, lses = [], []
    for grp_i, (window, dil) in enumerate(DIL_PATTERNS):
        w_grp = w_in[:, grp_i * DIL_QKV:(grp_i + 1) * DIL_QKV].astype(BF16)
        qkv = _project(xb, w_grp, modes, cos, sin, seq=seq, dil=dil, tn=tn)
        o, lse = _banded_attention(
            qkv, n_steps=dil, hq=DIL_HEADS, grp=1, window=window // dil + 1,
            q_at=lambda s: (s, 0), k_at=lambda s: (s, 1), v_at=lambda s: (s, 2), o_at=lambda s: (s, 0),
            o_dims=(dil, width), tq=256, pw=128, want_lse=True)
        outs.append(o)
        lses.append(lse.transpose(0, 2, 1, 3).reshape(m, DIL_HEADS))
    mixed = _mix_groups(outs, lses, seq=seq)
    return _out_proj_ln([mixed], [w_o.astype(BF16)], xf, g, b)


def kernel(x, positions, e_w_in, e_w_o, nsa_pe_k, nsa_pe_v, nsa_ck_w1, nsa_ck_w2, nsa_cv_w1, nsa_cv_w2, swa_sinks, o_w_in, o_w_o, ln1_g, ln1_b, mlp_w1, mlp_w2, ln2_g, ln2_b):
    bsz, seq, d = x.shape
    m = bsz * seq
    cos, sin = _rope_tables(positions)
    xf = x.reshape(m, d)
    xb = xf.astype(BF16)
    for layer in range(DEPTH):
        i = layer // 2
        if layer % 2 == 0:
            xf, xb = _even_mixer(xf, xb, cos, sin, bsz, seq, e_w_in[i], e_w_o[i], nsa_pe_k[i], nsa_pe_v[i],
                                 nsa_ck_w1[i], nsa_ck_w2[i], nsa_cv_w1[i], nsa_cv_w2[i], swa_sinks[i],
                                 ln1_g[layer], ln1_b[layer])
        else:
            xf, xb = _odd_mixer(xf, xb, cos, sin, bsz, seq, o_w_in[i], o_w_o[i], ln1_g[layer], ln1_b[layer])
        xf, xb = _mlp_ln(xf, xb, mlp_w1[layer].astype(BF16), mlp_w2[layer].astype(BF16), ln2_g[layer], ln2_b[layer])
    return xf.reshape(bsz, seq, d)
```

```python
import functools
import math

import numpy as np
import jax
import jax.numpy as jnp
from jax import lax
from jax.experimental import pallas as pl
from jax.experimental.pallas import tpu as pltpu

D_MODEL = 2048
DEPTH = 4
HEAD_DIM = 128
ROPE_THETA = 10000.0
LN_EPS = 1e-5

NSA_HEADS = D_MODEL // (2 * HEAD_DIM)
NSA_KV_HEADS = 2
NSA_GROUP = NSA_HEADS // NSA_KV_HEADS
CMP_BLOCK = 32
CMP_STRIDE = 16
SEL_BLOCK = 64
N_SELECT = 16
NSA_WINDOW = 512

SWA_HEADS = D_MODEL // (2 * HEAD_DIM)
SWA_KV_HEADS = 2
SWA_WINDOW = 128

DIL_HEADS = D_MODEL // HEAD_DIM
DIL_PATTERNS = ((128, 1), (512, 4), (2048, 16))

D_FF = 4 * D_MODEL
DN_ALPHA = (2 * DEPTH) ** 0.25

NSA_Q = NSA_HEADS * HEAD_DIM
NSA_KV = NSA_KV_HEADS * HEAD_DIM
SWA_Q = SWA_HEADS * HEAD_DIM
SWA_KV = SWA_KV_HEADS * HEAD_DIM
EVEN_WIDTHS = (NSA_Q, 3 * NSA_KV, 3 * NSA_KV, 3 * NSA_HEADS, SWA_Q, SWA_KV, SWA_KV)
EVEN_PROJ = NSA_Q + SWA_Q + 3 * NSA_KV + SWA_KV + 3 * NSA_KV + SWA_KV
DIL_QKV = 3 * DIL_HEADS * HEAD_DIM

LOG2E = math.log2(math.e)
LN2 = math.log(2.0)
QK_SCALE = HEAD_DIM ** -0.5 * LOG2E
NEG = -1e30
SOFTMAX_ROWS = 32
BANDED_SOFTMAX_ROWS = 32
NSA_CHUNK_UNROLL = 4
NSA_PREFIX_VARIANTS = 4
PROJ_SUB_ROWS = 256
VMEM_LIMIT_BYTES = 56 * 1024 * 1024

MODE_PLAIN, MODE_ROPE, MODE_ROPE_SCALED = 0, 1, 2

F32 = jnp.float32
BF16 = jnp.bfloat16


def _params(*semantics):
    return pltpu.CompilerParams(dimension_semantics=semantics, vmem_limit_bytes=VMEM_LIMIT_BYTES)


def _dot(a, b):
    return jnp.dot(a, b, preferred_element_type=F32)


def _dot_nt(a, b):
    return lax.dot_general(a, b, (((1,), (1,)), ((), ())), preferred_element_type=F32)


def _rope_table_kernel(pos_ref, inv_ref, sign_ref, cos_ref, sin_ref):
    ang = pos_ref[...].astype(F32) * inv_ref[...]
    cos_ref[...] = jnp.cos(ang)
    sin_ref[...] = jnp.sin(ang) * sign_ref[...]


def _rope_tables(positions):
    m = positions.size
    tm = min(m, 2048)
    inv = ROPE_THETA ** (-jnp.arange(0, HEAD_DIM, 2, dtype=F32) / HEAD_DIM)
    inv_full = jnp.concatenate([inv, inv]).reshape(1, HEAD_DIM)
    half = HEAD_DIM // 2
    sign = jnp.concatenate([-jnp.ones((half,), F32), jnp.ones((half,), F32)]).reshape(1, HEAD_DIM)
    row = pl.BlockSpec((tm, HEAD_DIM), lambda i: (i, 0))
    const = pl.BlockSpec((1, HEAD_DIM), lambda i: (0, 0))
    return pl.pallas_call(
        _rope_table_kernel,
        grid=(m // tm,),
        in_specs=[pl.BlockSpec((tm, 1), lambda i: (i, 0)), const, const],
        out_specs=[row, row],
        out_shape=[jax.ShapeDtypeStruct((m, HEAD_DIM), F32)] * 2,
        compiler_params=_params("parallel"),
        name="rope_tables",
    )(positions.reshape(m, 1), inv_full, sign)


def _proj_kernel(modes_ref, x_ref, w_ref, cos_ref, sin_ref, o_ref, *scratch, tn, dil):
    mode = modes_ref[pl.program_id(1)]
    tm = x_ref.shape[0]
    sub = min(PROJ_SUB_ROWS, tm)
    blocks = [slice(blk * HEAD_DIM, (blk + 1) * HEAD_DIM) for blk in range(tn // HEAD_DIM)]
    is_rope = mode != MODE_PLAIN
    scale = jnp.where(mode == MODE_ROPE_SCALED, QK_SCALE, 1.0).astype(F32)

    def epilogue(r, acc):
        rows = slice(r * sub, (r + 1) * sub)
        cos = jnp.where(is_rope, cos_ref[rows, :] * scale, 1.0)
        sin = jnp.where(is_rope, sin_ref[rows, :] * scale, 0.0)
        for blk, cols in enumerate(blocks):
            t = acc[:, cols]
            val = t * cos + pltpu.roll(t, HEAD_DIM // 2, 1) * sin
            if dil == 1:
                o_ref[rows, cols] = val.astype(o_ref.dtype)
            else:
                scratch[0][blk, rows, :] = val
        if dil > 1:
            n = sub // dil
            for c in range(dil):
                for blk, cols in enumerate(blocks):
                    o_ref[c, r * n:(r + 1) * n, cols] = scratch[0][
                        blk, pl.ds(r * sub + c, n, stride=dil), :].astype(o_ref.dtype)

    pending = None
    for r in range(tm // sub):
        acc = _dot(x_ref[r * sub:(r + 1) * sub, :], w_ref[...])
        if pending is not None:
            epilogue(*pending)
        pending = (r, acc)
    epilogue(*pending)


def _project(x, w, modes, cos, sin, *, seq, dil=1, tm=1024, tn=1024):
    m, k = x.shape
    n = w.shape[1]
    tm = min(tm, seq)
    tiles_per_batch = seq // tm
    bsz = m // seq
    if dil == 1:
        out_spec = pl.BlockSpec((tm, tn), lambda i, j, modes: (i, j))
        out_shape = jax.ShapeDtypeStruct((m, n), BF16)
        scratch = []
    else:
        out_spec = pl.BlockSpec((None, dil, tm // dil, tn),
                                lambda i, j, modes: (i // tiles_per_batch, 0, i % tiles_per_batch, j))
        out_shape = jax.ShapeDtypeStruct((bsz, dil, seq // dil, n), BF16)
        scratch = [pltpu.VMEM((tn // HEAD_DIM, tm, HEAD_DIM), F32)]
    grid_spec = pltpu.PrefetchScalarGridSpec(
        num_scalar_prefetch=1,
        grid=(m // tm, n // tn),
        in_specs=[
            pl.BlockSpec((tm, k), lambda i, j, modes: (i, 0)),
            pl.BlockSpec((k, tn), lambda i, j, modes: (0, j)),
            pl.BlockSpec((tm, HEAD_DIM), lambda i, j, modes: (i, 0)),
            pl.BlockSpec((tm, HEAD_DIM), lambda i, j, modes: (i, 0)),
        ],
        out_specs=out_spec,
        scratch_shapes=scratch,
    )
    out = pl.pallas_call(
        functools.partial(_proj_kernel, tn=tn, dil=dil),
        grid_spec=grid_spec,
        out_shape=out_shape,
        compiler_params=_params("parallel", "arbitrary"),
        name="in_proj" if dil == 1 else "in_proj_dil%d" % dil,
    )(modes, x, w, cos, sin)
    return out.reshape(bsz, 1, seq, n) if dil == 1 else out


def _gate_kernel(x_ref, w_ref, o_ref):
    o_ref[...] = jax.nn.sigmoid(_dot(x_ref[...], w_ref[...]))


def _gate_project(x, w, *, tm=1024):
    m, k = x.shape
    n = w.shape[1]
    tm = min(tm, m)
    return pl.pallas_call(
        _gate_kernel,
        grid=(m // tm,),
        in_specs=[pl.BlockSpec((tm, k), lambda i: (i, 0)), pl.BlockSpec((k, n), lambda i: (0, 0))],
        out_specs=pl.BlockSpec((tm, n), lambda i: (i, 0)),
        out_shape=jax.ShapeDtypeStruct((m, n), F32),
        compiler_params=_params("parallel"),
        name="gate_proj",
    )(x, w)


def _residual_layer_norm(res, y, g, b):
    z = DN_ALPHA * res + y
    mu = jnp.mean(z, axis=-1, keepdims=True)
    zc = z - mu
    var = jnp.mean(zc * zc, axis=-1, keepdims=True)
    return zc * lax.rsqrt(var + LN_EPS) * g + b


def _out_proj_kernel(*refs, n_in):
    x_refs, w_refs = refs[:n_in], refs[n_in:2 * n_in]
    res_ref, g_ref, b_ref, of_ref, ob_ref = refs[2 * n_in:]
    tm = res_ref.shape[0]
    sub = min(PROJ_SUB_ROWS, tm)

    def finish(r, y):
        rows = slice(r * sub, (r + 1) * sub)
        out = _residual_layer_norm(res_ref[rows, :], y, g_ref[...], b_ref[...])
        of_ref[rows, :] = out
        ob_ref[rows, :] = out.astype(BF16)

    pending = None
    for r in range(tm // sub):
        rows = slice(r * sub, (r + 1) * sub)
        y = _dot(x_refs[0][rows, :], w_refs[0][...])
        for x_ref, w_ref in zip(x_refs[1:], w_refs[1:]):
            y += _dot(x_ref[rows, :], w_ref[...])
        if pending is not None:
            finish(*pending)
        pending = (r, y)
    finish(*pending)


def _out_proj_ln(xs, ws, res, g, b, *, tm=512):
    m, d = res.shape
    tm = min(tm, m)
    n_in = len(xs)
    in_specs = [pl.BlockSpec((tm, x.shape[1]), lambda i: (i, 0)) for x in xs]
    in_specs += [pl.BlockSpec(w.shape, lambda i: (0, 0)) for w in ws]
    row = pl.BlockSpec((tm, d), lambda i: (i, 0))
    vec = pl.BlockSpec((1, d), lambda i: (0, 0))
    in_specs += [row, vec, vec]
    return pl.pallas_call(
        functools.partial(_out_proj_kernel, n_in=n_in),
        grid=(m // tm,),
        in_specs=in_specs,
        out_specs=[row, row],
        out_shape=[jax.ShapeDtypeStruct((m, d), F32), jax.ShapeDtypeStruct((m, d), BF16)],
        compiler_params=_params("parallel"),
        name="out_proj_ln",
    )(*xs, *ws, res, g.reshape(1, d), b.reshape(1, d))


def _mlp_kernel(xb_ref, w1_ref, w2_ref, xf_ref, g_ref, b_ref, of_ref, ob_ref, acc_ref):
    f = pl.program_id(1)
    tm = xb_ref.shape[0]
    sub = min(PROJ_SUB_ROWS, tm)
    @pl.when(f == 0)
    def _():
        acc_ref[...] = jnp.zeros(acc_ref.shape, F32)

    hidden = [None] * (tm // sub)
    for r in range(tm // sub + 1):
        if r < tm // sub:
            hidden[r] = _dot(xb_ref[r * sub:(r + 1) * sub, :], w1_ref[...])
        if r > 0:
            rows = slice((r - 1) * sub, r * sub)
            act = jnp.square(jnp.maximum(hidden[r - 1], 0.0)).astype(BF16)
            acc_ref[rows, :] += _dot(act, w2_ref[...])

    @pl.when(f == pl.num_programs(1) - 1)
    def _():
        out = _residual_layer_norm(xf_ref[...], acc_ref[...], g_ref[...], b_ref[...])
        of_ref[...] = out
        ob_ref[...] = out.astype(BF16)


def _mlp_ln(xf, xb, w1, w2, g, b, *, tm=512, tf=1024):
    m, d = xf.shape
    ff = w1.shape[1]
    tm = min(tm, m)
    row = pl.BlockSpec((tm, d), lambda i, f: (i, 0))
    vec = pl.BlockSpec((1, d), lambda i, f: (0, 0))
    return pl.pallas_call(
        _mlp_kernel,
        grid=(m // tm, ff // tf),
        in_specs=[row, pl.BlockSpec((d, tf), lambda i, f: (0, f)), pl.BlockSpec((tf, d), lambda i, f: (f, 0)),
                  row, vec, vec],
        out_specs=[row, row],
        out_shape=[jax.ShapeDtypeStruct((m, d), F32), jax.ShapeDtypeStruct((m, d), BF16)],
        scratch_shapes=[pltpu.VMEM((tm, d), F32)],
        compiler_params=_params("parallel", "arbitrary"),
        name="mlp_ln",
    )(xb, w1, w2, xf, g.reshape(1, d), b.reshape(1, d))


def _banded_kernel(*refs, hq, grp, window, tq, pw, has_sink, want_lse):
    refs = list(refs)
    sink_ref = refs.pop(0) if has_sink else None
    q_ref, kc_ref, kp_ref, vc_ref, vp_ref, o_ref = refs[:6]
    lse_ref = refs[6] if want_lse else None
    bias_ref = refs[-1]
    score_refs = refs[-1 - 3 * hq:-1 - 2 * hq]
    prob_refs = refs[-1 - 2 * hq:-1 - hq]
    linv_refs = refs[-1 - hq:-1]
    step, i = pl.program_id(1), pl.program_id(2)
    span = pw + tq
    rows_blk = min(BANDED_SOFTMAX_ROWS, tq)

    row = lax.broadcasted_iota(jnp.int32, (tq, span), 0)
    rel = lax.broadcasted_iota(jnp.int32, (tq, span), 1) - pw - row
    mask = (rel <= 0) & (rel > -window) & ((rel + row >= 0) | (i > 0))
    bias_ref[...] = jnp.where(mask, 0.0, NEG)

    @pl.when(i == 0)
    def _():
        for h in range(hq):
            prob_refs[h][...] = jnp.zeros(prob_refs[h].shape, prob_refs[h].dtype)

    for h in range(hq):
        kv = h // grp
        q = q_ref[:, h * HEAD_DIM:(h + 1) * HEAD_DIM]
        kcols = slice(kv * HEAD_DIM, (kv + 1) * HEAD_DIM)
        score_refs[h][:, :pw] = _dot_nt(q, kp_ref[:, kcols])
        score_refs[h][:, pw:] = _dot_nt(q, kc_ref[:, kcols])

    for h in range(hq):
        kv = h // grp
        qcols = slice(h * HEAD_DIM, (h + 1) * HEAD_DIM)
        kcols = slice(kv * HEAD_DIM, (kv + 1) * HEAD_DIM)
        if has_sink:
            sink = sink_ref[step * hq + h] * LOG2E
        for rb in range(tq // rows_blk):
            r0 = rb * rows_blk
            rows = slice(r0, r0 + rows_blk)
            cols = slice(max(0, r0 + pw - (window - 1)) // HEAD_DIM * HEAD_DIM,
                         -(-(r0 + rows_blk + pw) // HEAD_DIM) * HEAD_DIM)
            s = score_refs[h][rows, cols] + bias_ref[rows, cols]
            m = jnp.max(s, axis=-1, keepdims=True)
            if has_sink:
                m = jnp.maximum(m, sink)
            p = jnp.exp2(s - m)
            l = jnp.sum(p, axis=-1, keepdims=True)
            if has_sink:
                l = l + jnp.exp2(sink - m)
            prob_refs[h][rows, cols] = p.astype(BF16)
            linv_refs[h][rows, :] = jnp.broadcast_to(1.0 / l, (rows_blk, HEAD_DIM))
            if want_lse:
                lse_ref[rows, h:h + 1] = (m + jnp.log2(l)) * LN2
        o = _dot(prob_refs[h][:, :pw], vp_ref[:, kcols]) + _dot(prob_refs[h][:, pw:], vc_ref[:, kcols])
        o_ref[:, qcols] = (o * linv_refs[h][...]).astype(o_ref.dtype)


def _banded_attention(arr, *, n_steps, hq, grp, window, q_at, k_at, v_at, o_at, o_dims, tq, pw,
                      sinks=None, want_lse=False):
    bsz, _, seq, _ = arr.shape
    tq = min(tq, seq)
    pw = min(pw, tq)
    assert pw >= window - 1 or seq <= pw, (pw, window, seq)
    assert seq % tq == 0 and tq % pw == 0
    hk = hq // grp
    ratio = tq // pw
    wq, wk = hq * HEAD_DIM, hk * HEAD_DIM
    has_sink = sinks is not None

    def spec(width, rows, at_fn, row_fn):
        def index(b, s, i, *_):
            plane, col = at_fn(s)
            return (b, plane, row_fn(i), col)
        return pl.BlockSpec((None, None, rows, width), index)

    cur = lambda i: i
    prev = lambda i: jnp.maximum(i * ratio - 1, 0)
    in_specs = [spec(wq, tq, q_at, cur), spec(wk, tq, k_at, cur), spec(wk, pw, k_at, prev),
                spec(wk, tq, v_at, cur), spec(wk, pw, v_at, prev)]
    out_specs = [spec(wq, tq, o_at, cur)]
    out_shape = [jax.ShapeDtypeStruct((bsz, o_dims[0], seq, o_dims[1]), BF16)]
    if want_lse:
        out_specs.append(pl.BlockSpec((None, None, tq, hq), lambda b, s, i: (b, s, i, 0)))
        out_shape.append(jax.ShapeDtypeStruct((bsz, n_steps, seq, hq), F32))
    kern = functools.partial(_banded_kernel, hq=hq, grp=grp, window=window, tq=tq, pw=pw,
                             has_sink=has_sink, want_lse=want_lse)
    grid = (bsz, n_steps, seq // tq)
    params = _params("parallel", "parallel", "arbitrary")
    scratch = ([pltpu.VMEM((tq, pw + tq), F32)] * hq + [pltpu.VMEM((tq, pw + tq), BF16)] * hq
               + [pltpu.VMEM((tq, HEAD_DIM), F32)] * hq + [pltpu.VMEM((tq, pw + tq), F32)])
    if has_sink:
        grid_spec = pltpu.PrefetchScalarGridSpec(num_scalar_prefetch=1, grid=grid, in_specs=in_specs,
                                                 out_specs=out_specs, scratch_shapes=scratch)
        out = pl.pallas_call(kern, grid_spec=grid_spec, out_shape=out_shape, compiler_params=params,
                             name="banded_attn_sink")(sinks.astype(F32), arr, arr, arr, arr, arr)
    else:
        out = pl.pallas_call(kern, grid=grid, in_specs=in_specs, out_specs=out_specs, out_shape=out_shape,
                             scratch_shapes=scratch, compiler_params=params, name="banded_attn")(arr, arr, arr, arr, arr)
    return out if want_lse else out[0]


def _mix_kernel(o0_ref, o1_ref, o2_ref, l0_ref, l1_ref, l2_ref, out_ref, t1_ref, t2_ref):
    for src, dst in ((o1_ref, t1_ref), (o2_ref, t2_ref)):
        dil, rows = src.shape[0], src.shape[1]
        for c in range(dil):
            for h in range(DIL_HEADS):
                dst[h, pl.ds(c, rows, stride=dil), :] = src[c, :, h * HEAD_DIM:(h + 1) * HEAD_DIM].astype(F32)
    l0, l1, l2 = l0_ref[...], l1_ref[...], l2_ref[...]
    m = jnp.maximum(jnp.maximum(l0, l1), l2)
    e0, e1, e2 = jnp.exp(l0 - m), jnp.exp(l1 - m), jnp.exp(l2 - m)
    inv = 1.0 / (e0 + e1 + e2)
    w0, w1, w2 = e0 * inv, e1 * inv, e2 * inv
    for h in range(DIL_HEADS):
        cols = slice(h * HEAD_DIM, (h + 1) * HEAD_DIM)
        out = (w0[:, h:h + 1] * o0_ref[:, cols].astype(F32)
               + w1[:, h:h + 1] * t1_ref[h]
               + w2[:, h:h + 1] * t2_ref[h])
        out_ref[:, cols] = out.astype(out_ref.dtype)


def _mix_groups(outs, lses, *, seq, tm=512):
    bsz, _, _, d = outs[0].shape
    tm = min(tm, seq)
    tiles_per_batch = seq // tm

    def o_spec(dil):
        return pl.BlockSpec((None, dil, tm // dil, d), lambda i: (i // tiles_per_batch, 0, i % tiles_per_batch, 0))

    row = pl.BlockSpec((tm, d), lambda i: (i, 0))
    lrow = pl.BlockSpec((tm, DIL_HEADS), lambda i: (i, 0))
    dils = [o.shape[1] for o in outs]
    assert dils[0] == 1
    return pl.pallas_call(
        _mix_kernel,
        grid=(bsz * tiles_per_batch,),
        in_specs=[row, o_spec(dils[1]), o_spec(dils[2]), lrow, lrow, lrow],
        out_specs=row,
        out_shape=jax.ShapeDtypeStruct((bsz * seq, d), BF16),
        scratch_shapes=[pltpu.VMEM((DIL_HEADS, tm, HEAD_DIM), F32)] * 2,
        compiler_params=_params("parallel"),
        name="dilated_mix",
    )(outs[0].reshape(bsz * seq, d), outs[1], outs[2], *lses)


def _compress_kernel(x_ref, pe_lo_ref, pe_hi_ref, w1_lo_ref, w1_hi_ref, w2_ref, o_ref, shift_ref):
    n = x_ref.shape[0]
    x = x_ref[...].astype(F32)
    first = _dot((x + pe_lo_ref[...]).astype(BF16), w1_lo_ref[...])
    second = _dot((x + pe_hi_ref[...]).astype(BF16), w1_hi_ref[...])
    shift_ref[0:n, :] = second
    shift_ref[n:n + 8, :] = jnp.zeros((8, HEAD_DIM), F32)
    h = first + shift_ref[1:n + 1, :]
    o_ref[...] = _dot(jax.nn.gelu(h).astype(BF16), w2_ref[...]).astype(o_ref.dtype)


def _compress(rows, pe, w1, w2):
    bg, n, width = rows.shape
    half = CMP_STRIDE * HEAD_DIM
    pe_lo = pe[:CMP_STRIDE].reshape(1, half)
    pe_hi = pe[CMP_STRIDE:].reshape(1, half)
    w1_lo = w1[:half].astype(BF16)
    w1_hi = w1[half:].astype(BF16)
    const = lambda shape: pl.BlockSpec(shape, lambda i: (0, 0))
    return pl.pallas_call(
        _compress_kernel,
        grid=(bg,),
        in_specs=[pl.BlockSpec((None, n, width), lambda i: (i, 0, 0)), const((1, half)), const((1, half)),
                  const((half, HEAD_DIM)), const((half, HEAD_DIM)), const((HEAD_DIM, HEAD_DIM))],
        out_specs=pl.BlockSpec((None, n, HEAD_DIM), lambda i: (i, 0, 0)),
        out_shape=jax.ShapeDtypeStruct((bg, n, HEAD_DIM), BF16),
        scratch_shapes=[pltpu.VMEM((n + 8, HEAD_DIM), F32)],
        compiler_params=_params("parallel"),
        name="nsa_compress",
    )(rows, pe_lo, pe_hi, w1_lo, w1_hi, w2.astype(BF16))


def _nsa_kernel(q_ref, kc_ref, vc_ref, c2s_ref, ks_ref, vs_ref, ow_ref, g_ref, o_ref,
                kaug_ref, qaug_ref, s0_ref, s1_ref, p0_ref, p1_ref, a0_ref, a1_ref, m_ref, l_ref, acc_ref, ocmp_ref,
                *, tq, kt, n_blk, n_sel, n_var):
    grp = NSA_GROUP
    n_cmp = kc_ref.shape[0]
    seq = ks_ref.shape[0]
    n_halves = qaug_ref.shape[0]
    chunks_per_half = HEAD_DIM * SEL_BLOCK // kt
    q0 = pl.program_id(2) * tq

    @pl.when(pl.program_id(2) == 0)
    def _():
        def fill(r, carry):
            r0 = pl.multiple_of(r * kt, kt)
            blk_of_row = (lax.broadcasted_iota(jnp.int32, (kt, HEAD_DIM), 0) + r0) // SEL_BLOCK
            lane = lax.broadcasted_iota(jnp.int32, (kt, HEAD_DIM), 1)
            kaug_ref[pl.ds(r0, kt), 0:HEAD_DIM] = ks_ref[pl.ds(r0, kt), :]
            kaug_ref[pl.ds(r0, kt), HEAD_DIM:2 * HEAD_DIM] = jnp.where(
                lane == blk_of_row % HEAD_DIM, NEG, 0.0).astype(kaug_ref.dtype)
            return carry
        lax.fori_loop(0, seq // kt, fill, 0)

    qs = jnp.concatenate([q_ref[:, h * HEAD_DIM:(h + 1) * HEAD_DIM] for h in range(grp)], axis=0)
    qpos = q0 + lax.broadcasted_iota(jnp.int32, (tq, 1), 0)

    def select_blocks(n_vis, n_rows):
        cmp_end = lax.broadcasted_iota(jnp.int32, (tq, n_vis), 1) * CMP_STRIDE + (CMP_BLOCK - 1)
        cmp_bias = jnp.where(cmp_end <= qpos, 0.0, NEG)
        s = _dot_nt(qs, kc_ref[0:n_vis, :]).reshape(grp, tq, n_vis) + cmp_bias[None]
        m = jnp.max(s, axis=-1, keepdims=True)
        m = jnp.where(m < 0.5 * NEG, 0.0, m)
        p = jnp.exp2(s - m)
        l = jnp.sum(p, axis=-1, keepdims=True)
        p = p * (1.0 / jnp.where(l > 0.0, l, 1.0))
        ocmp_ref[...] = _dot(p.reshape(grp * tq, n_vis).astype(BF16), vc_ref[0:n_vis, :])

        p_sum = jnp.sum(p, axis=0)
        p_hi = p_sum.astype(BF16)
        p_lo = (p_sum - p_hi.astype(F32)).astype(BF16)
        c2s = c2s_ref[0:n_rows, 0:n_vis]
        imp = _dot_nt(c2s, p_hi) + _dot_nt(c2s, p_lo)

        blk = lax.broadcasted_iota(jnp.int32, (n_rows, tq), 0)
        qpos_lane = q0 + lax.broadcasted_iota(jnp.int32, (1, tq), 1)
        cur = qpos_lane // SEL_BLOCK
        forced = (blk == 0) | (blk == cur) | (blk == cur - 1)
        score = jnp.where(forced, jnp.inf, jnp.where(blk * SEL_BLOCK <= qpos_lane, imp, -1.0))
        blk_f = blk.astype(F32)
        not_sel = jnp.ones((n_rows, tq), F32)
        for _ in range(n_sel):
            best = jnp.max(score, axis=0, keepdims=True)
            first = jnp.min(jnp.where(score == best, blk_f, float(n_rows)), axis=0, keepdims=True)
            hit = blk_f == first
            not_sel = jnp.where(hit, 0.0, not_sel)
            score = jnp.where(hit, -2.0, score)
        if n_rows < n_pad:
            not_sel = jnp.concatenate([not_sel, jnp.ones((n_pad - n_rows, tq), F32)], axis=0)
        not_sel = not_sel.T
        for half in range(n_halves):
            flags = not_sel[:, half * HEAD_DIM:(half + 1) * HEAD_DIM].astype(qaug_ref.dtype)
            qaug_ref[half, :, 0:HEAD_DIM] = qs
            qaug_ref[half, :, HEAD_DIM:2 * HEAD_DIM] = jnp.concatenate([flags] * grp, axis=0)

    n_pad = c2s_ref.shape[0]
    for v in range(n_var):
        @pl.when(q0 // (seq // n_var) == v)
        def _(v=v):
            select_blocks((v + 1) * (n_cmp // n_var), (v + 1) * (n_pad // n_var))

    rows_blk = SOFTMAX_ROWS
    bufs = ((s0_ref, p0_ref, a0_ref), (s1_ref, p1_ref, a1_ref))
    m_ref[...] = jnp.full(m_ref.shape, NEG, F32)
    l_ref[...] = jnp.zeros(l_ref.shape, F32)
    acc_ref[...] = jnp.zeros(acc_ref.shape, F32)

    def scores(j, slot):
        k0 = pl.multiple_of(j * kt, kt)
        bufs[slot][0][...] = _dot_nt(qaug_ref[j // chunks_per_half], kaug_ref[pl.ds(k0, kt), :])

    def softmax(j, slot, causal):
        s_buf, p_buf, a_buf = bufs[slot]
        k0 = j * kt
        for rb in range(grp * tq // rows_blk):
            rows = slice(rb * rows_blk, (rb + 1) * rows_blk)
            sb = s_buf[rows, :]
            if causal:
                row_pos = q0 + (rb * rows_blk) % tq + lax.broadcasted_iota(jnp.int32, (rows_blk, 1), 0)
                key_pos = k0 + lax.broadcasted_iota(jnp.int32, (rows_blk, kt), 1)
                sb = jnp.where(key_pos <= row_pos, sb, NEG)
            m_prev = m_ref[rows, :]
            m_new = jnp.maximum(m_prev, jnp.max(sb, axis=-1, keepdims=True))
            alpha = jnp.exp2(m_prev - m_new)
            p = jnp.exp2(sb - jnp.concatenate([m_new] * (kt // HEAD_DIM), axis=1))
            l_ref[rows, :] = alpha * l_ref[rows, :] + jnp.sum(p, axis=-1, keepdims=True)
            m_ref[rows, :] = m_new
            a_buf[rows, :] = alpha
            p_buf[rows, :] = p.astype(p_buf.dtype)

    def weighted_values(j, slot):
        _, p_buf, a_buf = bufs[slot]
        k0 = pl.multiple_of(j * kt, kt)
        acc_ref[...] = a_buf[...] * acc_ref[...] + _dot(p_buf[...], vs_ref[pl.ds(k0, kt), :])

    n_interior = q0 // kt
    scores(0, 0)

    def interior_run(j0, count):
        for c in range(count):
            scores(j0 + c + 1, (c + 1) % 2)
            softmax(j0 + c, c % 2, causal=False)
            weighted_values(j0 + c, c % 2)

    unroll = NSA_CHUNK_UNROLL
    n_runs = n_interior // unroll

    def interior_body(r, carry):
        interior_run(r * unroll, unroll)
        return carry

    lax.fori_loop(0, n_runs, interior_body, 0)
    for count in (c for c in (4, 2) if c < unroll):
        @pl.when((n_interior % (2 * count)) // count == 1)
        def _(count=count):
            interior_run(n_interior // (2 * count) * (2 * count), count)

    @pl.when(n_interior % 2 == 0)
    def _():
        softmax(n_interior, 0, causal=True)
        weighted_values(n_interior, 0)

    @pl.when(n_interior % 2 == 1)
    def _():
        scores(n_interior, 1)
        softmax(n_interior - 1, 0, causal=False)
        weighted_values(n_interior - 1, 0)
        softmax(n_interior, 1, causal=True)
        weighted_values(n_interior, 1)

    o_slc = acc_ref[...] * (1.0 / l_ref[...])

    gates = g_ref[...]
    for h in range(grp):
        rows = slice(h * tq, (h + 1) * tq)
        cols = slice(h * HEAD_DIM, (h + 1) * HEAD_DIM)
        out = (gates[:, 3 * h:3 * h + 1] * ocmp_ref[rows, :]
               + gates[:, 3 * h + 1:3 * h + 2] * o_slc[rows]
               + gates[:, 3 * h + 2:3 * h + 3] * ow_ref[:, cols].astype(F32))
        o_ref[:, cols] = out.astype(o_ref.dtype)


def _sel_from_cmp(n_cmp_pad, n_blk_pad):
    pos = np.arange(n_cmp_pad)[:, None] * CMP_STRIDE + np.arange(CMP_BLOCK)[None, :]
    owner = pos // SEL_BLOCK
    frac = (owner[:, :, None] == np.arange(n_blk_pad)[None, None, :]).sum(axis=1) / CMP_BLOCK
    return jnp.asarray(frac.T, dtype=BF16)


def _nsa_attention(proj, kc, vc, o_win, gates, *, q_idx, ks_idx, vs_idx, tq=256, kt=512):
    bsz, seq, _ = proj.shape
    tq = min(tq, seq)
    kt = min(kt, seq)
    assert kt % tq == 0 and (HEAD_DIM * SEL_BLOCK) % kt == 0 and seq % kt == 0
    n_cmp = kc.shape[1]
    n_blk = seq // SEL_BLOCK
    n_sel = min(N_SELECT, n_blk)
    n_halves = -(-n_blk // HEAD_DIM)
    wq = NSA_GROUP * HEAD_DIM
    rows = NSA_GROUP * tq
    c2s = _sel_from_cmp(n_cmp, n_halves * HEAD_DIM)
    n_var = max(1, min(NSA_PREFIX_VARIANTS, n_cmp // (2 * HEAD_DIM)))
    assert seq % n_var == 0 and (seq // n_var) % tq == 0 and (n_halves * HEAD_DIM) % (8 * n_var) == 0
    kern = functools.partial(_nsa_kernel, tq=tq, kt=kt, n_blk=n_blk, n_sel=n_sel, n_var=n_var)
    full = lambda idx_fn: pl.BlockSpec((None, seq, HEAD_DIM), lambda b, g, i: (b, 0, idx_fn(g)))
    cmp_spec = pl.BlockSpec((None, n_cmp, HEAD_DIM), lambda b, g, i: (b * NSA_KV_HEADS + g, 0, 0))
    return pl.pallas_call(
        kern,
        grid=(bsz, NSA_KV_HEADS, seq // tq),
        in_specs=[
            pl.BlockSpec((None, tq, wq), lambda b, g, i: (b, i, q_idx(g))),
            cmp_spec, cmp_spec,
            pl.BlockSpec(c2s.shape, lambda b, g, i: (0, 0)),
            full(ks_idx), full(vs_idx),
            pl.BlockSpec((None, tq, wq), lambda b, g, i: (b, i, g)),
            pl.BlockSpec((None, tq, HEAD_DIM), lambda b, g, i: (b, i, g)),
        ],
        out_specs=pl.BlockSpec((None, tq, wq), lambda b, g, i: (b, i, g)),
        out_shape=jax.ShapeDtypeStruct((bsz, seq, NSA_KV_HEADS * wq), BF16),
        scratch_shapes=[pltpu.VMEM((seq, 2 * HEAD_DIM), BF16),
                        pltpu.VMEM((n_halves, rows, 2 * HEAD_DIM), BF16),
                        pltpu.VMEM((rows, kt), F32), pltpu.VMEM((rows, kt), F32),
                        pltpu.VMEM((rows, kt), BF16), pltpu.VMEM((rows, kt), BF16)]
                       + [pltpu.VMEM((rows, HEAD_DIM), F32)] * 6,
        compiler_params=_params("parallel", "parallel", "arbitrary"),
        name="nsa_core",
    )(proj, kc, vc, c2s, proj, proj, o_win, gates)


_BLK_AQ, _BLK_BQ = 0, NSA_HEADS
_BLK_AK = NSA_HEADS + SWA_HEADS
_BLK_BK = _BLK_AK + 3 * NSA_KV_HEADS
_BLK_AV = _BLK_BK + SWA_KV_HEADS
_BLK_BV = _BLK_AV + 3 * NSA_KV_HEADS


def _even_weights(w_in):
    aq, ak, av, ag, bq, bk, bv = jnp.split(w_in, np.cumsum(EVEN_WIDTHS)[:-1].tolist(), axis=-1)
    w_main = jnp.concatenate([aq, bq, ak, bk, av, bv], axis=-1).astype(BF16)
    ag = ag.reshape(D_MODEL, NSA_KV_HEADS, 3 * NSA_GROUP)
    w_gate = jnp.pad(ag, ((0, 0), (0, 0), (0, HEAD_DIM - 3 * NSA_GROUP))).reshape(D_MODEL, NSA_KV_HEADS * HEAD_DIM)
    return w_main, w_gate.astype(BF16)


def _tile_modes(blocks_per_tile, block_modes):
    modes = np.asarray(block_modes, np.int32).reshape(-1, blocks_per_tile)
    assert (modes == modes[:, :1]).all(), "a projection tile must not straddle q/k/v column groups"
    return jnp.asarray(modes[:, 0])


def _even_mixer(xf, xb, cos, sin, bsz, seq, w_in, w_o, pe_k, pe_v, ck_w1, ck_w2, cv_w1, cv_w2, sinks, g, b):
    w_main, w_gate = _even_weights(w_in)
    tn = 1024
    n_q, n_k = NSA_HEADS + SWA_HEADS, 3 * NSA_KV_HEADS + SWA_KV_HEADS
    modes = _tile_modes(tn // HEAD_DIM, [MODE_ROPE_SCALED] * n_q + [MODE_ROPE] * n_k + [MODE_PLAIN] * n_k)
    proj4 = _project(xb, w_main, modes, cos, sin, seq=seq, tn=tn)
    proj = proj4.reshape(bsz, seq, EVEN_PROJ)
    gates = _gate_project(xb, w_gate).reshape(bsz, seq, NSA_KV_HEADS * HEAD_DIM)

    def cmp_rows(blk0):
        t = proj[:, :, blk0 * HEAD_DIM:(blk0 + NSA_KV_HEADS) * HEAD_DIM]
        t = t.reshape(bsz, seq // CMP_STRIDE, CMP_STRIDE, NSA_KV_HEADS, HEAD_DIM).transpose(0, 3, 1, 2, 4)
        return t.reshape(bsz * NSA_KV_HEADS, seq // CMP_STRIDE, CMP_STRIDE * HEAD_DIM)

    kc = _compress(cmp_rows(_BLK_AK), pe_k, ck_w1, ck_w2)
    vc = _compress(cmp_rows(_BLK_AV), pe_v, cv_w1, cv_w2)
    o_win = _banded_attention(
        proj4, n_steps=NSA_KV_HEADS, hq=NSA_GROUP, grp=NSA_GROUP, window=NSA_WINDOW,
        q_at=lambda s: (0, s), k_at=lambda s: (0, _BLK_AK + 2 * NSA_KV_HEADS + s),
        v_at=lambda s: (0, _BLK_AV + 2 * NSA_KV_HEADS + s), o_at=lambda s: (0, s), o_dims=(1, NSA_Q),
        tq=512, pw=512).reshape(bsz, seq, NSA_Q)
    o_a = _nsa_attention(proj, kc, vc, o_win, gates, q_idx=lambda s: s,
                         ks_idx=lambda s: _BLK_AK + NSA_KV_HEADS + s, vs_idx=lambda s: _BLK_AV + NSA_KV_HEADS + s)
    swa_grp = SWA_HEADS // SWA_KV_HEADS
    o_b = _banded_attention(
        proj4, n_steps=SWA_KV_HEADS, hq=swa_grp, grp=swa_grp, window=SWA_WINDOW,
        q_at=lambda s: (0, NSA_HEADS // swa_grp + s), k_at=lambda s: (0, _BLK_BK + s),
        v_at=lambda s: (0, _BLK_BV + s), o_at=lambda s: (0, s), o_dims=(1, SWA_Q),
        tq=256, pw=128, sinks=sinks)
    m = bsz * seq
    w_o = w_o.astype(BF16)
    return _out_proj_ln([o_a.reshape(m, NSA_Q), o_b.reshape(m, SWA_Q)], [w_o[:NSA_Q], w_o[NSA_Q:]], xf, g, b)


def _odd_mixer(xf, xb, cos, sin, bsz, seq, w_in, w_o, g, b):
    tn = 1024
    modes = _tile_modes(tn // HEAD_DIM, [MODE_ROPE_SCALED] * DIL_HEADS + [MODE_ROPE] * DIL_HEADS + [MODE_PLAIN] * DIL_HEADS)
    m = bsz * seq
    width = DIL_HEADS * HEAD_DIM
    outs, lses = [], []
    for grp_i, (window, dil) in enumerate(DIL_PATTERNS):
        w_grp = w_in[:, grp_i * DIL_QKV:(grp_i + 1) * DIL_QKV].astype(BF16)
        qkv = _project(xb, w_grp, modes, cos, sin, seq=seq, dil=dil, tn=tn)
        o, lse = _banded_attention(
            qkv, n_steps=dil, hq=DIL_HEADS, grp=1, window=window // dil + 1,
            q_at=lambda s: (s, 0), k_at=lambda s: (s, 1), v_at=lambda s: (s, 2), o_at=lambda s: (s, 0),
            o_dims=(dil, width), tq=256, pw=128, want_lse=True)
        outs.append(o)
        lses.append(lse.transpose(0, 2, 1, 3).reshape(m, DIL_HEADS))
    mixed = _mix_groups(outs, lses, seq=seq)
    return _out_proj_ln([mixed], [w_o.astype(BF16)], xf, g, b)


def kernel(x, positions, e_w_in, e_w_o, nsa_pe_k, nsa_pe_v, nsa_ck_w1, nsa_ck_w2, nsa_cv_w1, nsa_cv_w2, swa_sinks, o_w_in, o_w_o, ln1_g, ln1_b, mlp_w1, mlp_w2, ln2_g, ln2_b):
    bsz, seq, d = x.shape
    m = bsz * seq
    cos, sin = _rope_tables(positions)
    xf = x.reshape(m, d)
    xb = xf.astype(BF16)
    for layer in range(DEPTH):
        i = layer // 2
        if layer % 2 == 0:
            xf, xb = _even_mixer(xf, xb, cos, sin, bsz, seq, e_w_in[i], e_w_o[i], nsa_pe_k[i], nsa_pe_v[i],
                                 nsa_ck_w1[i], nsa_ck_w2[i], nsa_cv_w1[i], nsa_cv_w2[i], swa_sinks[i],
                                 ln1_g[layer], ln1_b[layer])
        else:
            xf, xb = _odd_mixer(xf, xb, cos, sin, bsz, seq, o_w_in[i], o_w_o[i], ln1_g[layer], ln1_b[layer])
        xf, xb = _mlp_ln(xf, xb, mlp_w1[layer].astype(BF16), mlp_w2[layer].astype(BF16), ln2_g[layer], ln2_b[layer])
    return xf.reshape(bsz, seq, d)
```

```python
import functools
import math

import numpy as np
import jax
import jax.numpy as jnp
from jax import lax
from jax.experimental import pallas as pl
from jax.experimental.pallas import tpu as pltpu

D_MODEL = 2048
DEPTH = 4
HEAD_DIM = 128
ROPE_THETA = 10000.0
LN_EPS = 1e-5

NSA_HEADS = D_MODEL // (2 * HEAD_DIM)
NSA_KV_HEADS = 2
NSA_GROUP = NSA_HEADS // NSA_KV_HEADS
CMP_BLOCK = 32
CMP_STRIDE = 16
SEL_BLOCK = 64
N_SELECT = 16
NSA_WINDOW = 512

SWA_HEADS = D_MODEL // (2 * HEAD_DIM)
SWA_KV_HEADS = 2
SWA_WINDOW = 128

DIL_HEADS = D_MODEL // HEAD_DIM
DIL_PATTERNS = ((128, 1), (512, 4), (2048, 16))

D_FF = 4 * D_MODEL
DN_ALPHA = (2 * DEPTH) ** 0.25

NSA_Q = NSA_HEADS * HEAD_DIM
NSA_KV = NSA_KV_HEADS * HEAD_DIM
SWA_Q = SWA_HEADS * HEAD_DIM
SWA_KV = SWA_KV_HEADS * HEAD_DIM
EVEN_WIDTHS = (NSA_Q, 3 * NSA_KV, 3 * NSA_KV, 3 * NSA_HEADS, SWA_Q, SWA_KV, SWA_KV)
EVEN_PROJ = NSA_Q + SWA_Q + 3 * NSA_KV + SWA_KV + 3 * NSA_KV + SWA_KV
DIL_QKV = 3 * DIL_HEADS * HEAD_DIM

LOG2E = math.log2(math.e)
LN2 = math.log(2.0)
QK_SCALE = HEAD_DIM ** -0.5 * LOG2E
NEG = -1e30
SOFTMAX_ROWS = 32
BANDED_SOFTMAX_ROWS = 32
NSA_CHUNK_UNROLL = 4
NSA_PREFIX_VARIANTS = 4
PROJ_SUB_ROWS = 256
VMEM_LIMIT_BYTES = 56 * 1024 * 1024

MODE_PLAIN, MODE_ROPE, MODE_ROPE_SCALED = 0, 1, 2

F32 = jnp.float32
BF16 = jnp.bfloat16


def _params(*semantics):
    return pltpu.CompilerParams(dimension_semantics=semantics, vmem_limit_bytes=VMEM_LIMIT_BYTES)


def _dot(a, b):
    return jnp.dot(a, b, preferred_element_type=F32)


def _dot_nt(a, b):
    return lax.dot_general(a, b, (((1,), (1,)), ((), ())), preferred_element_type=F32)


def _rope_table_kernel(pos_ref, inv_ref, sign_ref, cos_ref, sin_ref):
    ang = pos_ref[...].astype(F32) * inv_ref[...]
    cos_ref[...] = jnp.cos(ang)
    sin_ref[...] = jnp.sin(ang) * sign_ref[...]


def _rope_tables(positions):
    m = positions.size
    tm = min(m, 2048)
    inv = ROPE_THETA ** (-jnp.arange(0, HEAD_DIM, 2, dtype=F32) / HEAD_DIM)
    inv_full = jnp.concatenate([inv, inv]).reshape(1, HEAD_DIM)
    half = HEAD_DIM // 2
    sign = jnp.concatenate([-jnp.ones((half,), F32), jnp.ones((half,), F32)]).reshape(1, HEAD_DIM)
    row = pl.BlockSpec((tm, HEAD_DIM), lambda i: (i, 0))
    const = pl.BlockSpec((1, HEAD_DIM), lambda i: (0, 0))
    return pl.pallas_call(
        _rope_table_kernel,
        grid=(m // tm,),
        in_specs=[pl.BlockSpec((tm, 1), lambda i: (i, 0)), const, const],
        out_specs=[row, row],
        out_shape=[jax.ShapeDtypeStruct((m, HEAD_DIM), F32)] * 2,
        compiler_params=_params("parallel"),
        name="rope_tables",
    )(positions.reshape(m, 1), inv_full, sign)


def _proj_kernel(modes_ref, x_ref, w_ref, cos_ref, sin_ref, o_ref, *scratch, tn, dil):
    mode = modes_ref[pl.program_id(1)]
    tm = x_ref.shape[0]
    sub = min(PROJ_SUB_ROWS, tm)
    blocks = [slice(blk * HEAD_DIM, (blk + 1) * HEAD_DIM) for blk in range(tn // HEAD_DIM)]
    is_rope = mode != MODE_PLAIN
    scale = jnp.where(mode == MODE_ROPE_SCALED, QK_SCALE, 1.0).astype(F32)

    def epilogue(r, acc):
        rows = slice(r * sub, (r + 1) * sub)
        cos = jnp.where(is_rope, cos_ref[rows, :] * scale, 1.0)
        sin = jnp.where(is_rope, sin_ref[rows, :] * scale, 0.0)
        for blk, cols in enumerate(blocks):
            t = acc[:, cols]
            val = t * cos + pltpu.roll(t, HEAD_DIM // 2, 1) * sin
            if dil == 1:
                o_ref[rows, cols] = val.astype(o_ref.dtype)
            else:
                scratch[0][blk, rows, :] = val
        if dil > 1:
            n = sub // dil
            for c in range(dil):
                for blk, cols in enumerate(blocks):
                    o_ref[c, r * n:(r + 1) * n, cols] = scratch[0][
                        blk, pl.ds(r * sub + c, n, stride=dil), :].astype(o_ref.dtype)

    pending = None
    for r in range(tm // sub):
        acc = _dot(x_ref[r * sub:(r + 1) * sub, :], w_ref[...])
        if pending is not None:
            epilogue(*pending)
        pending = (r, acc)
    epilogue(*pending)


def _project(x, w, modes, cos, sin, *, seq, dil=1, tm=1024, tn=1024):
    m, k = x.shape
    n = w.shape[1]
    tm = min(tm, seq)
    tiles_per_batch = seq // tm
    bsz = m // seq
    if dil == 1:
        out_spec = pl.BlockSpec((tm, tn), lambda i, j, modes: (i, j))
        out_shape = jax.ShapeDtypeStruct((m, n), BF16)
        scratch = []
    else:
        out_spec = pl.BlockSpec((None, dil, tm // dil, tn),
                                lambda i, j, modes: (i // tiles_per_batch, 0, i % tiles_per_batch, j))
        out_shape = jax.ShapeDtypeStruct((bsz, dil, seq // dil, n), BF16)
        scratch = [pltpu.VMEM((tn // HEAD_DIM, tm, HEAD_DIM), F32)]
    grid_spec = pltpu.PrefetchScalarGridSpec(
        num_scalar_prefetch=1,
        grid=(m // tm, n // tn),
        in_specs=[
            pl.BlockSpec((tm, k), lambda i, j, modes: (i, 0)),
            pl.BlockSpec((k, tn), lambda i, j, modes: (0, j)),
            pl.BlockSpec((tm, HEAD_DIM), lambda i, j, modes: (i, 0)),
            pl.BlockSpec((tm, HEAD_DIM), lambda i, j, modes: (i, 0)),
        ],
        out_specs=out_spec,
        scratch_shapes=scratch,
    )
    out = pl.pallas_call(
        functools.partial(_proj_kernel, tn=tn, dil=dil),
        grid_spec=grid_spec,
        out_shape=out_shape,
        compiler_params=_params("parallel", "arbitrary"),
        name="in_proj" if dil == 1 else "in_proj_dil%d" % dil,
    )(modes, x, w, cos, sin)
    return out.reshape(bsz, 1, seq, n) if dil == 1 else out


def _gate_kernel(x_ref, w_ref, o_ref):
    o_ref[...] = jax.nn.sigmoid(_dot(x_ref[...], w_ref[...]))


def _gate_project(x, w, *, tm=1024):
    m, k = x.shape
    n = w.shape[1]
    tm = min(tm, m)
    return pl.pallas_call(
        _gate_kernel,
        grid=(m // tm,),
        in_specs=[pl.BlockSpec((tm, k), lambda i: (i, 0)), pl.BlockSpec((k, n), lambda i: (0, 0))],
        out_specs=pl.BlockSpec((tm, n), lambda i: (i, 0)),
        out_shape=jax.ShapeDtypeStruct((m, n), F32),
        compiler_params=_params("parallel"),
        name="gate_proj",
    )(x, w)


def _residual_layer_norm(res, y, g, b):
    z = DN_ALPHA * res + y
    mu = jnp.mean(z, axis=-1, keepdims=True)
    zc = z - mu
    var = jnp.mean(zc * zc, axis=-1, keepdims=True)
    return zc * lax.rsqrt(var + LN_EPS) * g + b


def _out_proj_kernel(*refs, n_in):
    x_refs, w_refs = refs[:n_in], refs[n_in:2 * n_in]
    res_ref, g_ref, b_ref, of_ref, ob_ref = refs[2 * n_in:]
    tm = res_ref.shape[0]
    sub = min(PROJ_SUB_ROWS, tm)

    def finish(r, y):
        rows = slice(r * sub, (r + 1) * sub)
        out = _residual_layer_norm(res_ref[rows, :], y, g_ref[...], b_ref[...])
        of_ref[rows, :] = out
        ob_ref[rows, :] = out.astype(BF16)

    pending = None
    for r in range(tm // sub):
        rows = slice(r * sub, (r + 1) * sub)
        y = _dot(x_refs[0][rows, :], w_refs[0][...])
        for x_ref, w_ref in zip(x_refs[1:], w_refs[1:]):
            y += _dot(x_ref[rows, :], w_ref[...])
        if pending is not None:
            finish(*pending)
        pending = (r, y)
    finish(*pending)


def _out_proj_ln(xs, ws, res, g, b, *, tm=512):
    m, d = res.shape
    tm = min(tm, m)
    n_in = len(xs)
    in_specs = [pl.BlockSpec((tm, x.shape[1]), lambda i: (i, 0)) for x in xs]
    in_specs += [pl.BlockSpec(w.shape, lambda i: (0, 0)) for w in ws]
    row = pl.BlockSpec((tm, d), lambda i: (i, 0))
    vec = pl.BlockSpec((1, d), lambda i: (0, 0))
    in_specs += [row, vec, vec]
    return pl.pallas_call(
        functools.partial(_out_proj_kernel, n_in=n_in),
        grid=(m // tm,),
        in_specs=in_specs,
        out_specs=[row, row],
        out_shape=[jax.ShapeDtypeStruct((m, d), F32), jax.ShapeDtypeStruct((m, d), BF16)],
        compiler_params=_params("parallel"),
        name="out_proj_ln",
    )(*xs, *ws, res, g.reshape(1, d), b.reshape(1, d))


def _mlp_kernel(xb_ref, w1_ref, w2_ref, xf_ref, g_ref, b_ref, of_ref, ob_ref, acc_ref):
    f = pl.program_id(1)
    tm = xb_ref.shape[0]
    sub = min(PROJ_SUB_ROWS, tm)
    @pl.when(f == 0)
    def _():
        acc_ref[...] = jnp.zeros(acc_ref.shape, F32)

    hidden = [None] * (tm // sub)
    for r in range(tm // sub + 1):
        if r < tm // sub:
            hidden[r] = _dot(xb_ref[r * sub:(r + 1) * sub, :], w1_ref[...])
        if r > 0:
            rows = slice((r - 1) * sub, r * sub)
            act = jnp.square(jnp.maximum(hidden[r - 1], 0.0)).astype(BF16)
            acc_ref[rows, :] += _dot(act, w2_ref[...])

    @pl.when(f == pl.num_programs(1) - 1)
    def _():
        out = _residual_layer_norm(xf_ref[...], acc_ref[...], g_ref[...], b_ref[...])
        of_ref[...] = out
        ob_ref[...] = out.astype(BF16)


def _mlp_ln(xf, xb, w1, w2, g, b, *, tm=512, tf=1024):
    m, d = xf.shape
    ff = w1.shape[1]
    tm = min(tm, m)
    row = pl.BlockSpec((tm, d), lambda i, f: (i, 0))
    vec = pl.BlockSpec((1, d), lambda i, f: (0, 0))
    return pl.pallas_call(
        _mlp_kernel,
        grid=(m // tm, ff // tf),
        in_specs=[row, pl.BlockSpec((d, tf), lambda i, f: (0, f)), pl.BlockSpec((tf, d), lambda i, f: (f, 0)),
                  row, vec, vec],
        out_specs=[row, row],
        out_shape=[jax.ShapeDtypeStruct((m, d), F32), jax.ShapeDtypeStruct((m, d), BF16)],
        scratch_shapes=[pltpu.VMEM((tm, d), F32)],
        compiler_params=_params("parallel", "arbitrary"),
        name="mlp_ln",
    )(xb, w1, w2, xf, g.reshape(1, d), b.reshape(1, d))


def _banded_kernel(*refs, hq, grp, window, tq, pw, has_sink, want_lse):
    refs = list(refs)
    sink_ref = refs.pop(0) if has_sink else None
    q_ref, kc_ref, kp_ref, vc_ref, vp_ref, o_ref = refs[:6]
    lse_ref = refs[6] if want_lse else None
    bias_ref = refs[-1]
    score_refs = refs[-1 - 3 * hq:-1 - 2 * hq]
    prob_refs = refs[-1 - 2 * hq:-1 - hq]
    linv_refs = refs[-1 - hq:-1]
    step, i = pl.program_id(1), pl.program_id(2)
    span = pw + tq
    rows_blk = min(BANDED_SOFTMAX_ROWS, tq)

    row = lax.broadcasted_iota(jnp.int32, (tq, span), 0)
    rel = lax.broadcasted_iota(jnp.int32, (tq, span), 1) - pw - row
    mask = (rel <= 0) & (rel > -window) & ((rel + row >= 0) | (i > 0))
    bias_ref[...] = jnp.where(mask, 0.0, NEG)

    @pl.when(i == 0)
    def _():
        for h in range(hq):
            prob_refs[h][...] = jnp.zeros(prob_refs[h].shape, prob_refs[h].dtype)

    for h in range(hq):
        kv = h // grp
        q = q_ref[:, h * HEAD_DIM:(h + 1) * HEAD_DIM]
        kcols = slice(kv * HEAD_DIM, (kv + 1) * HEAD_DIM)
        score_refs[h][:, :pw] = _dot_nt(q, kp_ref[:, kcols])
        score_refs[h][:, pw:] = _dot_nt(q, kc_ref[:, kcols])

    for h in range(hq):
        kv = h // grp
        qcols = slice(h * HEAD_DIM, (h + 1) * HEAD_DIM)
        kcols = slice(kv * HEAD_DIM, (kv + 1) * HEAD_DIM)
        if has_sink:
            sink = sink_ref[step * hq + h] * LOG2E
        for rb in range(tq // rows_blk):
            r0 = rb * rows_blk
            rows = slice(r0, r0 + rows_blk)
            cols = slice(max(0, r0 + pw - (window - 1)) // HEAD_DIM * HEAD_DIM,
                         -(-(r0 + rows_blk + pw) // HEAD_DIM) * HEAD_DIM)
            s = score_refs[h][rows, cols] + bias_ref[rows, cols]
            m = jnp.max(s, axis=-1, keepdims=True)
            if has_sink:
                m = jnp.maximum(m, sink)
            p = jnp.exp2(s - m)
            l = jnp.sum(p, axis=-1, keepdims=True)
            if has_sink:
                l = l + jnp.exp2(sink - m)
            prob_refs[h][rows, cols] = p.astype(BF16)
            linv_refs[h][rows, :] = jnp.broadcast_to(1.0 / l, (rows_blk, HEAD_DIM))
            if want_lse:
                lse_ref[rows, h:h + 1] = (m + jnp.log2(l)) * LN2
        o = _dot(prob_refs[h][:, :pw], vp_ref[:, kcols]) + _dot(prob_refs[h][:, pw:], vc_ref[:, kcols])
        o_ref[:, qcols] = (o * linv_refs[h][...]).astype(o_ref.dtype)


def _banded_attention(arr, *, n_steps, hq, grp, window, q_at, k_at, v_at, o_at, o_dims, tq, pw,
                      sinks=None, want_lse=False):
    bsz, _, seq, _ = arr.shape
    tq = min(tq, seq)
    pw = min(pw, tq)
    assert pw >= window - 1 or seq <= pw, (pw, window, seq)
    assert seq % tq == 0 and tq % pw == 0
    hk = hq // grp
    ratio = tq // pw
    wq, wk = hq * HEAD_DIM, hk * HEAD_DIM
    has_sink = sinks is not None

    def spec(width, rows, at_fn, row_fn):
        def index(b, s, i, *_):
            plane, col = at_fn(s)
            return (b, plane, row_fn(i), col)
        return pl.BlockSpec((None, None, rows, width), index)

    cur = lambda i: i
    prev = lambda i: jnp.maximum(i * ratio - 1, 0)
    in_specs = [spec(wq, tq, q_at, cur), spec(wk, tq, k_at, cur), spec(wk, pw, k_at, prev),
                spec(wk, tq, v_at, cur), spec(wk, pw, v_at, prev)]
    out_specs = [spec(wq, tq, o_at, cur)]
    out_shape = [jax.ShapeDtypeStruct((bsz, o_dims[0], seq, o_dims[1]), BF16)]
    if want_lse:
        out_specs.append(pl.BlockSpec((None, None, tq, hq), lambda b, s, i: (b, s, i, 0)))
        out_shape.append(jax.ShapeDtypeStruct((bsz, n_steps, seq, hq), F32))
    kern = functools.partial(_banded_kernel, hq=hq, grp=grp, window=window, tq=tq, pw=pw,
                             has_sink=has_sink, want_lse=want_lse)
    grid = (bsz, n_steps, seq // tq)
    params = _params("parallel", "parallel", "arbitrary")
    scratch = ([pltpu.VMEM((tq, pw + tq), F32)] * hq + [pltpu.VMEM((tq, pw + tq), BF16)] * hq
               + [pltpu.VMEM((tq, HEAD_DIM), F32)] * hq + [pltpu.VMEM((tq, pw + tq), F32)])
    if has_sink:
        grid_spec = pltpu.PrefetchScalarGridSpec(num_scalar_prefetch=1, grid=grid, in_specs=in_specs,
                                                 out_specs=out_specs, scratch_shapes=scratch)
        out = pl.pallas_call(kern, grid_spec=grid_spec, out_shape=out_shape, compiler_params=params,
                             name="banded_attn_sink")(sinks.astype(F32), arr, arr, arr, arr, arr)
    else:
        out = pl.pallas_call(kern, grid=grid, in_specs=in_specs, out_specs=out_specs, out_shape=out_shape,
                             scratch_shapes=scratch, compiler_params=params, name="banded_attn")(arr, arr, arr, arr, arr)
    return out if want_lse else out[0]


def _mix_out_proj_kernel(o0_ref, o1_ref, o2_ref, l0_ref, l1_ref, l2_ref, w_ref, res_ref, g_ref, b_ref,
                         of_ref, ob_ref, t1_ref, t2_ref, x_ref):
    tm = res_ref.shape[0]
    sub = min(PROJ_SUB_ROWS, tm)

    def mix(r):
        rows = slice(r * sub, (r + 1) * sub)
        for src, dst in ((o1_ref, t1_ref), (o2_ref, t2_ref)):
            dil = src.shape[0]
            n = sub // dil
            for c in range(dil):
                for h in range(DIL_HEADS):
                    dst[h, pl.ds(r * sub + c, n, stride=dil), :] = src[
                        c, r * n:(r + 1) * n, h * HEAD_DIM:(h + 1) * HEAD_DIM].astype(F32)
        l0, l1, l2 = l0_ref[rows, :], l1_ref[rows, :], l2_ref[rows, :]
        m = jnp.maximum(jnp.maximum(l0, l1), l2)
        e0, e1, e2 = jnp.exp(l0 - m), jnp.exp(l1 - m), jnp.exp(l2 - m)
        inv = 1.0 / (e0 + e1 + e2)
        w0, w1, w2 = e0 * inv, e1 * inv, e2 * inv
        for h in range(DIL_HEADS):
            cols = slice(h * HEAD_DIM, (h + 1) * HEAD_DIM)
            mixed = (w0[:, h:h + 1] * o0_ref[rows, cols].astype(F32)
                     + w1[:, h:h + 1] * t1_ref[h, rows, :]
                     + w2[:, h:h + 1] * t2_ref[h, rows, :])
            x_ref[rows, cols] = mixed.astype(x_ref.dtype)

    def finish(r, y):
        rows = slice(r * sub, (r + 1) * sub)
        out = _residual_layer_norm(res_ref[rows, :], y, g_ref[...], b_ref[...])
        of_ref[rows, :] = out
        ob_ref[rows, :] = out.astype(BF16)

    n_sub = tm // sub
    mix(0)
    pending = None
    for r in range(n_sub):
        y = _dot(x_ref[r * sub:(r + 1) * sub, :], w_ref[...])
        if r + 1 < n_sub:
            mix(r + 1)
        if pending is not None:
            finish(*pending)
        pending = (r, y)
    finish(*pending)


def _mix_out_proj_ln(outs, lses, w, res, g, b, *, seq, tm=512):
    bsz, _, _, d = outs[0].shape
    tm = min(tm, seq)
    tiles_per_batch = seq // tm

    def o_spec(dil):
        return pl.BlockSpec((None, dil, tm // dil, d), lambda i: (i // tiles_per_batch, 0, i % tiles_per_batch, 0))

    row = pl.BlockSpec((tm, d), lambda i: (i, 0))
    lrow = pl.BlockSpec((tm, DIL_HEADS), lambda i: (i, 0))
    vec = pl.BlockSpec((1, d), lambda i: (0, 0))
    dils = [o.shape[1] for o in outs]
    assert dils[0] == 1
    return pl.pallas_call(
        _mix_out_proj_kernel,
        grid=(bsz * tiles_per_batch,),
        in_specs=[row, o_spec(dils[1]), o_spec(dils[2]), lrow, lrow, lrow,
                  pl.BlockSpec(w.shape, lambda i: (0, 0), pipeline_mode=pl.Buffered(1)), row, vec, vec],
        out_specs=[row, row],
        out_shape=[jax.ShapeDtypeStruct((bsz * seq, d), F32), jax.ShapeDtypeStruct((bsz * seq, d), BF16)],
        scratch_shapes=[pltpu.VMEM((DIL_HEADS, tm, HEAD_DIM), F32)] * 2 + [pltpu.VMEM((tm, d), BF16)],
        compiler_params=_params("parallel"),
        name="mix_out_proj_ln",
    )(outs[0].reshape(bsz * seq, d), outs[1], outs[2], *lses, w, res, g.reshape(1, d), b.reshape(1, d))


def _compress_kernel(x_ref, pe_ref, w1_ref, w2_ref, o_ref, stage_ref, first_ref, second_ref):
    seq = x_ref.shape[0]
    n = seq // CMP_STRIDE
    stage_ref[...] = x_ref[...].astype(F32)
    for j in range(CMP_STRIDE):
        piece = stage_ref[pl.ds(j, n, stride=CMP_STRIDE), :]
        lo = _dot((piece + pe_ref[j:j + 1, :]).astype(BF16), w1_ref[j * HEAD_DIM:(j + 1) * HEAD_DIM, :])
        hi_j = CMP_STRIDE + j
        hi = _dot((piece + pe_ref[hi_j:hi_j + 1, :]).astype(BF16), w1_ref[hi_j * HEAD_DIM:(hi_j + 1) * HEAD_DIM, :])
        if j == 0:
            first_ref[...] = lo
            second_ref[0:n, :] = hi
        else:
            first_ref[...] += lo
            second_ref[0:n, :] += hi
    second_ref[n:n + 8, :] = jnp.zeros((8, HEAD_DIM), F32)
    h = first_ref[...] + second_ref[1:n + 1, :]
    o_ref[...] = _dot(jax.nn.gelu(h).astype(BF16), w2_ref[...]).astype(o_ref.dtype)


def _compress(proj4, blk0, pe, w1, w2):
    bsz, _, seq, _ = proj4.shape
    n = seq // CMP_STRIDE
    const = lambda shape: pl.BlockSpec(shape, lambda b, g: (0, 0))
    return pl.pallas_call(
        _compress_kernel,
        grid=(bsz, NSA_KV_HEADS),
        in_specs=[pl.BlockSpec((None, None, seq, HEAD_DIM), lambda b, g: (b, 0, 0, blk0 + g)),
                  const((CMP_BLOCK, HEAD_DIM)), const((CMP_BLOCK * HEAD_DIM, HEAD_DIM)), const((HEAD_DIM, HEAD_DIM))],
        out_specs=pl.BlockSpec((None, n, HEAD_DIM), lambda b, g: (b * NSA_KV_HEADS + g, 0, 0)),
        out_shape=jax.ShapeDtypeStruct((bsz * NSA_KV_HEADS, n, HEAD_DIM), BF16),
        scratch_shapes=[pltpu.VMEM((seq, HEAD_DIM), F32), pltpu.VMEM((n, HEAD_DIM), F32),
                        pltpu.VMEM((n + 8, HEAD_DIM), F32)],
        compiler_params=_params("parallel", "parallel"),
        name="nsa_compress",
    )(proj4, pe, w1.astype(BF16), w2.astype(BF16))


def _nsa_kernel(q_ref, kc_ref, vc_ref, c2s_ref, ks_ref, vs_ref, ow_ref, g_ref, o_ref,
                kaug_ref, qaug_ref, s0_ref, s1_ref, p0_ref, p1_ref, a0_ref, a1_ref, m_ref, l_ref, acc_ref, ocmp_ref,
                *, tq, kt, n_blk, n_sel, n_var):
    grp = NSA_GROUP
    n_cmp = kc_ref.shape[0]
    seq = ks_ref.shape[0]
    n_halves = qaug_ref.shape[0]
    chunks_per_half = HEAD_DIM * SEL_BLOCK // kt
    q0 = pl.program_id(2) * tq

    @pl.when(pl.program_id(2) == 0)
    def _():
        def fill(r, carry):
            r0 = pl.multiple_of(r * kt, kt)
            blk_of_row = (lax.broadcasted_iota(jnp.int32, (kt, HEAD_DIM), 0) + r0) // SEL_BLOCK
            lane = lax.broadcasted_iota(jnp.int32, (kt, HEAD_DIM), 1)
            kaug_ref[pl.ds(r0, kt), 0:HEAD_DIM] = ks_ref[pl.ds(r0, kt), :]
            kaug_ref[pl.ds(r0, kt), HEAD_DIM:2 * HEAD_DIM] = jnp.where(
                lane == blk_of_row % HEAD_DIM, NEG, 0.0).astype(kaug_ref.dtype)
            return carry
        lax.fori_loop(0, seq // kt, fill, 0)

    qs = jnp.concatenate([q_ref[:, h * HEAD_DIM:(h + 1) * HEAD_DIM] for h in range(grp)], axis=0)
    qpos = q0 + lax.broadcasted_iota(jnp.int32, (tq, 1), 0)

    def select_blocks(n_vis, n_rows):
        cmp_end = lax.broadcasted_iota(jnp.int32, (tq, n_vis), 1) * CMP_STRIDE + (CMP_BLOCK - 1)
        cmp_bias = jnp.where(cmp_end <= qpos, 0.0, NEG)
        s = _dot_nt(qs, kc_ref[0:n_vis, :]).reshape(grp, tq, n_vis) + cmp_bias[None]
        m = jnp.max(s, axis=-1, keepdims=True)
        m = jnp.where(m < 0.5 * NEG, 0.0, m)
        p = jnp.exp2(s - m)
        l = jnp.sum(p, axis=-1, keepdims=True)
        p = p * (1.0 / jnp.where(l > 0.0, l, 1.0))
        ocmp_ref[...] = _dot(p.reshape(grp * tq, n_vis).astype(BF16), vc_ref[0:n_vis, :])

        p_sum = jnp.sum(p, axis=0)
        p_hi = p_sum.astype(BF16)
        p_lo = (p_sum - p_hi.astype(F32)).astype(BF16)
        c2s = c2s_ref[0:n_rows, 0:n_vis]
        imp = _dot_nt(c2s, p_hi) + _dot_nt(c2s, p_lo)

        blk = lax.broadcasted_iota(jnp.int32, (n_rows, tq), 0)
        qpos_lane = q0 + lax.broadcasted_iota(jnp.int32, (1, tq), 1)
        cur = qpos_lane // SEL_BLOCK
        forced = (blk == 0) | (blk == cur) | (blk == cur - 1)
        score = jnp.where(forced, jnp.inf, jnp.where(blk * SEL_BLOCK <= qpos_lane, imp, -1.0))
        blk_f = blk.astype(F32)
        not_sel = jnp.ones((n_rows, tq), F32)
        for _ in range(n_sel):
            best = jnp.max(score, axis=0, keepdims=True)
            first = jnp.min(jnp.where(score == best, blk_f, float(n_rows)), axis=0, keepdims=True)
            hit = blk_f == first
            not_sel = jnp.where(hit, 0.0, not_sel)
            score = jnp.where(hit, -2.0, score)
        if n_rows < n_pad:
            not_sel = jnp.concatenate([not_sel, jnp.ones((n_pad - n_rows, tq), F32)], axis=0)
        not_sel = not_sel.T
        for half in range(n_halves):
            flags = not_sel[:, half * HEAD_DIM:(half + 1) * HEAD_DIM].astype(qaug_ref.dtype)
            qaug_ref[half, :, 0:HEAD_DIM] = qs
            qaug_ref[half, :, HEAD_DIM:2 * HEAD_DIM] = jnp.concatenate([flags] * grp, axis=0)

    n_pad = c2s_ref.shape[0]
    for v in range(n_var):
        @pl.when(q0 // (seq // n_var) == v)
        def _(v=v):
            select_blocks((v + 1) * (n_cmp // n_var), (v + 1) * (n_pad // n_var))

    rows_blk = SOFTMAX_ROWS
    bufs = ((s0_ref, p0_ref, a0_ref), (s1_ref, p1_ref, a1_ref))
    m_ref[...] = jnp.full(m_ref.shape, NEG, F32)
    l_ref[...] = jnp.zeros(l_ref.shape, F32)
    acc_ref[...] = jnp.zeros(acc_ref.shape, F32)

    def scores(j, slot):
        k0 = pl.multiple_of(j * kt, kt)
        bufs[slot][0][...] = _dot_nt(qaug_ref[j // chunks_per_half], kaug_ref[pl.ds(k0, kt), :])

    def softmax(j, slot, causal):
        s_buf, p_buf, a_buf = bufs[slot]
        k0 = j * kt
        for rb in range(grp * tq // rows_blk):
            rows = slice(rb * rows_blk, (rb + 1) * rows_blk)
            sb = s_buf[rows, :]
            if causal:
                row_pos = q0 + (rb * rows_blk) % tq + lax.broadcasted_iota(jnp.int32, (rows_blk, 1), 0)
                key_pos = k0 + lax.broadcasted_iota(jnp.int32, (rows_blk, kt), 1)
                sb = jnp.where(key_pos <= row_pos, sb, NEG)
            m_prev = m_ref[rows, :]
            m_new = jnp.maximum(m_prev, jnp.max(sb, axis=-1, keepdims=True))
            alpha = jnp.exp2(m_prev - m_new)
            p = jnp.exp2(sb - jnp.concatenate([m_new] * (kt // HEAD_DIM), axis=1))
            l_ref[rows, :] = alpha * l_ref[rows, :] + jnp.sum(p, axis=-1, keepdims=True)
            m_ref[rows, :] = m_new
            a_buf[rows, :] = alpha
            p_buf[rows, :] = p.astype(p_buf.dtype)

    def weighted_values(j, slot):
        _, p_buf, a_buf = bufs[slot]
        k0 = pl.multiple_of(j * kt, kt)
        acc_ref[...] = a_buf[...] * acc_ref[...] + _dot(p_buf[...], vs_ref[pl.ds(k0, kt), :])

    n_interior = q0 // kt
    scores(0, 0)

    def interior_run(j0, count):
        for c in range(count):
            scores(j0 + c + 1, (c + 1) % 2)
            softmax(j0 + c, c % 2, causal=False)
            weighted_values(j0 + c, c % 2)

    unroll = NSA_CHUNK_UNROLL
    n_runs = n_interior // unroll

    def interior_body(r, carry):
        interior_run(r * unroll, unroll)
        return carry

    lax.fori_loop(0, n_runs, interior_body, 0)
    for count in (c for c in (4, 2) if c < unroll):
        @pl.when((n_interior % (2 * count)) // count == 1)
        def _(count=count):
            interior_run(n_interior // (2 * count) * (2 * count), count)

    @pl.when(n_interior % 2 == 0)
    def _():
        softmax(n_interior, 0, causal=True)
        weighted_values(n_interior, 0)

    @pl.when(n_interior % 2 == 1)
    def _():
        scores(n_interior, 1)
        softmax(n_interior - 1, 0, causal=False)
        weighted_values(n_interior - 1, 0)
        softmax(n_interior, 1, causal=True)
        weighted_values(n_interior, 1)

    o_slc = acc_ref[...] * (1.0 / l_ref[...])

    gates = g_ref[...]
    for h in range(grp):
        rows = slice(h * tq, (h + 1) * tq)
        cols = slice(h * HEAD_DIM, (h + 1) * HEAD_DIM)
        out = (gates[:, 3 * h:3 * h + 1] * ocmp_ref[rows, :]
               + gates[:, 3 * h + 1:3 * h + 2] * o_slc[rows]
               + gates[:, 3 * h + 2:3 * h + 3] * ow_ref[:, cols].astype(F32))
        o_ref[:, cols] = out.astype(o_ref.dtype)


def _sel_from_cmp(n_cmp_pad, n_blk_pad):
    pos = np.arange(n_cmp_pad)[:, None] * CMP_STRIDE + np.arange(CMP_BLOCK)[None, :]
    owner = pos // SEL_BLOCK
    frac = (owner[:, :, None] == np.arange(n_blk_pad)[None, None, :]).sum(axis=1) / CMP_BLOCK
    return jnp.asarray(frac.T, dtype=BF16)


def _nsa_attention(proj, kc, vc, o_win, gates, *, q_idx, ks_idx, vs_idx, tq=256, kt=512):
    bsz, seq, _ = proj.shape
    tq = min(tq, seq)
    kt = min(kt, seq)
    assert kt % tq == 0 and (HEAD_DIM * SEL_BLOCK) % kt == 0 and seq % kt == 0
    n_cmp = kc.shape[1]
    n_blk = seq // SEL_BLOCK
    n_sel = min(N_SELECT, n_blk)
    n_halves = -(-n_blk // HEAD_DIM)
    wq = NSA_GROUP * HEAD_DIM
    rows = NSA_GROUP * tq
    c2s = _sel_from_cmp(n_cmp, n_halves * HEAD_DIM)
    n_var = max(1, min(NSA_PREFIX_VARIANTS, n_cmp // (2 * HEAD_DIM)))
    assert seq % n_var == 0 and (seq // n_var) % tq == 0 and (n_halves * HEAD_DIM) % (8 * n_var) == 0
    kern = functools.partial(_nsa_kernel, tq=tq, kt=kt, n_blk=n_blk, n_sel=n_sel, n_var=n_var)
    full = lambda idx_fn: pl.BlockSpec((None, seq, HEAD_DIM), lambda b, g, i: (b, 0, idx_fn(g)))
    cmp_spec = pl.BlockSpec((None, n_cmp, HEAD_DIM), lambda b, g, i: (b * NSA_KV_HEADS + g, 0, 0))
    return pl.pallas_call(
        kern,
        grid=(bsz, NSA_KV_HEADS, seq // tq),
        in_specs=[
            pl.BlockSpec((None, tq, wq), lambda b, g, i: (b, i, q_idx(g))),
            cmp_spec, cmp_spec,
            pl.BlockSpec(c2s.shape, lambda b, g, i: (0, 0)),
            full(ks_idx), full(vs_idx),
            pl.BlockSpec((None, tq, wq), lambda b, g, i: (b, i, g)),
            pl.BlockSpec((None, tq, HEAD_DIM), lambda b, g, i: (b, i, g)),
        ],
        out_specs=pl.BlockSpec((None, tq, wq), lambda b, g, i: (b, i, g)),
        out_shape=jax.ShapeDtypeStruct((bsz, seq, NSA_KV_HEADS * wq), BF16),
        scratch_shapes=[pltpu.VMEM((seq, 2 * HEAD_DIM), BF16),
                        pltpu.VMEM((n_halves, rows, 2 * HEAD_DIM), BF16),
                        pltpu.VMEM((rows, kt), F32), pltpu.VMEM((rows, kt), F32),
                        pltpu.VMEM((rows, kt), BF16), pltpu.VMEM((rows, kt), BF16)]
                       + [pltpu.VMEM((rows, HEAD_DIM), F32)] * 6,
        compiler_params=_params("parallel", "parallel", "arbitrary"),
        name="nsa_core",
    )(proj, kc, vc, c2s, proj, proj, o_win, gates)


_BLK_AQ, _BLK_BQ = 0, NSA_HEADS
_BLK_AK = NSA_HEADS + SWA_HEADS
_BLK_BK = _BLK_AK + 3 * NSA_KV_HEADS
_BLK_AV = _BLK_BK + SWA_KV_HEADS
_BLK_BV = _BLK_AV + 3 * NSA_KV_HEADS


def _even_weights(w_in):
    aq, ak, av, ag, bq, bk, bv = jnp.split(w_in, np.cumsum(EVEN_WIDTHS)[:-1].tolist(), axis=-1)
    w_main = jnp.concatenate([aq, bq, ak, bk, av, bv], axis=-1).astype(BF16)
    ag = ag.reshape(D_MODEL, NSA_KV_HEADS, 3 * NSA_GROUP)
    w_gate = jnp.pad(ag, ((0, 0), (0, 0), (0, HEAD_DIM - 3 * NSA_GROUP))).reshape(D_MODEL, NSA_KV_HEADS * HEAD_DIM)
    return w_main, w_gate.astype(BF16)


def _tile_modes(blocks_per_tile, block_modes):
    modes = np.asarray(block_modes, np.int32).reshape(-1, blocks_per_tile)
    assert (modes == modes[:, :1]).all(), "a projection tile must not straddle q/k/v column groups"
    return jnp.asarray(modes[:, 0])


def _even_mixer(xf, xb, cos, sin, bsz, seq, w_in, w_o, pe_k, pe_v, ck_w1, ck_w2, cv_w1, cv_w2, sinks, g, b):
    w_main, w_gate = _even_weights(w_in)
    tn = 1024
    n_q, n_k = NSA_HEADS + SWA_HEADS, 3 * NSA_KV_HEADS + SWA_KV_HEADS
    modes = _tile_modes(tn // HEAD_DIM, [MODE_ROPE_SCALED] * n_q + [MODE_ROPE] * n_k + [MODE_PLAIN] * n_k)
    proj4 = _project(xb, w_main, modes, cos, sin, seq=seq, tn=tn)
    proj = proj4.reshape(bsz, seq, EVEN_PROJ)
    gates = _gate_project(xb, w_gate).reshape(bsz, seq, NSA_KV_HEADS * HEAD_DIM)

    kc = _compress(proj4, _BLK_AK, pe_k, ck_w1, ck_w2)
    vc = _compress(proj4, _BLK_AV, pe_v, cv_w1, cv_w2)
    o_win = _banded_attention(
        proj4, n_steps=NSA_KV_HEADS, hq=NSA_GROUP, grp=NSA_GROUP, window=NSA_WINDOW,
        q_at=lambda s: (0, s), k_at=lambda s: (0, _BLK_AK + 2 * NSA_KV_HEADS + s),
        v_at=lambda s: (0, _BLK_AV + 2 * NSA_KV_HEADS + s), o_at=lambda s: (0, s), o_dims=(1, NSA_Q),
        tq=512, pw=512).reshape(bsz, seq, NSA_Q)
    o_a = _nsa_attention(proj, kc, vc, o_win, gates, q_idx=lambda s: s,
                         ks_idx=lambda s: _BLK_AK + NSA_KV_HEADS + s, vs_idx=lambda s: _BLK_AV + NSA_KV_HEADS + s)
    swa_grp = SWA_HEADS // SWA_KV_HEADS
    o_b = _banded_attention(
        proj4, n_steps=SWA_KV_HEADS, hq=swa_grp, grp=swa_grp, window=SWA_WINDOW,
        q_at=lambda s: (0, NSA_HEADS // swa_grp + s), k_at=lambda s: (0, _BLK_BK + s),
        v_at=lambda s: (0, _BLK_BV + s), o_at=lambda s: (0, s), o_dims=(1, SWA_Q),
        tq=256, pw=128, sinks=sinks)
    m = bsz * seq
    w_o = w_o.astype(BF16)
    return _out_proj_ln([o_a.reshape(m, NSA_Q), o_b.reshape(m, SWA_Q)], [w_o[:NSA_Q], w_o[NSA_Q:]], xf, g, b)


def _odd_mixer(xf, xb, cos, sin, bsz, seq, w_in, w_o, g, b):
    tn = 1024
    modes = _tile_modes(tn // HEAD_DIM, [MODE_ROPE_SCALED] * DIL_HEADS + [MODE_ROPE] * DIL_HEADS + [MODE_PLAIN] * DIL_HEADS)
    m = bsz * seq
    width = DIL_HEADS * HEAD_DIM
    outs, lses = [], []
    for grp_i, (window, dil) in enumerate(DIL_PATTERNS):
        w_grp = w_in[:, grp_i * DIL_QKV:(grp_i + 1) * DIL_QKV].astype(BF16)
        qkv = _project(xb, w_grp, modes, cos, sin, seq=seq, dil=dil, tn=tn)
        o, lse = _banded_attention(
            qkv, n_steps=dil, hq=DIL_HEADS, grp=1, window=window // dil + 1,
            q_at=lambda s: (s, 0), k_at=lambda s: (s, 1), v_at=lambda s: (s, 2), o_at=lambda s: (s, 0),
            o_dims=(dil, width), tq=256, pw=128, want_lse=True)
        outs.append(o)
        lses.append(lse.transpose(0, 2, 1, 3).reshape(m, DIL_HEADS))
    return _mix_out_proj_ln(outs, lses, w_o.astype(BF16), xf, g, b, seq=seq)


def kernel(x, positions, e_w_in, e_w_o, nsa_pe_k, nsa_pe_v, nsa_ck_w1, nsa_ck_w2, nsa_cv_w1, nsa_cv_w2, swa_sinks, o_w_in, o_w_o, ln1_g, ln1_b, mlp_w1, mlp_w2, ln2_g, ln2_b):
    bsz, seq, d = x.shape
    m = bsz * seq
    cos, sin = _rope_tables(positions)
    xf = x.reshape(m, d)
    xb = xf.astype(BF16)
    for layer in range(DEPTH):
        i = layer // 2
        if layer % 2 == 0:
            xf, xb = _even_mixer(xf, xb, cos, sin, bsz, seq, e_w_in[i], e_w_o[i], nsa_pe_k[i], nsa_pe_v[i],
                                 nsa_ck_w1[i], nsa_ck_w2[i], nsa_cv_w1[i], nsa_cv_w2[i], swa_sinks[i],
                                 ln1_g[layer], ln1_b[layer])
        else:
            xf, xb = _odd_mixer(xf, xb, cos, sin, bsz, seq, o_w_in[i], o_w_o[i], ln1_g[layer], ln1_b[layer])
        xf, xb = _mlp_ln(xf, xb, mlp_w1[layer].astype(BF16), mlp_w2[layer].astype(BF16), ln2_g[layer], ln2_b[layer])
    return xf.reshape(bsz, seq, d)
```

```python
import functools
import math

import numpy as np
import jax
import jax.numpy as jnp
from jax import lax
from jax.experimental import pallas as pl
from jax.experimental.pallas import tpu as pltpu

D_MODEL = 2048
DEPTH = 4
HEAD_DIM = 128
ROPE_THETA = 10000.0
LN_EPS = 1e-5

NSA_HEADS = D_MODEL // (2 * HEAD_DIM)
NSA_KV_HEADS = 2
NSA_GROUP = NSA_HEADS // NSA_KV_HEADS
CMP_BLOCK = 32
CMP_STRIDE = 16
SEL_BLOCK = 64
N_SELECT = 16
NSA_WINDOW = 512

SWA_HEADS = D_MODEL // (2 * HEAD_DIM)
SWA_KV_HEADS = 2
SWA_WINDOW = 128

DIL_HEADS = D_MODEL // HEAD_DIM
DIL_PATTERNS = ((128, 1), (512, 4), (2048, 16))

D_FF = 4 * D_MODEL
DN_ALPHA = (2 * DEPTH) ** 0.25

NSA_Q = NSA_HEADS * HEAD_DIM
NSA_KV = NSA_KV_HEADS * HEAD_DIM
SWA_Q = SWA_HEADS * HEAD_DIM
SWA_KV = SWA_KV_HEADS * HEAD_DIM
EVEN_WIDTHS = (NSA_Q, 3 * NSA_KV, 3 * NSA_KV, 3 * NSA_HEADS, SWA_Q, SWA_KV, SWA_KV)
EVEN_PROJ = NSA_Q + SWA_Q + 3 * NSA_KV + SWA_KV + 3 * NSA_KV + SWA_KV
DIL_QKV = 3 * DIL_HEADS * HEAD_DIM

LOG2E = math.log2(math.e)
LN2 = math.log(2.0)
QK_SCALE = HEAD_DIM ** -0.5 * LOG2E
NEG = -1e30
SOFTMAX_ROWS = 32
BANDED_SOFTMAX_ROWS = 32
NSA_CHUNK_UNROLL = 4
NSA_PREFIX_VARIANTS = 4
PROJ_SUB_ROWS = 256
VMEM_LIMIT_BYTES = 56 * 1024 * 1024

MODE_PLAIN, MODE_ROPE, MODE_ROPE_SCALED = 0, 1, 2

F32 = jnp.float32
BF16 = jnp.bfloat16


def _params(*semantics):
    return pltpu.CompilerParams(dimension_semantics=semantics, vmem_limit_bytes=VMEM_LIMIT_BYTES)


def _dot(a, b):
    return jnp.dot(a, b, preferred_element_type=F32)


def _dot_nt(a, b):
    return lax.dot_general(a, b, (((1,), (1,)), ((), ())), preferred_element_type=F32)


def _rope_table_kernel(pos_ref, inv_ref, sign_ref, cos_ref, sin_ref):
    ang = pos_ref[...].astype(F32) * inv_ref[...]
    cos_ref[...] = jnp.cos(ang)
    sin_ref[...] = jnp.sin(ang) * sign_ref[...]


def _rope_tables(positions):
    m = positions.size
    tm = min(m, 2048)
    inv = ROPE_THETA ** (-jnp.arange(0, HEAD_DIM, 2, dtype=F32) / HEAD_DIM)
    inv_full = jnp.concatenate([inv, inv]).reshape(1, HEAD_DIM)
    half = HEAD_DIM // 2
    sign = jnp.concatenate([-jnp.ones((half,), F32), jnp.ones((half,), F32)]).reshape(1, HEAD_DIM)
    row = pl.BlockSpec((tm, HEAD_DIM), lambda i: (i, 0))
    const = pl.BlockSpec((1, HEAD_DIM), lambda i: (0, 0))
    return pl.pallas_call(
        _rope_table_kernel,
        grid=(m // tm,),
        in_specs=[pl.BlockSpec((tm, 1), lambda i: (i, 0)), const, const],
        out_specs=[row, row],
        out_shape=[jax.ShapeDtypeStruct((m, HEAD_DIM), F32)] * 2,
        compiler_params=_params("parallel"),
        name="rope_tables",
    )(positions.reshape(m, 1), inv_full, sign)


def _proj_kernel(modes_ref, x_ref, w_ref, cos_ref, sin_ref, o_ref, *scratch, tn, dil):
    mode = modes_ref[pl.program_id(1)]
    tm = x_ref.shape[0]
    sub = min(PROJ_SUB_ROWS, tm)
    blocks = [slice(blk * HEAD_DIM, (blk + 1) * HEAD_DIM) for blk in range(tn // HEAD_DIM)]
    is_rope = mode != MODE_PLAIN
    scale = jnp.where(mode == MODE_ROPE_SCALED, QK_SCALE, 1.0).astype(F32)

    def epilogue(r, acc):
        rows = slice(r * sub, (r + 1) * sub)
        cos = jnp.where(is_rope, cos_ref[rows, :] * scale, 1.0)
        sin = jnp.where(is_rope, sin_ref[rows, :] * scale, 0.0)
        for blk, cols in enumerate(blocks):
            t = acc[:, cols]
            val = t * cos + pltpu.roll(t, HEAD_DIM // 2, 1) * sin
            if dil == 1:
                o_ref[rows, cols] = val.astype(o_ref.dtype)
            else:
                scratch[0][blk, rows, :] = val
        if dil > 1:
            n = sub // dil
            for c in range(dil):
                for blk, cols in enumerate(blocks):
                    o_ref[c, r * n:(r + 1) * n, cols] = scratch[0][
                        blk, pl.ds(r * sub + c, n, stride=dil), :].astype(o_ref.dtype)

    pending = None
    for r in range(tm // sub):
        acc = _dot(x_ref[r * sub:(r + 1) * sub, :], w_ref[...])
        if pending is not None:
            epilogue(*pending)
        pending = (r, acc)
    epilogue(*pending)


def _project(x, w, modes, cos, sin, *, seq, dil=1, tm=1024, tn=1024, col0=0, n=None):
    m, k = x.shape
    n = w.shape[1] if n is None else n
    assert col0 % tn == 0 and n % tn == 0
    tile0 = col0 // tn
    tm = min(tm, seq)
    tiles_per_batch = seq // tm
    bsz = m // seq
    if dil == 1:
        out_spec = pl.BlockSpec((tm, tn), lambda i, j, modes: (i, j))
        out_shape = jax.ShapeDtypeStruct((m, n), BF16)
        scratch = []
    else:
        out_spec = pl.BlockSpec((None, dil, tm // dil, tn),
                                lambda i, j, modes: (i // tiles_per_batch, 0, i % tiles_per_batch, j))
        out_shape = jax.ShapeDtypeStruct((bsz, dil, seq // dil, n), BF16)
        scratch = [pltpu.VMEM((tn // HEAD_DIM, tm, HEAD_DIM), F32)]
    grid_spec = pltpu.PrefetchScalarGridSpec(
        num_scalar_prefetch=1,
        grid=(m // tm, n // tn),
        in_specs=[
            pl.BlockSpec((tm, k), lambda i, j, modes: (i, 0)),
            pl.BlockSpec((k, tn), lambda i, j, modes: (0, tile0 + j)),
            pl.BlockSpec((tm, HEAD_DIM), lambda i, j, modes: (i, 0)),
            pl.BlockSpec((tm, HEAD_DIM), lambda i, j, modes: (i, 0)),
        ],
        out_specs=out_spec,
        scratch_shapes=scratch,
    )
    out = pl.pallas_call(
        functools.partial(_proj_kernel, tn=tn, dil=dil),
        grid_spec=grid_spec,
        out_shape=out_shape,
        compiler_params=_params("parallel", "arbitrary"),
        name="in_proj" if dil == 1 else "in_proj_dil%d" % dil,
    )(modes, x, w, cos, sin)
    return out.reshape(bsz, 1, seq, n) if dil == 1 else out


def _gate_kernel(x_ref, w_ref, o_ref):
    o_ref[...] = jax.nn.sigmoid(_dot(x_ref[...], w_ref[...]))


def _gate_project(x, w, *, tm=1024):
    m, k = x.shape
    n = w.shape[1]
    tm = min(tm, m)
    return pl.pallas_call(
        _gate_kernel,
        grid=(m // tm,),
        in_specs=[pl.BlockSpec((tm, k), lambda i: (i, 0)), pl.BlockSpec((k, n), lambda i: (0, 0))],
        out_specs=pl.BlockSpec((tm, n), lambda i: (i, 0)),
        out_shape=jax.ShapeDtypeStruct((m, n), F32),
        compiler_params=_params("parallel"),
        name="gate_proj",
    )(x, w)


def _residual_layer_norm(res, y, g, b):
    z = DN_ALPHA * res + y
    mu = jnp.mean(z, axis=-1, keepdims=True)
    zc = z - mu
    var = jnp.mean(zc * zc, axis=-1, keepdims=True)
    return zc * lax.rsqrt(var + LN_EPS) * g + b


def _out_proj_kernel(*refs, n_in):
    x_refs, w_refs = refs[:n_in], refs[n_in:2 * n_in]
    res_ref, g_ref, b_ref, of_ref, ob_ref = refs[2 * n_in:]
    tm = res_ref.shape[0]
    sub = min(PROJ_SUB_ROWS, tm)

    def finish(r, y):
        rows = slice(r * sub, (r + 1) * sub)
        out = _residual_layer_norm(res_ref[rows, :], y, g_ref[...], b_ref[...])
        of_ref[rows, :] = out
        ob_ref[rows, :] = out.astype(BF16)

    pending = None
    for r in range(tm // sub):
        rows = slice(r * sub, (r + 1) * sub)
        y = _dot(x_refs[0][rows, :], w_refs[0][...])
        for x_ref, w_ref in zip(x_refs[1:], w_refs[1:]):
            y += _dot(x_ref[rows, :], w_ref[...])
        if pending is not None:
            finish(*pending)
        pending = (r, y)
    finish(*pending)


def _out_proj_ln(xs, ws, res, g, b, *, tm=512):
    m, d = res.shape
    tm = min(tm, m)
    n_in = len(xs)
    in_specs = [pl.BlockSpec((tm, x.shape[1]), lambda i: (i, 0)) for x in xs]
    in_specs += [pl.BlockSpec(w.shape, lambda i: (0, 0)) for w in ws]
    row = pl.BlockSpec((tm, d), lambda i: (i, 0))
    vec = pl.BlockSpec((1, d), lambda i: (0, 0))
    in_specs += [row, vec, vec]
    return pl.pallas_call(
        functools.partial(_out_proj_kernel, n_in=n_in),
        grid=(m // tm,),
        in_specs=in_specs,
        out_specs=[row, row],
        out_shape=[jax.ShapeDtypeStruct((m, d), F32), jax.ShapeDtypeStruct((m, d), BF16)],
        compiler_params=_params("parallel"),
        name="out_proj_ln",
    )(*xs, *ws, res, g.reshape(1, d), b.reshape(1, d))


def _mlp_kernel(xb_ref, w1_ref, w2_ref, xf_ref, g_ref, b_ref, of_ref, ob_ref, acc_ref):
    f = pl.program_id(1)
    tm = xb_ref.shape[0]
    sub = min(PROJ_SUB_ROWS, tm)
    @pl.when(f == 0)
    def _():
        acc_ref[...] = jnp.zeros(acc_ref.shape, F32)

    def chunk(finish):
        hidden = [None] * (tm // sub)
        for r in range(tm // sub + 1):
            if r < tm // sub:
                hidden[r] = _dot(xb_ref[r * sub:(r + 1) * sub, :], w1_ref[...])
            if r > 0:
                rows = slice((r - 1) * sub, r * sub)
                act = jnp.square(jnp.maximum(hidden[r - 1], 0.0)).astype(BF16)
                total = acc_ref[rows, :] + _dot(act, w2_ref[...])
                if finish:
                    out = _residual_layer_norm(xf_ref[rows, :], total, g_ref[...], b_ref[...])
                    of_ref[rows, :] = out
                    ob_ref[rows, :] = out.astype(BF16)
                else:
                    acc_ref[rows, :] = total

    last = pl.num_programs(1) - 1
    pl.when(f != last)(functools.partial(chunk, False))
    pl.when(f == last)(functools.partial(chunk, True))


def _mlp_ln(xf, xb, w1, w2, g, b, *, tm=512, tf=1024):
    m, d = xf.shape
    ff = w1.shape[1]
    tm = min(tm, m)
    row = pl.BlockSpec((tm, d), lambda i, f: (i, 0))
    vec = pl.BlockSpec((1, d), lambda i, f: (0, 0))
    return pl.pallas_call(
        _mlp_kernel,
        grid=(m // tm, ff // tf),
        in_specs=[row, pl.BlockSpec((d, tf), lambda i, f: (0, f)), pl.BlockSpec((tf, d), lambda i, f: (f, 0)),
                  row, vec, vec],
        out_specs=[row, row],
        out_shape=[jax.ShapeDtypeStruct((m, d), F32), jax.ShapeDtypeStruct((m, d), BF16)],
        scratch_shapes=[pltpu.VMEM((tm, d), F32)],
        compiler_params=_params("parallel", "arbitrary"),
        name="mlp_ln",
    )(xb, w1, w2, xf, g.reshape(1, d), b.reshape(1, d))


def _banded_kernel(*refs, hq, grp, window, tq, pw, has_sink, want_lse):
    refs = list(refs)
    sink_ref = refs.pop(0) if has_sink else None
    q_ref, kc_ref, kp_ref, vc_ref, vp_ref, o_ref = refs[:6]
    lse_ref = refs[6] if want_lse else None
    bias_ref = refs[-1]
    score_refs = refs[-1 - 3 * hq:-1 - 2 * hq]
    prob_refs = refs[-1 - 2 * hq:-1 - hq]
    linv_refs = refs[-1 - hq:-1]
    step, i = pl.program_id(1), pl.program_id(2)
    span = pw + tq
    rows_blk = min(BANDED_SOFTMAX_ROWS, tq)

    row = lax.broadcasted_iota(jnp.int32, (tq, span), 0)
    rel = lax.broadcasted_iota(jnp.int32, (tq, span), 1) - pw - row
    mask = (rel <= 0) & (rel > -window) & ((rel + row >= 0) | (i > 0))
    bias_ref[...] = jnp.where(mask, 0.0, NEG)

    @pl.when(i == 0)
    def _():
        for h in range(hq):
            prob_refs[h][...] = jnp.zeros(prob_refs[h].shape, prob_refs[h].dtype)

    for h in range(hq):
        kv = h // grp
        q = q_ref[:, h * HEAD_DIM:(h + 1) * HEAD_DIM]
        kcols = slice(kv * HEAD_DIM, (kv + 1) * HEAD_DIM)
        score_refs[h][:, :pw] = _dot_nt(q, kp_ref[:, kcols])
        score_refs[h][:, pw:] = _dot_nt(q, kc_ref[:, kcols])

    for h in range(hq):
        kv = h // grp
        qcols = slice(h * HEAD_DIM, (h + 1) * HEAD_DIM)
        kcols = slice(kv * HEAD_DIM, (kv + 1) * HEAD_DIM)
        if has_sink:
            sink = sink_ref[step * hq + h] * LOG2E
        for rb in range(tq // rows_blk):
            r0 = rb * rows_blk
            rows = slice(r0, r0 + rows_blk)
            cols = slice(max(0, r0 + pw - (window - 1)) // HEAD_DIM * HEAD_DIM,
                         -(-(r0 + rows_blk + pw) // HEAD_DIM) * HEAD_DIM)
            s = score_refs[h][rows, cols] + bias_ref[rows, cols]
            m = jnp.max(s, axis=-1, keepdims=True)
            if has_sink:
                m = jnp.maximum(m, sink)
            p = jnp.exp2(s - m)
            l = jnp.sum(p, axis=-1, keepdims=True)
            if has_sink:
                l = l + jnp.exp2(sink - m)
            prob_refs[h][rows, cols] = p.astype(BF16)
            linv_refs[h][rows, :] = jnp.broadcast_to(1.0 / l, (rows_blk, HEAD_DIM))
            if want_lse:
                lse_ref[rows, h:h + 1] = (m + jnp.log2(l)) * LN2
        o = _dot(prob_refs[h][:, :pw], vp_ref[:, kcols]) + _dot(prob_refs[h][:, pw:], vc_ref[:, kcols])
        o_ref[:, qcols] = (o * linv_refs[h][...]).astype(o_ref.dtype)


def _banded_attention(arr, *, n_steps, hq, grp, window, q_at, k_at, v_at, o_at, o_dims, tq, pw,
                      sinks=None, want_lse=False):
    bsz, _, seq, _ = arr.shape
    tq = min(tq, seq)
    pw = min(pw, tq)
    assert pw >= window - 1 or seq <= pw, (pw, window, seq)
    assert seq % tq == 0 and tq % pw == 0
    hk = hq // grp
    ratio = tq // pw
    wq, wk = hq * HEAD_DIM, hk * HEAD_DIM
    has_sink = sinks is not None

    def spec(width, rows, at_fn, row_fn):
        def index(b, s, i, *_):
            plane, col = at_fn(s)
            return (b, plane, row_fn(i), col)
        return pl.BlockSpec((None, None, rows, width), index)

    cur = lambda i: i
    prev = lambda i: jnp.maximum(i * ratio - 1, 0)
    in_specs = [spec(wq, tq, q_at, cur), spec(wk, tq, k_at, cur), spec(wk, pw, k_at, prev),
                spec(wk, tq, v_at, cur), spec(wk, pw, v_at, prev)]
    out_specs = [spec(wq, tq, o_at, cur)]
    out_shape = [jax.ShapeDtypeStruct((bsz, o_dims[0], seq, o_dims[1]), BF16)]
    if want_lse:
        out_specs.append(pl.BlockSpec((None, None, tq, hq), lambda b, s, i: (b, s, i, 0)))
        out_shape.append(jax.ShapeDtypeStruct((bsz, n_steps, seq, hq), F32))
    kern = functools.partial(_banded_kernel, hq=hq, grp=grp, window=window, tq=tq, pw=pw,
                             has_sink=has_sink, want_lse=want_lse)
    grid = (bsz, n_steps, seq // tq)
    params = _params("parallel", "parallel", "arbitrary")
    scratch = ([pltpu.VMEM((tq, pw + tq), F32)] * hq + [pltpu.VMEM((tq, pw + tq), BF16)] * hq
               + [pltpu.VMEM((tq, HEAD_DIM), F32)] * hq + [pltpu.VMEM((tq, pw + tq), F32)])
    if has_sink:
        grid_spec = pltpu.PrefetchScalarGridSpec(num_scalar_prefetch=1, grid=grid, in_specs=in_specs,
                                                 out_specs=out_specs, scratch_shapes=scratch)
        out = pl.pallas_call(kern, grid_spec=grid_spec, out_shape=out_shape, compiler_params=params,
                             name="banded_attn_sink")(sinks.astype(F32), arr, arr, arr, arr, arr)
    else:
        out = pl.pallas_call(kern, grid=grid, in_specs=in_specs, out_specs=out_specs, out_shape=out_shape,
                             scratch_shapes=scratch, compiler_params=params, name="banded_attn")(arr, arr, arr, arr, arr)
    return out if want_lse else out[0]


def _mix_out_proj_kernel(o0_ref, o1_ref, o2_ref, l0_ref, l1_ref, l2_ref, w_ref, res_ref, g_ref, b_ref,
                         of_ref, ob_ref, t1_ref, t2_ref, x_ref):
    tm = res_ref.shape[0]
    sub = min(PROJ_SUB_ROWS, tm)

    def mix(r):
        rows = slice(r * sub, (r + 1) * sub)
        for src, dst in ((o1_ref, t1_ref), (o2_ref, t2_ref)):
            dil = src.shape[0]
            n = sub // dil
            for c in range(dil):
                for h in range(DIL_HEADS):
                    dst[h, pl.ds(r * sub + c, n, stride=dil), :] = src[
                        c, r * n:(r + 1) * n, h * HEAD_DIM:(h + 1) * HEAD_DIM].astype(F32)
        l0, l1, l2 = l0_ref[rows, :], l1_ref[rows, :], l2_ref[rows, :]
        m = jnp.maximum(jnp.maximum(l0, l1), l2)
        e0, e1, e2 = jnp.exp(l0 - m), jnp.exp(l1 - m), jnp.exp(l2 - m)
        inv = 1.0 / (e0 + e1 + e2)
        w0, w1, w2 = e0 * inv, e1 * inv, e2 * inv
        for h in range(DIL_HEADS):
            cols = slice(h * HEAD_DIM, (h + 1) * HEAD_DIM)
            mixed = (w0[:, h:h + 1] * o0_ref[rows, cols].astype(F32)
                     + w1[:, h:h + 1] * t1_ref[h, rows, :]
                     + w2[:, h:h + 1] * t2_ref[h, rows, :])
            x_ref[rows, cols] = mixed.astype(x_ref.dtype)

    def finish(r, y):
        rows = slice(r * sub, (r + 1) * sub)
        out = _residual_layer_norm(res_ref[rows, :], y, g_ref[...], b_ref[...])
        of_ref[rows, :] = out
        ob_ref[rows, :] = out.astype(BF16)

    n_sub = tm // sub
    mix(0)
    pending = None
    for r in range(n_sub):
        y = _dot(x_ref[r * sub:(r + 1) * sub, :], w_ref[...])
        if r + 1 < n_sub:
            mix(r + 1)
        if pending is not None:
            finish(*pending)
        pending = (r, y)
    finish(*pending)


def _mix_out_proj_ln(outs, lses, w, res, g, b, *, seq, tm=512):
    bsz, _, _, d = outs[0].shape
    tm = min(tm, seq)
    tiles_per_batch = seq // tm

    def o_spec(dil):
        return pl.BlockSpec((None, dil, tm // dil, d), lambda i: (i // tiles_per_batch, 0, i % tiles_per_batch, 0))

    row = pl.BlockSpec((tm, d), lambda i: (i, 0))
    lrow = pl.BlockSpec((tm, DIL_HEADS), lambda i: (i, 0))
    vec = pl.BlockSpec((1, d), lambda i: (0, 0))
    dils = [o.shape[1] for o in outs]
    assert dils[0] == 1
    return pl.pallas_call(
        _mix_out_proj_kernel,
        grid=(bsz * tiles_per_batch,),
        in_specs=[row, o_spec(dils[1]), o_spec(dils[2]), lrow, lrow, lrow,
                  pl.BlockSpec(w.shape, lambda i: (0, 0), pipeline_mode=pl.Buffered(1)), row, vec, vec],
        out_specs=[row, row],
        out_shape=[jax.ShapeDtypeStruct((bsz * seq, d), F32), jax.ShapeDtypeStruct((bsz * seq, d), BF16)],
        scratch_shapes=[pltpu.VMEM((DIL_HEADS, tm, HEAD_DIM), F32)] * 2 + [pltpu.VMEM((tm, d), BF16)],
        compiler_params=_params("parallel"),
        name="mix_out_proj_ln",
    )(outs[0].reshape(bsz * seq, d), outs[1], outs[2], *lses, w, res, g.reshape(1, d), b.reshape(1, d))


def _compress_kernel(x_ref, pe_ref, w1_ref, w2_ref, o_ref, stage_ref, first_ref, second_ref):
    seq = x_ref.shape[0]
    n = seq // CMP_STRIDE
    stage_ref[...] = x_ref[...].astype(F32)
    for j in range(CMP_STRIDE):
        piece = stage_ref[pl.ds(j, n, stride=CMP_STRIDE), :]
        lo = _dot((piece + pe_ref[j:j + 1, :]).astype(BF16), w1_ref[j * HEAD_DIM:(j + 1) * HEAD_DIM, :])
        hi_j = CMP_STRIDE + j
        hi = _dot((piece + pe_ref[hi_j:hi_j + 1, :]).astype(BF16), w1_ref[hi_j * HEAD_DIM:(hi_j + 1) * HEAD_DIM, :])
        if j == 0:
            first_ref[...] = lo
            second_ref[0:n, :] = hi
        else:
            first_ref[...] += lo
            second_ref[0:n, :] += hi
    second_ref[n:n + 8, :] = jnp.zeros((8, HEAD_DIM), F32)
    h = first_ref[...] + second_ref[1:n + 1, :]
    o_ref[...] = _dot(jax.nn.gelu(h).astype(BF16), w2_ref[...]).astype(o_ref.dtype)


def _compress(proj4, blk0, pe, w1, w2):
    bsz, _, seq, _ = proj4.shape
    n = seq // CMP_STRIDE
    const = lambda shape: pl.BlockSpec(shape, lambda b, g: (0, 0))
    return pl.pallas_call(
        _compress_kernel,
        grid=(bsz, NSA_KV_HEADS),
        in_specs=[pl.BlockSpec((None, None, seq, HEAD_DIM), lambda b, g: (b, 0, 0, blk0 + g)),
                  const((CMP_BLOCK, HEAD_DIM)), const((CMP_BLOCK * HEAD_DIM, HEAD_DIM)), const((HEAD_DIM, HEAD_DIM))],
        out_specs=pl.BlockSpec((None, n, HEAD_DIM), lambda b, g: (b * NSA_KV_HEADS + g, 0, 0)),
        out_shape=jax.ShapeDtypeStruct((bsz * NSA_KV_HEADS, n, HEAD_DIM), BF16),
        scratch_shapes=[pltpu.VMEM((seq, HEAD_DIM), F32), pltpu.VMEM((n, HEAD_DIM), F32),
                        pltpu.VMEM((n + 8, HEAD_DIM), F32)],
        compiler_params=_params("parallel", "parallel"),
        name="nsa_compress",
    )(proj4, pe, w1.astype(BF16), w2.astype(BF16))


def _nsa_kernel(q_ref, kc_ref, vc_ref, c2s_ref, ks_ref, vs_ref, ow_ref, g_ref, o_ref,
                kaug_ref, qaug_ref, s0_ref, s1_ref, p0_ref, p1_ref, a0_ref, a1_ref, m_ref, l_ref, acc_ref, ocmp_ref,
                *, tq, kt, n_blk, n_sel, n_var):
    grp = NSA_GROUP
    n_cmp = kc_ref.shape[0]
    seq = ks_ref.shape[0]
    n_halves = qaug_ref.shape[0]
    chunks_per_half = HEAD_DIM * SEL_BLOCK // kt
    q0 = pl.program_id(2) * tq

    @pl.when(pl.program_id(2) == 0)
    def _():
        def fill(r, carry):
            r0 = pl.multiple_of(r * kt, kt)
            blk_of_row = (lax.broadcasted_iota(jnp.int32, (kt, HEAD_DIM), 0) + r0) // SEL_BLOCK
            lane = lax.broadcasted_iota(jnp.int32, (kt, HEAD_DIM), 1)
            kaug_ref[pl.ds(r0, kt), 0:HEAD_DIM] = ks_ref[pl.ds(r0, kt), :]
            kaug_ref[pl.ds(r0, kt), HEAD_DIM:2 * HEAD_DIM] = jnp.where(
                lane == blk_of_row % HEAD_DIM, NEG, 0.0).astype(kaug_ref.dtype)
            return carry
        lax.fori_loop(0, seq // kt, fill, 0)

    qs = jnp.concatenate([q_ref[:, h * HEAD_DIM:(h + 1) * HEAD_DIM] for h in range(grp)], axis=0)
    qpos = q0 + lax.broadcasted_iota(jnp.int32, (tq, 1), 0)

    def select_blocks(n_vis, n_rows):
        cmp_end = lax.broadcasted_iota(jnp.int32, (tq, n_vis), 1) * CMP_STRIDE + (CMP_BLOCK - 1)
        cmp_bias = jnp.where(cmp_end <= qpos, 0.0, NEG)
        s = _dot_nt(qs, kc_ref[0:n_vis, :]).reshape(grp, tq, n_vis) + cmp_bias[None]
        m = jnp.max(s, axis=-1, keepdims=True)
        m = jnp.where(m < 0.5 * NEG, 0.0, m)
        p = jnp.exp2(s - m)
        l = jnp.sum(p, axis=-1, keepdims=True)
        p = p * (1.0 / jnp.where(l > 0.0, l, 1.0))
        ocmp_ref[...] = _dot(p.reshape(grp * tq, n_vis).astype(BF16), vc_ref[0:n_vis, :])

        p_sum = jnp.sum(p, axis=0)
        p_hi = p_sum.astype(BF16)
        p_lo = (p_sum - p_hi.astype(F32)).astype(BF16)
        c2s = c2s_ref[0:n_rows, 0:n_vis]
        imp = _dot_nt(c2s, p_hi) + _dot_nt(c2s, p_lo)

        blk = lax.broadcasted_iota(jnp.int32, (n_rows, tq), 0)
        qpos_lane = q0 + lax.broadcasted_iota(jnp.int32, (1, tq), 1)
        cur = qpos_lane // SEL_BLOCK
        forced = (blk == 0) | (blk == cur) | (blk == cur - 1)
        score = jnp.where(forced, jnp.inf, jnp.where(blk * SEL_BLOCK <= qpos_lane, imp, -1.0))
        blk_f = blk.astype(F32)
        not_sel = jnp.ones((n_rows, tq), F32)
        for _ in range(n_sel):
            best = jnp.max(score, axis=0, keepdims=True)
            first = jnp.min(jnp.where(score == best, blk_f, float(n_rows)), axis=0, keepdims=True)
            hit = blk_f == first
            not_sel = jnp.where(hit, 0.0, not_sel)
            score = jnp.where(hit, -2.0, score)
        if n_rows < n_pad:
            not_sel = jnp.concatenate([not_sel, jnp.ones((n_pad - n_rows, tq), F32)], axis=0)
        not_sel = not_sel.T
        for half in range(n_halves):
            flags = not_sel[:, half * HEAD_DIM:(half + 1) * HEAD_DIM].astype(qaug_ref.dtype)
            qaug_ref[half, :, 0:HEAD_DIM] = qs
            qaug_ref[half, :, HEAD_DIM:2 * HEAD_DIM] = jnp.concatenate([flags] * grp, axis=0)

    n_pad = c2s_ref.shape[0]
    for v in range(n_var):
        @pl.when(q0 // (seq // n_var) == v)
        def _(v=v):
            select_blocks((v + 1) * (n_cmp // n_var), (v + 1) * (n_pad // n_var))

    rows_blk = SOFTMAX_ROWS
    bufs = ((s0_ref, p0_ref, a0_ref), (s1_ref, p1_ref, a1_ref))
    m_ref[...] = jnp.full(m_ref.shape, NEG, F32)
    l_ref[...] = jnp.zeros(l_ref.shape, F32)
    acc_ref[...] = jnp.zeros(acc_ref.shape, F32)

    def scores(j, slot):
        k0 = pl.multiple_of(j * kt, kt)
        bufs[slot][0][...] = _dot_nt(qaug_ref[j // chunks_per_half], kaug_ref[pl.ds(k0, kt), :])

    def softmax(j, slot, causal):
        s_buf, p_buf, a_buf = bufs[slot]
        k0 = j * kt
        for rb in range(grp * tq // rows_blk):
            rows = slice(rb * rows_blk, (rb + 1) * rows_blk)
            sb = s_buf[rows, :]
            if causal:
                row_pos = q0 + (rb * rows_blk) % tq + lax.broadcasted_iota(jnp.int32, (rows_blk, 1), 0)
                key_pos = k0 + lax.broadcasted_iota(jnp.int32, (rows_blk, kt), 1)
                sb = jnp.where(key_pos <= row_pos, sb, NEG)
            m_prev = m_ref[rows, :]
            m_new = jnp.maximum(m_prev, jnp.max(sb, axis=-1, keepdims=True))
            alpha = jnp.exp2(m_prev - m_new)
            p = jnp.exp2(sb - jnp.concatenate([m_new] * (kt // HEAD_DIM), axis=1))
            l_ref[rows, :] = alpha * l_ref[rows, :] + jnp.sum(p, axis=-1, keepdims=True)
            m_ref[rows, :] = m_new
            a_buf[rows, :] = alpha
            p_buf[rows, :] = p.astype(p_buf.dtype)

    def weighted_values(j, slot):
        _, p_buf, a_buf = bufs[slot]
        k0 = pl.multiple_of(j * kt, kt)
        acc_ref[...] = a_buf[...] * acc_ref[...] + _dot(p_buf[...], vs_ref[pl.ds(k0, kt), :])

    n_interior = q0 // kt
    scores(0, 0)

    def interior_run(j0, count):
        for c in range(count):
            scores(j0 + c + 1, (c + 1) % 2)
            softmax(j0 + c, c % 2, causal=False)
            weighted_values(j0 + c, c % 2)

    unroll = NSA_CHUNK_UNROLL
    n_runs = n_interior // unroll

    def interior_body(r, carry):
        interior_run(r * unroll, unroll)
        return carry

    lax.fori_loop(0, n_runs, interior_body, 0)
    for count in (c for c in (4, 2) if c < unroll):
        @pl.when((n_interior % (2 * count)) // count == 1)
        def _(count=count):
            interior_run(n_interior // (2 * count) * (2 * count), count)

    @pl.when(n_interior % 2 == 0)
    def _():
        softmax(n_interior, 0, causal=True)
        weighted_values(n_interior, 0)

    @pl.when(n_interior % 2 == 1)
    def _():
        scores(n_interior, 1)
        softmax(n_interior - 1, 0, causal=False)
        weighted_values(n_interior - 1, 0)
        softmax(n_interior, 1, causal=True)
        weighted_values(n_interior, 1)

    o_slc = acc_ref[...] * (1.0 / l_ref[...])

    gates = g_ref[...]
    for h in range(grp):
        rows = slice(h * tq, (h + 1) * tq)
        cols = slice(h * HEAD_DIM, (h + 1) * HEAD_DIM)
        out = (gates[:, 3 * h:3 * h + 1] * ocmp_ref[rows, :]
               + gates[:, 3 * h + 1:3 * h + 2] * o_slc[rows]
               + gates[:, 3 * h + 2:3 * h + 3] * ow_ref[:, cols].astype(F32))
        o_ref[:, cols] = out.astype(o_ref.dtype)


def _sel_from_cmp(n_cmp_pad, n_blk_pad):
    pos = np.arange(n_cmp_pad)[:, None] * CMP_STRIDE + np.arange(CMP_BLOCK)[None, :]
    owner = pos // SEL_BLOCK
    frac = (owner[:, :, None] == np.arange(n_blk_pad)[None, None, :]).sum(axis=1) / CMP_BLOCK
    return jnp.asarray(frac.T, dtype=BF16)


def _nsa_attention(proj, kc, vc, o_win, gates, *, q_idx, ks_idx, vs_idx, tq=256, kt=512):
    bsz, seq, _ = proj.shape
    tq = min(tq, seq)
    kt = min(kt, seq)
    assert kt % tq == 0 and (HEAD_DIM * SEL_BLOCK) % kt == 0 and seq % kt == 0
    n_cmp = kc.shape[1]
    n_blk = seq // SEL_BLOCK
    n_sel = min(N_SELECT, n_blk)
    n_halves = -(-n_blk // HEAD_DIM)
    wq = NSA_GROUP * HEAD_DIM
    rows = NSA_GROUP * tq
    c2s = _sel_from_cmp(n_cmp, n_halves * HEAD_DIM)
    n_var = max(1, min(NSA_PREFIX_VARIANTS, n_cmp // (2 * HEAD_DIM)))
    assert seq % n_var == 0 and (seq // n_var) % tq == 0 and (n_halves * HEAD_DIM) % (8 * n_var) == 0
    kern = functools.partial(_nsa_kernel, tq=tq, kt=kt, n_blk=n_blk, n_sel=n_sel, n_var=n_var)
    full = lambda idx_fn: pl.BlockSpec((None, seq, HEAD_DIM), lambda b, g, i: (b, 0, idx_fn(g)))
    cmp_spec = pl.BlockSpec((None, n_cmp, HEAD_DIM), lambda b, g, i: (b * NSA_KV_HEADS + g, 0, 0))
    return pl.pallas_call(
        kern,
        grid=(bsz, NSA_KV_HEADS, seq // tq),
        in_specs=[
            pl.BlockSpec((None, tq, wq), lambda b, g, i: (b, i, q_idx(g))),
            cmp_spec, cmp_spec,
            pl.BlockSpec(c2s.shape, lambda b, g, i: (0, 0)),
            full(ks_idx), full(vs_idx),
            pl.BlockSpec((None, tq, wq), lambda b, g, i: (b, i, g)),
            pl.BlockSpec((None, tq, HEAD_DIM), lambda b, g, i: (b, i, g)),
        ],
        out_specs=pl.BlockSpec((None, tq, wq), lambda b, g, i: (b, i, g)),
        out_shape=jax.ShapeDtypeStruct((bsz, seq, NSA_KV_HEADS * wq), BF16),
        scratch_shapes=[pltpu.VMEM((seq, 2 * HEAD_DIM), BF16),
                        pltpu.VMEM((n_halves, rows, 2 * HEAD_DIM), BF16),
                        pltpu.VMEM((rows, kt), F32), pltpu.VMEM((rows, kt), F32),
                        pltpu.VMEM((rows, kt), BF16), pltpu.VMEM((rows, kt), BF16)]
                       + [pltpu.VMEM((rows, HEAD_DIM), F32)] * 6,
        compiler_params=_params("parallel", "parallel", "arbitrary"),
        name="nsa_core",
    )(proj, kc, vc, c2s, proj, proj, o_win, gates)


_BLK_AQ, _BLK_BQ = 0, NSA_HEADS
_BLK_AK = NSA_HEADS + SWA_HEADS
_BLK_BK = _BLK_AK + 3 * NSA_KV_HEADS
_BLK_AV = _BLK_BK + SWA_KV_HEADS
_BLK_BV = _BLK_AV + 3 * NSA_KV_HEADS


def _even_weights(w_in):
    aq, ak, av, ag, bq, bk, bv = jnp.split(w_in, np.cumsum(EVEN_WIDTHS)[:-1].tolist(), axis=-1)
    w_main = jnp.concatenate([aq, bq, ak, bk, av, bv], axis=-1).astype(BF16)
    ag = ag.reshape(D_MODEL, NSA_KV_HEADS, 3 * NSA_GROUP)
    w_gate = jnp.pad(ag, ((0, 0), (0, 0), (0, HEAD_DIM - 3 * NSA_GROUP))).reshape(D_MODEL, NSA_KV_HEADS * HEAD_DIM)
    return w_main, w_gate.astype(BF16)


def _tile_modes(blocks_per_tile, block_modes):
    modes = np.asarray(block_modes, np.int32).reshape(-1, blocks_per_tile)
    assert (modes == modes[:, :1]).all(), "a projection tile must not straddle q/k/v column groups"
    return jnp.asarray(modes[:, 0])


def _even_mixer(xf, xb, cos, sin, bsz, seq, w_in, w_o, pe_k, pe_v, ck_w1, ck_w2, cv_w1, cv_w2, sinks, g, b):
    w_main, w_gate = _even_weights(w_in)
    tn = 1024
    n_q, n_k = NSA_HEADS + SWA_HEADS, 3 * NSA_KV_HEADS + SWA_KV_HEADS
    modes = _tile_modes(tn // HEAD_DIM, [MODE_ROPE_SCALED] * n_q + [MODE_ROPE] * n_k + [MODE_PLAIN] * n_k)
    proj4 = _project(xb, w_main, modes, cos, sin, seq=seq, tn=tn)
    proj = proj4.reshape(bsz, seq, EVEN_PROJ)
    gates = _gate_project(xb, w_gate).reshape(bsz, seq, NSA_KV_HEADS * HEAD_DIM)

    kc = _compress(proj4, _BLK_AK, pe_k, ck_w1, ck_w2)
    vc = _compress(proj4, _BLK_AV, pe_v, cv_w1, cv_w2)
    o_win = _banded_attention(
        proj4, n_steps=NSA_KV_HEADS, hq=NSA_GROUP, grp=NSA_GROUP, window=NSA_WINDOW,
        q_at=lambda s: (0, s), k_at=lambda s: (0, _BLK_AK + 2 * NSA_KV_HEADS + s),
        v_at=lambda s: (0, _BLK_AV + 2 * NSA_KV_HEADS + s), o_at=lambda s: (0, s), o_dims=(1, NSA_Q),
        tq=512, pw=512).reshape(bsz, seq, NSA_Q)
    o_a = _nsa_attention(proj, kc, vc, o_win, gates, q_idx=lambda s: s,
                         ks_idx=lambda s: _BLK_AK + NSA_KV_HEADS + s, vs_idx=lambda s: _BLK_AV + NSA_KV_HEADS + s)
    swa_grp = SWA_HEADS // SWA_KV_HEADS
    o_b = _banded_attention(
        proj4, n_steps=SWA_KV_HEADS, hq=swa_grp, grp=swa_grp, window=SWA_WINDOW,
        q_at=lambda s: (0, NSA_HEADS // swa_grp + s), k_at=lambda s: (0, _BLK_BK + s),
        v_at=lambda s: (0, _BLK_BV + s), o_at=lambda s: (0, s), o_dims=(1, SWA_Q),
        tq=256, pw=128, sinks=sinks)
    m = bsz * seq
    w_o = w_o.astype(BF16)
    return _out_proj_ln([o_a.reshape(m, NSA_Q), o_b.reshape(m, SWA_Q)], [w_o[:NSA_Q], w_o[NSA_Q:]], xf, g, b)


def _odd_mixer(xf, xb, cos, sin, bsz, seq, w_in, w_o, g, b):
    tn = 1024
    modes = _tile_modes(tn // HEAD_DIM, [MODE_ROPE_SCALED] * DIL_HEADS + [MODE_ROPE] * DIL_HEADS + [MODE_PLAIN] * DIL_HEADS)
    m = bsz * seq
    width = DIL_HEADS * HEAD_DIM
    w_in = w_in.astype(BF16)
    outs, lses = [], []
    for grp_i, (window, dil) in enumerate(DIL_PATTERNS):
        qkv = _project(xb, w_in, modes, cos, sin, seq=seq, dil=dil, tn=tn, col0=grp_i * DIL_QKV, n=DIL_QKV)
        o, lse = _banded_attention(
            qkv, n_steps=dil, hq=DIL_HEADS, grp=1, window=window // dil + 1,
            q_at=lambda s: (s, 0), k_at=lambda s: (s, 1), v_at=lambda s: (s, 2), o_at=lambda s: (s, 0),
            o_dims=(dil, width), tq=256, pw=128, want_lse=True)
        outs.append(o)
        lses.append(lse.transpose(0, 2, 1, 3).reshape(m, DIL_HEADS))
    return _mix_out_proj_ln(outs, lses, w_o.astype(BF16), xf, g, b, seq=seq)


def kernel(x, positions, e_w_in, e_w_o, nsa_pe_k, nsa_pe_v, nsa_ck_w1, nsa_ck_w2, nsa_cv_w1, nsa_cv_w2, swa_sinks, o_w_in, o_w_o, ln1_g, ln1_b, mlp_w1, mlp_w2, ln2_g, ln2_b):
    bsz, seq, d = x.shape
    m = bsz * seq
    cos, sin = _rope_tables(positions)
    xf = x.reshape(m, d)
    xb = xf.astype(BF16)
    for layer in range(DEPTH):
        i = layer // 2
        if layer % 2 == 0:
            xf, xb = _even_mixer(xf, xb, cos, sin, bsz, seq, e_w_in[i], e_w_o[i], nsa_pe_k[i], nsa_pe_v[i],
                                 nsa_ck_w1[i], nsa_ck_w2[i], nsa_cv_w1[i], nsa_cv_w2[i], swa_sinks[i],
                                 ln1_g[layer], ln1_b[layer])
        else:
            xf, xb = _odd_mixer(xf, xb, cos, sin, bsz, seq, o_w_in[i], o_w_o[i], ln1_g[layer], ln1_b[layer])
        xf, xb = _mlp_ln(xf, xb, mlp_w1[layer].astype(BF16), mlp_w2[layer].astype(BF16), ln2_g[layer], ln2_b[layer])
    return xf.reshape(bsz, seq, d)
```

```python
import functools
import math

import numpy as np
import jax
import jax.numpy as jnp
from jax import lax
from jax.experimental import pallas as pl
from jax.experimental.pallas import tpu as pltpu

D_MODEL = 2048
DEPTH = 4
HEAD_DIM = 128
ROPE_THETA = 10000.0
LN_EPS = 1e-5

NSA_HEADS = D_MODEL // (2 * HEAD_DIM)
NSA_KV_HEADS = 2
NSA_GROUP = NSA_HEADS // NSA_KV_HEADS
CMP_BLOCK = 32
CMP_STRIDE = 16
SEL_BLOCK = 64
N_SELECT = 16
NSA_WINDOW = 512

SWA_HEADS = D_MODEL // (2 * HEAD_DIM)
SWA_KV_HEADS = 2
SWA_WINDOW = 128

DIL_HEADS = D_MODEL // HEAD_DIM
DIL_PATTERNS = ((128, 1), (512, 4), (2048, 16))

D_FF = 4 * D_MODEL
DN_ALPHA = (2 * DEPTH) ** 0.25

NSA_Q = NSA_HEADS * HEAD_DIM
NSA_KV = NSA_KV_HEADS * HEAD_DIM
SWA_Q = SWA_HEADS * HEAD_DIM
SWA_KV = SWA_KV_HEADS * HEAD_DIM
EVEN_WIDTHS = (NSA_Q, 3 * NSA_KV, 3 * NSA_KV, 3 * NSA_HEADS, SWA_Q, SWA_KV, SWA_KV)
EVEN_PROJ = NSA_Q + SWA_Q + 3 * NSA_KV + SWA_KV + 3 * NSA_KV + SWA_KV
DIL_QKV = 3 * DIL_HEADS * HEAD_DIM

LOG2E = math.log2(math.e)
LN2 = math.log(2.0)
QK_SCALE = HEAD_DIM ** -0.5 * LOG2E
NEG = -1e30
SOFTMAX_ROWS = 32
BANDED_SOFTMAX_ROWS = 32
NSA_CHUNK_UNROLL = 4
NSA_PREFIX_VARIANTS = 4
PROJ_SUB_ROWS = 256
VMEM_LIMIT_BYTES = 56 * 1024 * 1024

MODE_PLAIN, MODE_ROPE, MODE_ROPE_SCALED = 0, 1, 2

F32 = jnp.float32
BF16 = jnp.bfloat16


def _params(*semantics):
    return pltpu.CompilerParams(dimension_semantics=semantics, vmem_limit_bytes=VMEM_LIMIT_BYTES)


def _dot(a, b):
    return jnp.dot(a, b, preferred_element_type=F32)


def _dot_nt(a, b):
    return lax.dot_general(a, b, (((1,), (1,)), ((), ())), preferred_element_type=F32)


def _rope_table_kernel(pos_ref, inv_ref, sign_ref, cos_ref, sin_ref):
    ang = pos_ref[...].astype(F32) * inv_ref[...]
    cos_ref[...] = jnp.cos(ang)
    sin_ref[...] = jnp.sin(ang) * sign_ref[...]


def _rope_tables(positions):
    m = positions.size
    tm = min(m, 2048)
    inv = ROPE_THETA ** (-jnp.arange(0, HEAD_DIM, 2, dtype=F32) / HEAD_DIM)
    inv_full = jnp.concatenate([inv, inv]).reshape(1, HEAD_DIM)
    half = HEAD_DIM // 2
    sign = jnp.concatenate([-jnp.ones((half,), F32), jnp.ones((half,), F32)]).reshape(1, HEAD_DIM)
    row = pl.BlockSpec((tm, HEAD_DIM), lambda i: (i, 0))
    const = pl.BlockSpec((1, HEAD_DIM), lambda i: (0, 0))
    return pl.pallas_call(
        _rope_table_kernel,
        grid=(m // tm,),
        in_specs=[pl.BlockSpec((tm, 1), lambda i: (i, 0)), const, const],
        out_specs=[row, row],
        out_shape=[jax.ShapeDtypeStruct((m, HEAD_DIM), F32)] * 2,
        compiler_params=_params("parallel"),
        name="rope_tables",
    )(positions.reshape(m, 1), inv_full, sign)


def _proj_kernel(modes_ref, x_ref, w_ref, cos_ref, sin_ref, o_ref, *scratch, tn, dil):
    mode = modes_ref[pl.program_id(1)]
    tm = x_ref.shape[0]
    sub = min(PROJ_SUB_ROWS, tm)
    blocks = [slice(blk * HEAD_DIM, (blk + 1) * HEAD_DIM) for blk in range(tn // HEAD_DIM)]
    is_rope = mode != MODE_PLAIN
    scale = jnp.where(mode == MODE_ROPE_SCALED, QK_SCALE, 1.0).astype(F32)

    def epilogue(r, acc):
        rows = slice(r * sub, (r + 1) * sub)
        cos = jnp.where(is_rope, cos_ref[rows, :] * scale, 1.0)
        sin = jnp.where(is_rope, sin_ref[rows, :] * scale, 0.0)
        for blk, cols in enumerate(blocks):
            t = acc[:, cols]
            val = t * cos + pltpu.roll(t, HEAD_DIM // 2, 1) * sin
            if dil == 1:
                o_ref[rows, cols] = val.astype(o_ref.dtype)
            else:
                scratch[0][blk, rows, :] = val
        if dil > 1:
            n = sub // dil
            for c in range(dil):
                for blk, cols in enumerate(blocks):
                    o_ref[c, r * n:(r + 1) * n, cols] = scratch[0][
                        blk, pl.ds(r * sub + c, n, stride=dil), :].astype(o_ref.dtype)

    pending = None
    for r in range(tm // sub):
        acc = _dot(x_ref[r * sub:(r + 1) * sub, :], w_ref[...])
        if pending is not None:
            epilogue(*pending)
        pending = (r, acc)
    epilogue(*pending)


def _project(x, w, modes, cos, sin, *, seq, dil=1, tm=1024, tn=1024, col0=0, n=None):
    m, k = x.shape
    n = w.shape[1] if n is None else n
    assert col0 % tn == 0 and n % tn == 0
    tile0 = col0 // tn
    tm = min(tm, seq)
    tiles_per_batch = seq // tm
    bsz = m // seq
    if dil == 1:
        out_spec = pl.BlockSpec((tm, tn), lambda i, j, modes: (i, j))
        out_shape = jax.ShapeDtypeStruct((m, n), BF16)
        scratch = []
    else:
        out_spec = pl.BlockSpec((None, dil, tm // dil, tn),
                                lambda i, j, modes: (i // tiles_per_batch, 0, i % tiles_per_batch, j))
        out_shape = jax.ShapeDtypeStruct((bsz, dil, seq // dil, n), BF16)
        scratch = [pltpu.VMEM((tn // HEAD_DIM, tm, HEAD_DIM), F32)]
    grid_spec = pltpu.PrefetchScalarGridSpec(
        num_scalar_prefetch=1,
        grid=(m // tm, n // tn),
        in_specs=[
            pl.BlockSpec((tm, k), lambda i, j, modes: (i, 0)),
            pl.BlockSpec((k, tn), lambda i, j, modes: (0, tile0 + j)),
            pl.BlockSpec((tm, HEAD_DIM), lambda i, j, modes: (i, 0)),
            pl.BlockSpec((tm, HEAD_DIM), lambda i, j, modes: (i, 0)),
        ],
        out_specs=out_spec,
        scratch_shapes=scratch,
    )
    out = pl.pallas_call(
        functools.partial(_proj_kernel, tn=tn, dil=dil),
        grid_spec=grid_spec,
        out_shape=out_shape,
        compiler_params=_params("parallel", "arbitrary"),
        name="in_proj" if dil == 1 else "in_proj_dil%d" % dil,
    )(modes, x, w, cos, sin)
    return out.reshape(bsz, 1, seq, n) if dil == 1 else out


def _gate_kernel(x_ref, w_ref, o_ref):
    o_ref[...] = jax.nn.sigmoid(_dot(x_ref[...], w_ref[...]))


def _gate_project(x, w, *, tm=1024):
    m, k = x.shape
    n = w.shape[1]
    tm = min(tm, m)
    return pl.pallas_call(
        _gate_kernel,
        grid=(m // tm,),
        in_specs=[pl.BlockSpec((tm, k), lambda i: (i, 0)), pl.BlockSpec((k, n), lambda i: (0, 0))],
        out_specs=pl.BlockSpec((tm, n), lambda i: (i, 0)),
        out_shape=jax.ShapeDtypeStruct((m, n), F32),
        compiler_params=_params("parallel"),
        name="gate_proj",
    )(x, w)


def _residual_layer_norm(res, y, g, b):
    z = DN_ALPHA * res + y
    mu = jnp.mean(z, axis=-1, keepdims=True)
    zc = z - mu
    var = jnp.mean(zc * zc, axis=-1, keepdims=True)
    return zc * lax.rsqrt(var + LN_EPS) * g + b


def _out_proj_kernel(*refs, n_in):
    x_refs, w_refs = refs[:n_in], refs[n_in:2 * n_in]
    res_ref, g_ref, b_ref, of_ref, ob_ref = refs[2 * n_in:]
    tm = res_ref.shape[0]
    sub = min(PROJ_SUB_ROWS, tm)

    def finish(r, y):
        rows = slice(r * sub, (r + 1) * sub)
        out = _residual_layer_norm(res_ref[rows, :], y, g_ref[...], b_ref[...])
        of_ref[rows, :] = out
        ob_ref[rows, :] = out.astype(BF16)

    pending = None
    for r in range(tm // sub):
        rows = slice(r * sub, (r + 1) * sub)
        y = _dot(x_refs[0][rows, :], w_refs[0][...])
        for x_ref, w_ref in zip(x_refs[1:], w_refs[1:]):
            y += _dot(x_ref[rows, :], w_ref[...])
        if pending is not None:
            finish(*pending)
        pending = (r, y)
    finish(*pending)


def _out_proj_ln(xs, ws, res, g, b, *, tm=512):
    m, d = res.shape
    tm = min(tm, m)
    n_in = len(xs)
    in_specs = [pl.BlockSpec((tm, x.shape[1]), lambda i: (i, 0)) for x in xs]
    in_specs += [pl.BlockSpec(w.shape, lambda i: (0, 0)) for w in ws]
    row = pl.BlockSpec((tm, d), lambda i: (i, 0))
    vec = pl.BlockSpec((1, d), lambda i: (0, 0))
    in_specs += [row, vec, vec]
    return pl.pallas_call(
        functools.partial(_out_proj_kernel, n_in=n_in),
        grid=(m // tm,),
        in_specs=in_specs,
        out_specs=[row, row],
        out_shape=[jax.ShapeDtypeStruct((m, d), F32), jax.ShapeDtypeStruct((m, d), BF16)],
        compiler_params=_params("parallel"),
        name="out_proj_ln",
    )(*xs, *ws, res, g.reshape(1, d), b.reshape(1, d))


def _mlp_kernel(xb_ref, w1_ref, w2_ref, xf_ref, g_ref, b_ref, of_ref, ob_ref, acc_ref):
    f = pl.program_id(1)
    tm = xb_ref.shape[0]
    sub = min(PROJ_SUB_ROWS, tm)
    @pl.when(f == 0)
    def _():
        acc_ref[...] = jnp.zeros(acc_ref.shape, F32)

    def chunk(finish):
        hidden = [None] * (tm // sub)
        for r in range(tm // sub + 1):
            if r < tm // sub:
                hidden[r] = _dot(xb_ref[r * sub:(r + 1) * sub, :], w1_ref[...])
            if r > 0:
                rows = slice((r - 1) * sub, r * sub)
                act = jnp.square(jnp.maximum(hidden[r - 1], 0.0)).astype(BF16)
                total = acc_ref[rows, :] + _dot(act, w2_ref[...])
                if finish:
                    out = _residual_layer_norm(xf_ref[rows, :], total, g_ref[...], b_ref[...])
                    of_ref[rows, :] = out
                    ob_ref[rows, :] = out.astype(BF16)
                else:
                    acc_ref[rows, :] = total

    last = pl.num_programs(1) - 1
    pl.when(f != last)(functools.partial(chunk, False))
    pl.when(f == last)(functools.partial(chunk, True))


def _mlp_ln(xf, xb, w1, w2, g, b, *, tm=512, tf=1024):
    m, d = xf.shape
    ff = w1.shape[1]
    tm = min(tm, m)
    row = pl.BlockSpec((tm, d), lambda i, f: (i, 0))
    vec = pl.BlockSpec((1, d), lambda i, f: (0, 0))
    w1 = w1.reshape(d, ff // tf, tf).transpose(1, 0, 2)
    return pl.pallas_call(
        _mlp_kernel,
        grid=(m // tm, ff // tf),
        in_specs=[row, pl.BlockSpec((None, d, tf), lambda i, f: (f, 0, 0)), pl.BlockSpec((tf, d), lambda i, f: (f, 0)),
                  row, vec, vec],
        out_specs=[row, row],
        out_shape=[jax.ShapeDtypeStruct((m, d), F32), jax.ShapeDtypeStruct((m, d), BF16)],
        scratch_shapes=[pltpu.VMEM((tm, d), F32)],
        compiler_params=_params("parallel", "arbitrary"),
        name="mlp_ln",
    )(xb, w1, w2, xf, g.reshape(1, d), b.reshape(1, d))


def _banded_kernel(*refs, hq, grp, window, tq, pw, has_sink, want_lse):
    refs = list(refs)
    sink_ref = refs.pop(0) if has_sink else None
    q_ref, kc_ref, kp_ref, vc_ref, vp_ref, o_ref = refs[:6]
    lse_ref = refs[6] if want_lse else None
    bias_ref = refs[-1]
    score_refs = refs[-1 - 3 * hq:-1 - 2 * hq]
    prob_refs = refs[-1 - 2 * hq:-1 - hq]
    linv_refs = refs[-1 - hq:-1]
    step, i = pl.program_id(1), pl.program_id(2)
    span = pw + tq
    rows_blk = min(BANDED_SOFTMAX_ROWS, tq)

    row = lax.broadcasted_iota(jnp.int32, (tq, span), 0)
    rel = lax.broadcasted_iota(jnp.int32, (tq, span), 1) - pw - row
    mask = (rel <= 0) & (rel > -window) & ((rel + row >= 0) | (i > 0))
    bias_ref[...] = jnp.where(mask, 0.0, NEG)

    @pl.when(i == 0)
    def _():
        for h in range(hq):
            prob_refs[h][...] = jnp.zeros(prob_refs[h].shape, prob_refs[h].dtype)

    for h in range(hq):
        kv = h // grp
        q = q_ref[:, h * HEAD_DIM:(h + 1) * HEAD_DIM]
        kcols = slice(kv * HEAD_DIM, (kv + 1) * HEAD_DIM)
        score_refs[h][:, :pw] = _dot_nt(q, kp_ref[:, kcols])
        score_refs[h][:, pw:] = _dot_nt(q, kc_ref[:, kcols])

    for h in range(hq):
        kv = h // grp
        qcols = slice(h * HEAD_DIM, (h + 1) * HEAD_DIM)
        kcols = slice(kv * HEAD_DIM, (kv + 1) * HEAD_DIM)
        if has_sink:
            sink = sink_ref[step * hq + h] * LOG2E
        for rb in range(tq // rows_blk):
            r0 = rb * rows_blk
            rows = slice(r0, r0 + rows_blk)
            cols = slice(max(0, r0 + pw - (window - 1)) // HEAD_DIM * HEAD_DIM,
                         -(-(r0 + rows_blk + pw) // HEAD_DIM) * HEAD_DIM)
            s = score_refs[h][rows, cols] + bias_ref[rows, cols]
            m = jnp.max(s, axis=-1, keepdims=True)
            if has_sink:
                m = jnp.maximum(m, sink)
            p = jnp.exp2(s - m)
            l = jnp.sum(p, axis=-1, keepdims=True)
            if has_sink:
                l = l + jnp.exp2(sink - m)
            prob_refs[h][rows, cols] = p.astype(BF16)
            linv_refs[h][rows, :] = jnp.broadcast_to(1.0 / l, (rows_blk, HEAD_DIM))
            if want_lse:
                lse_ref[rows, h:h + 1] = (m + jnp.log2(l)) * LN2
        o = _dot(prob_refs[h][:, :pw], vp_ref[:, kcols]) + _dot(prob_refs[h][:, pw:], vc_ref[:, kcols])
        o_ref[:, qcols] = (o * linv_refs[h][...]).astype(o_ref.dtype)


def _banded_attention(arr, *, n_steps, hq, grp, window, q_at, k_at, v_at, o_at, o_dims, tq, pw,
                      sinks=None, want_lse=False):
    bsz, _, seq, _ = arr.shape
    tq = min(tq, seq)
    pw = min(pw, tq)
    assert pw >= window - 1 or seq <= pw, (pw, window, seq)
    assert seq % tq == 0 and tq % pw == 0
    hk = hq // grp
    ratio = tq // pw
    wq, wk = hq * HEAD_DIM, hk * HEAD_DIM
    has_sink = sinks is not None

    def spec(width, rows, at_fn, row_fn):
        def index(b, s, i, *_):
            plane, col = at_fn(s)
            return (b, plane, row_fn(i), col)
        return pl.BlockSpec((None, None, rows, width), index)

    cur = lambda i: i
    prev = lambda i: jnp.maximum(i * ratio - 1, 0)
    in_specs = [spec(wq, tq, q_at, cur), spec(wk, tq, k_at, cur), spec(wk, pw, k_at, prev),
                spec(wk, tq, v_at, cur), spec(wk, pw, v_at, prev)]
    out_specs = [spec(wq, tq, o_at, cur)]
    out_shape = [jax.ShapeDtypeStruct((bsz, o_dims[0], seq, o_dims[1]), BF16)]
    if want_lse:
        out_specs.append(pl.BlockSpec((None, None, tq, hq), lambda b, s, i: (b, s, i, 0)))
        out_shape.append(jax.ShapeDtypeStruct((bsz, n_steps, seq, hq), F32))
    kern = functools.partial(_banded_kernel, hq=hq, grp=grp, window=window, tq=tq, pw=pw,
                             has_sink=has_sink, want_lse=want_lse)
    grid = (bsz, n_steps, seq // tq)
    params = _params("parallel", "parallel", "arbitrary")
    scratch = ([pltpu.VMEM((tq, pw + tq), F32)] * hq + [pltpu.VMEM((tq, pw + tq), BF16)] * hq
               + [pltpu.VMEM((tq, HEAD_DIM), F32)] * hq + [pltpu.VMEM((tq, pw + tq), F32)])
    if has_sink:
        grid_spec = pltpu.PrefetchScalarGridSpec(num_scalar_prefetch=1, grid=grid, in_specs=in_specs,
                                                 out_specs=out_specs, scratch_shapes=scratch)
        out = pl.pallas_call(kern, grid_spec=grid_spec, out_shape=out_shape, compiler_params=params,
                             name="banded_attn_sink")(sinks.astype(F32), arr, arr, arr, arr, arr)
    else:
        out = pl.pallas_call(kern, grid=grid, in_specs=in_specs, out_specs=out_specs, out_shape=out_shape,
                             scratch_shapes=scratch, compiler_params=params, name="banded_attn")(arr, arr, arr, arr, arr)
    return out if want_lse else out[0]


def _mix_out_proj_kernel(o0_ref, o1_ref, o2_ref, l0_ref, l1_ref, l2_ref, w_ref, res_ref, g_ref, b_ref,
                         of_ref, ob_ref, t1_ref, t2_ref, x_ref):
    tm = res_ref.shape[0]
    sub = min(PROJ_SUB_ROWS, tm)

    def mix(r):
        rows = slice(r * sub, (r + 1) * sub)
        for src, dst in ((o1_ref, t1_ref), (o2_ref, t2_ref)):
            dil = src.shape[0]
            n = sub // dil
            for c in range(dil):
                for h in range(DIL_HEADS):
                    dst[h, pl.ds(r * sub + c, n, stride=dil), :] = src[
                        c, r * n:(r + 1) * n, h * HEAD_DIM:(h + 1) * HEAD_DIM].astype(F32)
        l0, l1, l2 = l0_ref[rows, :], l1_ref[rows, :], l2_ref[rows, :]
        m = jnp.maximum(jnp.maximum(l0, l1), l2)
        e0, e1, e2 = jnp.exp(l0 - m), jnp.exp(l1 - m), jnp.exp(l2 - m)
        inv = 1.0 / (e0 + e1 + e2)
        w0, w1, w2 = e0 * inv, e1 * inv, e2 * inv
        for h in range(DIL_HEADS):
            cols = slice(h * HEAD_DIM, (h + 1) * HEAD_DIM)
            mixed = (w0[:, h:h + 1] * o0_ref[rows, cols].astype(F32)
                     + w1[:, h:h + 1] * t1_ref[h, rows, :]
                     + w2[:, h:h + 1] * t2_ref[h, rows, :])
            x_ref[rows, cols] = mixed.astype(x_ref.dtype)

    def finish(r, y):
        rows = slice(r * sub, (r + 1) * sub)
        out = _residual_layer_norm(res_ref[rows, :], y, g_ref[...], b_ref[...])
        of_ref[rows, :] = out
        ob_ref[rows, :] = out.astype(BF16)

    n_sub = tm // sub
    mix(0)
    pending = None
    for r in range(n_sub):
        y = _dot(x_ref[r * sub:(r + 1) * sub, :], w_ref[...])
        if r + 1 < n_sub:
            mix(r + 1)
        if pending is not None:
            finish(*pending)
        pending = (r, y)
    finish(*pending)


def _mix_out_proj_ln(outs, lses, w, res, g, b, *, seq, tm=512):
    bsz, _, _, d = outs[0].shape
    tm = min(tm, seq)
    tiles_per_batch = seq // tm

    def o_spec(dil):
        return pl.BlockSpec((None, dil, tm // dil, d), lambda i: (i // tiles_per_batch, 0, i % tiles_per_batch, 0))

    row = pl.BlockSpec((tm, d), lambda i: (i, 0))
    lrow = pl.BlockSpec((tm, DIL_HEADS), lambda i: (i, 0))
    vec = pl.BlockSpec((1, d), lambda i: (0, 0))
    dils = [o.shape[1] for o in outs]
    assert dils[0] == 1
    return pl.pallas_call(
        _mix_out_proj_kernel,
        grid=(bsz * tiles_per_batch,),
        in_specs=[row, o_spec(dils[1]), o_spec(dils[2]), lrow, lrow, lrow,
                  pl.BlockSpec(w.shape, lambda i: (0, 0), pipeline_mode=pl.Buffered(1)), row, vec, vec],
        out_specs=[row, row],
        out_shape=[jax.ShapeDtypeStruct((bsz * seq, d), F32), jax.ShapeDtypeStruct((bsz * seq, d), BF16)],
        scratch_shapes=[pltpu.VMEM((DIL_HEADS, tm, HEAD_DIM), F32)] * 2 + [pltpu.VMEM((tm, d), BF16)],
        compiler_params=_params("parallel"),
        name="mix_out_proj_ln",
    )(outs[0].reshape(bsz * seq, d), outs[1], outs[2], *lses, w, res, g.reshape(1, d), b.reshape(1, d))


def _compress_kernel(x_ref, pe_ref, w1_ref, w2_ref, o_ref, stage_ref, first_ref, second_ref):
    seq = x_ref.shape[0]
    n = seq // CMP_STRIDE
    stage_ref[...] = x_ref[...].astype(F32)
    for j in range(CMP_STRIDE):
        piece = stage_ref[pl.ds(j, n, stride=CMP_STRIDE), :]
        lo = _dot((piece + pe_ref[j:j + 1, :]).astype(BF16), w1_ref[j * HEAD_DIM:(j + 1) * HEAD_DIM, :])
        hi_j = CMP_STRIDE + j
        hi = _dot((piece + pe_ref[hi_j:hi_j + 1, :]).astype(BF16), w1_ref[hi_j * HEAD_DIM:(hi_j + 1) * HEAD_DIM, :])
        if j == 0:
            first_ref[...] = lo
            second_ref[0:n, :] = hi
        else:
            first_ref[...] += lo
            second_ref[0:n, :] += hi
    second_ref[n:n + 8, :] = jnp.zeros((8, HEAD_DIM), F32)
    h = first_ref[...] + second_ref[1:n + 1, :]
    o_ref[...] = _dot(jax.nn.gelu(h).astype(BF16), w2_ref[...]).astype(o_ref.dtype)


def _compress(proj4, blk0, pe, w1, w2):
    bsz, _, seq, _ = proj4.shape
    n = seq // CMP_STRIDE
    const = lambda shape: pl.BlockSpec(shape, lambda b, g: (0, 0))
    return pl.pallas_call(
        _compress_kernel,
        grid=(bsz, NSA_KV_HEADS),
        in_specs=[pl.BlockSpec((None, None, seq, HEAD_DIM), lambda b, g: (b, 0, 0, blk0 + g)),
                  const((CMP_BLOCK, HEAD_DIM)), const((CMP_BLOCK * HEAD_DIM, HEAD_DIM)), const((HEAD_DIM, HEAD_DIM))],
        out_specs=pl.BlockSpec((None, n, HEAD_DIM), lambda b, g: (b * NSA_KV_HEADS + g, 0, 0)),
        out_shape=jax.ShapeDtypeStruct((bsz * NSA_KV_HEADS, n, HEAD_DIM), BF16),
        scratch_shapes=[pltpu.VMEM((seq, HEAD_DIM), F32), pltpu.VMEM((n, HEAD_DIM), F32),
                        pltpu.VMEM((n + 8, HEAD_DIM), F32)],
        compiler_params=_params("parallel", "parallel"),
        name="nsa_compress",
    )(proj4, pe, w1.astype(BF16), w2.astype(BF16))


def _nsa_kernel(q_ref, kc_ref, vc_ref, c2s_ref, ks_ref, vs_ref, ow_ref, g_ref, o_ref,
                kaug_ref, qaug_ref, s0_ref, s1_ref, p0_ref, p1_ref, a0_ref, a1_ref, m_ref, l_ref, acc_ref, ocmp_ref,
                *, tq, kt, n_blk, n_sel, n_var):
    grp = NSA_GROUP
    n_cmp = kc_ref.shape[0]
    seq = ks_ref.shape[0]
    n_halves = qaug_ref.shape[0]
    chunks_per_half = HEAD_DIM * SEL_BLOCK // kt
    q0 = pl.program_id(2) * tq

    @pl.when(pl.program_id(2) == 0)
    def _():
        def fill(r, carry):
            r0 = pl.multiple_of(r * kt, kt)
            blk_of_row = (lax.broadcasted_iota(jnp.int32, (kt, HEAD_DIM), 0) + r0) // SEL_BLOCK
            lane = lax.broadcasted_iota(jnp.int32, (kt, HEAD_DIM), 1)
            kaug_ref[pl.ds(r0, kt), 0:HEAD_DIM] = ks_ref[pl.ds(r0, kt), :]
            kaug_ref[pl.ds(r0, kt), HEAD_DIM:2 * HEAD_DIM] = jnp.where(
                lane == blk_of_row % HEAD_DIM, NEG, 0.0).astype(kaug_ref.dtype)
            return carry
        lax.fori_loop(0, seq // kt, fill, 0)

    qs = jnp.concatenate([q_ref[:, h * HEAD_DIM:(h + 1) * HEAD_DIM] for h in range(grp)], axis=0)
    qpos = q0 + lax.broadcasted_iota(jnp.int32, (tq, 1), 0)

    def select_blocks(n_vis, n_rows):
        cmp_end = lax.broadcasted_iota(jnp.int32, (tq, n_vis), 1) * CMP_STRIDE + (CMP_BLOCK - 1)
        cmp_bias = jnp.where(cmp_end <= qpos, 0.0, NEG)
        s = _dot_nt(qs, kc_ref[0:n_vis, :]).reshape(grp, tq, n_vis) + cmp_bias[None]
        m = jnp.max(s, axis=-1, keepdims=True)
        m = jnp.where(m < 0.5 * NEG, 0.0, m)
        p = jnp.exp2(s - m)
        l = jnp.sum(p, axis=-1, keepdims=True)
        p = p * (1.0 / jnp.where(l > 0.0, l, 1.0))
        ocmp_ref[...] = _dot(p.reshape(grp * tq, n_vis).astype(BF16), vc_ref[0:n_vis, :])

        p_sum = jnp.sum(p, axis=0)
        p_hi = p_sum.astype(BF16)
        p_lo = (p_sum - p_hi.astype(F32)).astype(BF16)
        c2s = c2s_ref[0:n_rows, 0:n_vis]
        imp = _dot_nt(c2s, p_hi) + _dot_nt(c2s, p_lo)

        blk = lax.broadcasted_iota(jnp.int32, (n_rows, tq), 0)
        qpos_lane = q0 + lax.broadcasted_iota(jnp.int32, (1, tq), 1)
        cur = qpos_lane // SEL_BLOCK
        forced = (blk == 0) | (blk == cur) | (blk == cur - 1)
        score = jnp.where(forced, jnp.inf, jnp.where(blk * SEL_BLOCK <= qpos_lane, imp, -1.0))
        blk_f = blk.astype(F32)
        not_sel = jnp.ones((n_rows, tq), F32)
        for _ in range(n_sel):
            best = jnp.max(score, axis=0, keepdims=True)
            first = jnp.min(jnp.where(score == best, blk_f, float(n_rows)), axis=0, keepdims=True)
            hit = blk_f == first
            not_sel = jnp.where(hit, 0.0, not_sel)
            score = jnp.where(hit, -2.0, score)
        if n_rows < n_pad:
            not_sel = jnp.concatenate([not_sel, jnp.ones((n_pad - n_rows, tq), F32)], axis=0)
        not_sel = not_sel.T
        for half in range(n_halves):
            flags = not_sel[:, half * HEAD_DIM:(half + 1) * HEAD_DIM].astype(qaug_ref.dtype)
            qaug_ref[half, :, 0:HEAD_DIM] = qs
            qaug_ref[half, :, HEAD_DIM:2 * HEAD_DIM] = jnp.concatenate([flags] * grp, axis=0)

    n_pad = c2s_ref.shape[0]
    for v in range(n_var):
        @pl.when(q0 // (seq // n_var) == v)
        def _(v=v):
            select_blocks((v + 1) * (n_cmp // n_var), (v + 1) * (n_pad // n_var))

    rows_blk = SOFTMAX_ROWS
    bufs = ((s0_ref, p0_ref, a0_ref), (s1_ref, p1_ref, a1_ref))
    m_ref[...] = jnp.full(m_ref.shape, NEG, F32)
    l_ref[...] = jnp.zeros(l_ref.shape, F32)
    acc_ref[...] = jnp.zeros(acc_ref.shape, F32)

    def scores(j, slot):
        k0 = pl.multiple_of(j * kt, kt)
        bufs[slot][0][...] = _dot_nt(qaug_ref[j // chunks_per_half], kaug_ref[pl.ds(k0, kt), :])

    def softmax(j, slot, causal):
        s_buf, p_buf, a_buf = bufs[slot]
        k0 = j * kt
        for rb in range(grp * tq // rows_blk):
            rows = slice(rb * rows_blk, (rb + 1) * rows_blk)
            sb = s_buf[rows, :]
            if causal:
                row_pos = q0 + (rb * rows_blk) % tq + lax.broadcasted_iota(jnp.int32, (rows_blk, 1), 0)
                key_pos = k0 + lax.broadcasted_iota(jnp.int32, (rows_blk, kt), 1)
                sb = jnp.where(key_pos <= row_pos, sb, NEG)
            m_prev = m_ref[rows, :]
            m_new = jnp.maximum(m_prev, jnp.max(sb, axis=-1, keepdims=True))
            alpha = jnp.exp2(m_prev - m_new)
            p = jnp.exp2(sb - jnp.concatenate([m_new] * (kt // HEAD_DIM), axis=1))
            l_ref[rows, :] = alpha * l_ref[rows, :] + jnp.sum(p, axis=-1, keepdims=True)
            m_ref[rows, :] = m_new
            a_buf[rows, :] = alpha
            p_buf[rows, :] = p.astype(p_buf.dtype)

    def weighted_values(j, slot):
        _, p_buf, a_buf = bufs[slot]
        k0 = pl.multiple_of(j * kt, kt)
        acc_ref[...] = a_buf[...] * acc_ref[...] + _dot(p_buf[...], vs_ref[pl.ds(k0, kt), :])

    n_interior = q0 // kt
    scores(0, 0)

    def interior_run(j0, count):
        for c in range(count):
            scores(j0 + c + 1, (c + 1) % 2)
            softmax(j0 + c, c % 2, causal=False)
            weighted_values(j0 + c, c % 2)

    unroll = NSA_CHUNK_UNROLL
    n_runs = n_interior // unroll

    def interior_body(r, carry):
        interior_run(r * unroll, unroll)
        return carry

    lax.fori_loop(0, n_runs, interior_body, 0)
    for count in (c for c in (4, 2) if c < unroll):
        @pl.when((n_interior % (2 * count)) // count == 1)
        def _(count=count):
            interior_run(n_interior // (2 * count) * (2 * count), count)

    @pl.when(n_interior % 2 == 0)
    def _():
        softmax(n_interior, 0, causal=True)
        weighted_values(n_interior, 0)

    @pl.when(n_interior % 2 == 1)
    def _():
        scores(n_interior, 1)
        softmax(n_interior - 1, 0, causal=False)
        weighted_values(n_interior - 1, 0)
        softmax(n_interior, 1, causal=True)
        weighted_values(n_interior, 1)

    o_slc = acc_ref[...] * (1.0 / l_ref[...])

    gates = g_ref[...]
    for h in range(grp):
        rows = slice(h * tq, (h + 1) * tq)
        cols = slice(h * HEAD_DIM, (h + 1) * HEAD_DIM)
        out = (gates[:, 3 * h:3 * h + 1] * ocmp_ref[rows, :]
               + gates[:, 3 * h + 1:3 * h + 2] * o_slc[rows]
               + gates[:, 3 * h + 2:3 * h + 3] * ow_ref[:, cols].astype(F32))
        o_ref[:, cols] = out.astype(o_ref.dtype)


def _sel_from_cmp(n_cmp_pad, n_blk_pad):
    pos = np.arange(n_cmp_pad)[:, None] * CMP_STRIDE + np.arange(CMP_BLOCK)[None, :]
    owner = pos // SEL_BLOCK
    frac = (owner[:, :, None] == np.arange(n_blk_pad)[None, None, :]).sum(axis=1) / CMP_BLOCK
    return jnp.asarray(frac.T, dtype=BF16)


def _nsa_attention(proj, kc, vc, o_win, gates, *, q_idx, ks_idx, vs_idx, tq=256, kt=512):
    bsz, seq, _ = proj.shape
    tq = min(tq, seq)
    kt = min(kt, seq)
    assert kt % tq == 0 and (HEAD_DIM * SEL_BLOCK) % kt == 0 and seq % kt == 0
    n_cmp = kc.shape[1]
    n_blk = seq // SEL_BLOCK
    n_sel = min(N_SELECT, n_blk)
    n_halves = -(-n_blk // HEAD_DIM)
    wq = NSA_GROUP * HEAD_DIM
    rows = NSA_GROUP * tq
    c2s = _sel_from_cmp(n_cmp, n_halves * HEAD_DIM)
    n_var = max(1, min(NSA_PREFIX_VARIANTS, n_cmp // (2 * HEAD_DIM)))
    assert seq % n_var == 0 and (seq // n_var) % tq == 0 and (n_halves * HEAD_DIM) % (8 * n_var) == 0
    kern = functools.partial(_nsa_kernel, tq=tq, kt=kt, n_blk=n_blk, n_sel=n_sel, n_var=n_var)
    full = lambda idx_fn: pl.BlockSpec((None, seq, HEAD_DIM), lambda b, g, i: (b, 0, idx_fn(g)))
    cmp_spec = pl.BlockSpec((None, n_cmp, HEAD_DIM), lambda b, g, i: (b * NSA_KV_HEADS + g, 0, 0))
    return pl.pallas_call(
        kern,
        grid=(bsz, NSA_KV_HEADS, seq // tq),
        in_specs=[
            pl.BlockSpec((None, tq, wq), lambda b, g, i: (b, i, q_idx(g))),
            cmp_spec, cmp_spec,
            pl.BlockSpec(c2s.shape, lambda b, g, i: (0, 0)),
            full(ks_idx), full(vs_idx),
            pl.BlockSpec((None, tq, wq), lambda b, g, i: (b, i, g)),
            pl.BlockSpec((None, tq, HEAD_DIM), lambda b, g, i: (b, i, g)),
        ],
        out_specs=pl.BlockSpec((None, tq, wq), lambda b, g, i: (b, i, g)),
        out_shape=jax.ShapeDtypeStruct((bsz, seq, NSA_KV_HEADS * wq), BF16),
        scratch_shapes=[pltpu.VMEM((seq, 2 * HEAD_DIM), BF16),
                        pltpu.VMEM((n_halves, rows, 2 * HEAD_DIM), BF16),
                        pltpu.VMEM((rows, kt), F32), pltpu.VMEM((rows, kt), F32),
                        pltpu.VMEM((rows, kt), BF16), pltpu.VMEM((rows, kt), BF16)]
                       + [pltpu.VMEM((rows, HEAD_DIM), F32)] * 6,
        compiler_params=_params("parallel", "parallel", "arbitrary"),
        name="nsa_core",
    )(proj, kc, vc, c2s, proj, proj, o_win, gates)


_BLK_AQ, _BLK_BQ = 0, NSA_HEADS
_BLK_AK = NSA_HEADS + SWA_HEADS
_BLK_BK = _BLK_AK + 3 * NSA_KV_HEADS
_BLK_AV = _BLK_BK + SWA_KV_HEADS
_BLK_BV = _BLK_AV + 3 * NSA_KV_HEADS


def _even_weights(w_in):
    aq, ak, av, ag, bq, bk, bv = jnp.split(w_in, np.cumsum(EVEN_WIDTHS)[:-1].tolist(), axis=-1)
    w_main = jnp.concatenate([aq, bq, ak, bk, av, bv], axis=-1).astype(BF16)
    ag = ag.reshape(D_MODEL, NSA_KV_HEADS, 3 * NSA_GROUP)
    w_gate = jnp.pad(ag, ((0, 0), (0, 0), (0, HEAD_DIM - 3 * NSA_GROUP))).reshape(D_MODEL, NSA_KV_HEADS * HEAD_DIM)
    return w_main, w_gate.astype(BF16)


def _tile_modes(blocks_per_tile, block_modes):
    modes = np.asarray(block_modes, np.int32).reshape(-1, blocks_per_tile)
    assert (modes == modes[:, :1]).all(), "a projection tile must not straddle q/k/v column groups"
    return jnp.asarray(modes[:, 0])


def _even_mixer(xf, xb, cos, sin, bsz, seq, w_in, w_o, pe_k, pe_v, ck_w1, ck_w2, cv_w1, cv_w2, sinks, g, b):
    w_main, w_gate = _even_weights(w_in)
    tn = 1024
    n_q, n_k = NSA_HEADS + SWA_HEADS, 3 * NSA_KV_HEADS + SWA_KV_HEADS
    modes = _tile_modes(tn // HEAD_DIM, [MODE_ROPE_SCALED] * n_q + [MODE_ROPE] * n_k + [MODE_PLAIN] * n_k)
    proj4 = _project(xb, w_main, modes, cos, sin, seq=seq, tn=tn)
    proj = proj4.reshape(bsz, seq, EVEN_PROJ)
    gates = _gate_project(xb, w_gate).reshape(bsz, seq, NSA_KV_HEADS * HEAD_DIM)

    kc = _compress(proj4, _BLK_AK, pe_k, ck_w1, ck_w2)
    vc = _compress(proj4, _BLK_AV, pe_v, cv_w1, cv_w2)
    o_win = _banded_attention(
        proj4, n_steps=NSA_KV_HEADS, hq=NSA_GROUP, grp=NSA_GROUP, window=NSA_WINDOW,
        q_at=lambda s: (0, s), k_at=lambda s: (0, _BLK_AK + 2 * NSA_KV_HEADS + s),
        v_at=lambda s: (0, _BLK_AV + 2 * NSA_KV_HEADS + s), o_at=lambda s: (0, s), o_dims=(1, NSA_Q),
        tq=512, pw=512).reshape(bsz, seq, NSA_Q)
    o_a = _nsa_attention(proj, kc, vc, o_win, gates, q_idx=lambda s: s,
                         ks_idx=lambda s: _BLK_AK + NSA_KV_HEADS + s, vs_idx=lambda s: _BLK_AV + NSA_KV_HEADS + s)
    swa_grp = SWA_HEADS // SWA_KV_HEADS
    o_b = _banded_attention(
        proj4, n_steps=SWA_KV_HEADS, hq=swa_grp, grp=swa_grp, window=SWA_WINDOW,
        q_at=lambda s: (0, NSA_HEADS // swa_grp + s), k_at=lambda s: (0, _BLK_BK + s),
        v_at=lambda s: (0, _BLK_BV + s), o_at=lambda s: (0, s), o_dims=(1, SWA_Q),
        tq=256, pw=128, sinks=sinks)
    m = bsz * seq
    w_o = w_o.astype(BF16)
    return _out_proj_ln([o_a.reshape(m, NSA_Q), o_b.reshape(m, SWA_Q)], [w_o[:NSA_Q], w_o[NSA_Q:]], xf, g, b)


def _odd_mixer(xf, xb, cos, sin, bsz, seq, w_in, w_o, g, b):
    tn = 1024
    modes = _tile_modes(tn // HEAD_DIM, [MODE_ROPE_SCALED] * DIL_HEADS + [MODE_ROPE] * DIL_HEADS + [MODE_PLAIN] * DIL_HEADS)
    m = bsz * seq
    width = DIL_HEADS * HEAD_DIM
    w_in = w_in.astype(BF16)
    outs, lses = [], []
    for grp_i, (window, dil) in enumerate(DIL_PATTERNS):
        qkv = _project(xb, w_in, modes, cos, sin, seq=seq, dil=dil, tn=tn, col0=grp_i * DIL_QKV, n=DIL_QKV)
        o, lse = _banded_attention(
            qkv, n_steps=dil, hq=DIL_HEADS, grp=1, window=window // dil + 1,
            q_at=lambda s: (s, 0), k_at=lambda s: (s, 1), v_at=lambda s: (s, 2), o_at=lambda s: (s, 0),
            o_dims=(dil, width), tq=256, pw=128, want_lse=True)
        outs.append(o)
        lses.append(lse.transpose(0, 2, 1, 3).reshape(m, DIL_HEADS))
    return _mix_out_proj_ln(outs, lses, w_o.astype(BF16), xf, g, b, seq=seq)


def kernel(x, positions, e_w_in, e_w_o, nsa_pe_k, nsa_pe_v, nsa_ck_w1, nsa_ck_w2, nsa_cv_w1, nsa_cv_w2, swa_sinks, o_w_in, o_w_o, ln1_g, ln1_b, mlp_w1, mlp_w2, ln2_g, ln2_b):
    bsz, seq, d = x.shape
    m = bsz * seq
    cos, sin = _rope_tables(positions)
    xf = x.reshape(m, d)
    xb = xf.astype(BF16)
    for layer in range(DEPTH):
        i = layer // 2
        if layer % 2 == 0:
            xf, xb = _even_mixer(xf, xb, cos, sin, bsz, seq, e_w_in[i], e_w_o[i], nsa_pe_k[i], nsa_pe_v[i],
                                 nsa_ck_w1[i], nsa_ck_w2[i], nsa_cv_w1[i], nsa_cv_w2[i], swa_sinks[i],
                                 ln1_g[layer], ln1_b[layer])
        else:
            xf, xb = _odd_mixer(xf, xb, cos, sin, bsz, seq, o_w_in[i], o_w_o[i], ln1_g[layer], ln1_b[layer])
        xf, xb = _mlp_ln(xf, xb, mlp_w1[layer].astype(BF16), mlp_w2[layer].astype(BF16), ln2_g[layer], ln2_b[layer])
    return xf.reshape(bsz, seq, d)
```

```python
import functools
import math

import numpy as np
import jax
import jax.numpy as jnp
from jax import lax
from jax.experimental import pallas as pl
from jax.experimental.pallas import tpu as pltpu

D_MODEL = 2048
DEPTH = 4
HEAD_DIM = 128
ROPE_THETA = 10000.0
LN_EPS = 1e-5

NSA_HEADS = D_MODEL // (2 * HEAD_DIM)
NSA_KV_HEADS = 2
NSA_GROUP = NSA_HEADS // NSA_KV_HEADS
CMP_BLOCK = 32
CMP_STRIDE = 16
SEL_BLOCK = 64
N_SELECT = 16
NSA_WINDOW = 512

SWA_HEADS = D_MODEL // (2 * HEAD_DIM)
SWA_KV_HEADS = 2
SWA_WINDOW = 128

DIL_HEADS = D_MODEL // HEAD_DIM
DIL_PATTERNS = ((128, 1), (512, 4), (2048, 16))

DN_ALPHA = (2 * DEPTH) ** 0.25

NSA_Q = NSA_HEADS * HEAD_DIM
NSA_KV = NSA_KV_HEADS * HEAD_DIM
SWA_Q = SWA_HEADS * HEAD_DIM
SWA_KV = SWA_KV_HEADS * HEAD_DIM
EVEN_WIDTHS = (NSA_Q, 3 * NSA_KV, 3 * NSA_KV, 3 * NSA_HEADS, SWA_Q, SWA_KV, SWA_KV)
EVEN_PROJ = NSA_Q + SWA_Q + 3 * NSA_KV + SWA_KV + 3 * NSA_KV + SWA_KV
DIL_QKV = 3 * DIL_HEADS * HEAD_DIM

LOG2E = math.log2(math.e)
LN2 = math.log(2.0)
QK_SCALE = HEAD_DIM ** -0.5 * LOG2E
NEG = -1e30
SOFTMAX_ROWS = 32
BANDED_SOFTMAX_ROWS = 32
NSA_CHUNK_UNROLL = 4
NSA_PREFIX_VARIANTS = 4
PROJ_SUB_ROWS = 256
MLP_VMEM_LIMIT_BYTES = 62 * 1024 * 1024
VMEM_LIMIT_BYTES = 56 * 1024 * 1024

MODE_PLAIN, MODE_ROPE, MODE_ROPE_SCALED = 0, 1, 2

F32 = jnp.float32
BF16 = jnp.bfloat16


def _params(*semantics):
    return pltpu.CompilerParams(dimension_semantics=semantics, vmem_limit_bytes=VMEM_LIMIT_BYTES)


def _dot(a, b):
    return jnp.dot(a, b, preferred_element_type=F32)


def _dot_nt(a, b):
    return lax.dot_general(a, b, (((1,), (1,)), ((), ())), preferred_element_type=F32)


def _rope_table_kernel(pos_ref, inv_ref, sign_ref, cos_ref, sin_ref):
    ang = pos_ref[...].astype(F32) * inv_ref[...]
    cos_ref[...] = jnp.cos(ang)
    sin_ref[...] = jnp.sin(ang) * sign_ref[...]


def _rope_tables(positions):
    m = positions.size
    tm = min(m, 2048)
    inv = ROPE_THETA ** (-jnp.arange(0, HEAD_DIM, 2, dtype=F32) / HEAD_DIM)
    inv_full = jnp.concatenate([inv, inv]).reshape(1, HEAD_DIM)
    half = HEAD_DIM // 2
    sign = jnp.concatenate([-jnp.ones((half,), F32), jnp.ones((half,), F32)]).reshape(1, HEAD_DIM)
    row = pl.BlockSpec((tm, HEAD_DIM), lambda i: (i, 0))
    const = pl.BlockSpec((1, HEAD_DIM), lambda i: (0, 0))
    return pl.pallas_call(
        _rope_table_kernel,
        grid=(m // tm,),
        in_specs=[pl.BlockSpec((tm, 1), lambda i: (i, 0)), const, const],
        out_specs=[row, row],
        out_shape=[jax.ShapeDtypeStruct((m, HEAD_DIM), F32)] * 2,
        compiler_params=_params("parallel"),
        name="rope_tables",
    )(positions.reshape(m, 1), inv_full, sign)


def _proj_kernel(modes_ref, x_ref, w_ref, cos_ref, sin_ref, o_ref, *scratch, tn, dil):
    mode = modes_ref[pl.program_id(1)]
    tm = x_ref.shape[0]
    sub = min(PROJ_SUB_ROWS, tm)
    blocks = [slice(blk * HEAD_DIM, (blk + 1) * HEAD_DIM) for blk in range(tn // HEAD_DIM)]
    is_rope = mode != MODE_PLAIN
    scale = jnp.where(mode == MODE_ROPE_SCALED, QK_SCALE, 1.0).astype(F32)

    def epilogue(r, acc):
        rows = slice(r * sub, (r + 1) * sub)
        cos = jnp.where(is_rope, cos_ref[rows, :] * scale, 1.0)
        sin = jnp.where(is_rope, sin_ref[rows, :] * scale, 0.0)
        for blk, cols in enumerate(blocks):
            t = acc[:, cols]
            val = t * cos + pltpu.roll(t, HEAD_DIM // 2, 1) * sin
            if dil == 1:
                o_ref[rows, cols] = val.astype(o_ref.dtype)
            else:
                scratch[0][blk, rows, :] = val
        if dil > 1:
            n = sub // dil
            for c in range(dil):
                for blk, cols in enumerate(blocks):
                    o_ref[c, r * n:(r + 1) * n, cols] = scratch[0][
                        blk, pl.ds(r * sub + c, n, stride=dil), :].astype(o_ref.dtype)

    pending = None
    for r in range(tm // sub):
        acc = _dot(x_ref[r * sub:(r + 1) * sub, :], w_ref[...])
        if pending is not None:
            epilogue(*pending)
        pending = (r, acc)
    epilogue(*pending)


def _project(x, w, modes, cos, sin, *, seq, dil=1, tm=1024, tn=1024, col0=0, n=None):
    m, k = x.shape
    n = w.shape[1] if n is None else n
    assert col0 % tn == 0 and n % tn == 0
    tile0 = col0 // tn
    tm = min(tm, seq)
    tiles_per_batch = seq // tm
    bsz = m // seq
    if dil == 1:
        out_spec = pl.BlockSpec((tm, tn), lambda i, j, modes: (i, j))
        out_shape = jax.ShapeDtypeStruct((m, n), BF16)
        scratch = []
    else:
        out_spec = pl.BlockSpec((None, dil, tm // dil, tn),
                                lambda i, j, modes: (i // tiles_per_batch, 0, i % tiles_per_batch, j))
        out_shape = jax.ShapeDtypeStruct((bsz, dil, seq // dil, n), BF16)
        scratch = [pltpu.VMEM((tn // HEAD_DIM, tm, HEAD_DIM), F32)]
    grid_spec = pltpu.PrefetchScalarGridSpec(
        num_scalar_prefetch=1,
        grid=(m // tm, n // tn),
        in_specs=[
            pl.BlockSpec((tm, k), lambda i, j, modes: (i, 0)),
            pl.BlockSpec((k, tn), lambda i, j, modes: (0, tile0 + j)),
            pl.BlockSpec((tm, HEAD_DIM), lambda i, j, modes: (i, 0)),
            pl.BlockSpec((tm, HEAD_DIM), lambda i, j, modes: (i, 0)),
        ],
        out_specs=out_spec,
        scratch_shapes=scratch,
    )
    out = pl.pallas_call(
        functools.partial(_proj_kernel, tn=tn, dil=dil),
        grid_spec=grid_spec,
        out_shape=out_shape,
        compiler_params=_params("parallel", "arbitrary"),
        name="in_proj" if dil == 1 else "in_proj_dil%d" % dil,
    )(modes, x, w, cos, sin)
    return out.reshape(bsz, 1, seq, n) if dil == 1 else out


def _gate_kernel(x_ref, w_ref, o_ref):
    o_ref[...] = jax.nn.sigmoid(_dot(x_ref[...], w_ref[...]))


def _gate_project(x, w, *, tm=1024):
    m, k = x.shape
    n = w.shape[1]
    tm = min(tm, m)
    return pl.pallas_call(
        _gate_kernel,
        grid=(m // tm,),
        in_specs=[pl.BlockSpec((tm, k), lambda i: (i, 0)), pl.BlockSpec((k, n), lambda i: (0, 0))],
        out_specs=pl.BlockSpec((tm, n), lambda i: (i, 0)),
        out_shape=jax.ShapeDtypeStruct((m, n), F32),
        compiler_params=_params("parallel"),
        name="gate_proj",
    )(x, w)


def _residual_layer_norm(res, y, g, b):
    z = DN_ALPHA * res + y
    mu = jnp.mean(z, axis=-1, keepdims=True)
    zc = z - mu
    var = jnp.mean(zc * zc, axis=-1, keepdims=True)
    return zc * lax.rsqrt(var + LN_EPS) * g + b


def _out_proj_kernel(*refs, n_in):
    x_refs, w_refs = refs[:n_in], refs[n_in:2 * n_in]
    res_ref, g_ref, b_ref, of_ref, ob_ref = refs[2 * n_in:]
    tm = res_ref.shape[0]
    sub = min(PROJ_SUB_ROWS, tm)

    def finish(r, y):
        rows = slice(r * sub, (r + 1) * sub)
        out = _residual_layer_norm(res_ref[rows, :], y, g_ref[...], b_ref[...])
        of_ref[rows, :] = out
        ob_ref[rows, :] = out.astype(BF16)

    pending = None
    for r in range(tm // sub):
        rows = slice(r * sub, (r + 1) * sub)
        y = _dot(x_refs[0][rows, :], w_refs[0][...])
        for x_ref, w_ref in zip(x_refs[1:], w_refs[1:]):
            y += _dot(x_ref[rows, :], w_ref[...])
        if pending is not None:
            finish(*pending)
        pending = (r, y)
    finish(*pending)


def _out_proj_ln(xs, ws, res, g, b, *, tm=512):
    m, d = res.shape
    tm = min(tm, m)
    n_in = len(xs)
    in_specs = [pl.BlockSpec((tm, x.shape[1]), lambda i: (i, 0)) for x in xs]
    in_specs += [pl.BlockSpec(w.shape, lambda i: (0, 0)) for w in ws]
    row = pl.BlockSpec((tm, d), lambda i: (i, 0))
    vec = pl.BlockSpec((1, d), lambda i: (0, 0))
    in_specs += [row, vec, vec]
    return pl.pallas_call(
        functools.partial(_out_proj_kernel, n_in=n_in),
        grid=(m // tm,),
        in_specs=in_specs,
        out_specs=[row, row],
        out_shape=[jax.ShapeDtypeStruct((m, d), F32), jax.ShapeDtypeStruct((m, d), BF16)],
        compiler_params=_params("parallel"),
        name="out_proj_ln",
    )(*xs, *ws, res, g.reshape(1, d), b.reshape(1, d))


def _mlp_kernel(xb_ref, w1_ref, w2_ref, xf_ref, g_ref, b_ref, of_ref, ob_ref):
    acc_ref = of_ref
    f = pl.program_id(1)
    tm = xb_ref.shape[0]
    sub = min(PROJ_SUB_ROWS, tm)
    @pl.when(f == 0)
    def _():
        acc_ref[...] = jnp.zeros(acc_ref.shape, F32)

    def chunk(finish):
        hidden = [None] * (tm // sub)
        for r in range(tm // sub + 1):
            if r < tm // sub:
                hidden[r] = _dot(xb_ref[r * sub:(r + 1) * sub, :], w1_ref[...])
            if r > 0:
                rows = slice((r - 1) * sub, r * sub)
                act = jnp.square(jnp.maximum(hidden[r - 1], 0.0)).astype(BF16)
                total = acc_ref[rows, :] + _dot(act, w2_ref[...])
                if finish:
                    out = _residual_layer_norm(xf_ref[rows, :], total, g_ref[...], b_ref[...])
                    of_ref[rows, :] = out
                    ob_ref[rows, :] = out.astype(BF16)
                else:
                    acc_ref[rows, :] = total

    last = pl.num_programs(1) - 1
    pl.when(f != last)(functools.partial(chunk, False))
    pl.when(f == last)(functools.partial(chunk, True))


def _mlp_ln(xf, xb, w1, w2, g, b, *, tm=1024, tf=512):
    m, d = xf.shape
    ff = w1.shape[1]
    tm = min(tm, m)
    row = pl.BlockSpec((tm, d), lambda i, f: (i, 0))
    vec = pl.BlockSpec((1, d), lambda i, f: (0, 0))
    return pl.pallas_call(
        _mlp_kernel,
        grid=(m // tm, ff // tf),
        in_specs=[row, pl.BlockSpec((d, tf), lambda i, f: (0, f)), pl.BlockSpec((tf, d), lambda i, f: (f, 0)),
                  row, vec, vec],
        out_specs=[row, row],
        out_shape=[jax.ShapeDtypeStruct((m, d), F32), jax.ShapeDtypeStruct((m, d), BF16)],
        compiler_params=pltpu.CompilerParams(dimension_semantics=("parallel", "arbitrary"),
                                             vmem_limit_bytes=MLP_VMEM_LIMIT_BYTES),
        name="mlp_ln",
    )(xb, w1, w2, xf, g.reshape(1, d), b.reshape(1, d))


def _banded_kernel(*refs, hq, grp, window, tq, pw, has_sink, want_lse):
    refs = list(refs)
    sink_ref = refs.pop(0) if has_sink else None
    q_ref, kc_ref, kp_ref, vc_ref, vp_ref, o_ref = refs[:6]
    lse_ref = refs[6] if want_lse else None
    bias_ref = refs[-1]
    score_refs = refs[-1 - 3 * hq:-1 - 2 * hq]
    prob_refs = refs[-1 - 2 * hq:-1 - hq]
    linv_refs = refs[-1 - hq:-1]
    step, i = pl.program_id(1), pl.program_id(2)
    span = pw + tq
    rows_blk = min(BANDED_SOFTMAX_ROWS, tq)

    row = lax.broadcasted_iota(jnp.int32, (tq, span), 0)
    rel = lax.broadcasted_iota(jnp.int32, (tq, span), 1) - pw - row
    mask = (rel <= 0) & (rel > -window) & ((rel + row >= 0) | (i > 0))
    bias_ref[...] = jnp.where(mask, 0.0, NEG)

    @pl.when(i == 0)
    def _():
        for h in range(hq):
            prob_refs[h][...] = jnp.zeros(prob_refs[h].shape, prob_refs[h].dtype)

    for h in range(hq):
        kv = h // grp
        q = q_ref[:, h * HEAD_DIM:(h + 1) * HEAD_DIM]
        kcols = slice(kv * HEAD_DIM, (kv + 1) * HEAD_DIM)
        score_refs[h][:, :pw] = _dot_nt(q, kp_ref[:, kcols])
        score_refs[h][:, pw:] = _dot_nt(q, kc_ref[:, kcols])

    for h in range(hq):
        kv = h // grp
        qcols = slice(h * HEAD_DIM, (h + 1) * HEAD_DIM)
        kcols = slice(kv * HEAD_DIM, (kv + 1) * HEAD_DIM)
        if has_sink:
            sink = sink_ref[step * hq + h] * LOG2E
        for rb in range(tq // rows_blk):
            r0 = rb * rows_blk
            rows = slice(r0, r0 + rows_blk)
            cols = slice(max(0, r0 + pw - (window - 1)) // HEAD_DIM * HEAD_DIM,
                         -(-(r0 + rows_blk + pw) // HEAD_DIM) * HEAD_DIM)
            s = score_refs[h][rows, cols] + bias_ref[rows, cols]
            m = jnp.max(s, axis=-1, keepdims=True)
            if has_sink:
                m = jnp.maximum(m, sink)
            p = jnp.exp2(s - m)
            l = jnp.sum(p, axis=-1, keepdims=True)
            if has_sink:
                l = l + jnp.exp2(sink - m)
            prob_refs[h][rows, cols] = p.astype(BF16)
            linv_refs[h][rows, :] = jnp.broadcast_to(1.0 / l, (rows_blk, HEAD_DIM))
            if want_lse:
                lse_ref[rows, h:h + 1] = (m + jnp.log2(l)) * LN2
        o = _dot(prob_refs[h][:, :pw], vp_ref[:, kcols]) + _dot(prob_refs[h][:, pw:], vc_ref[:, kcols])
        o_ref[:, qcols] = (o * linv_refs[h][...]).astype(o_ref.dtype)


def _banded_attention(arr, *, n_steps, hq, grp, window, q_at, k_at, v_at, o_at, o_dims, tq, pw,
                      sinks=None, want_lse=False):
    bsz, _, seq, _ = arr.shape
    tq = min(tq, seq)
    pw = min(pw, tq)
    assert pw >= window - 1 or seq <= pw, (pw, window, seq)
    assert seq % tq == 0 and tq % pw == 0
    hk = hq // grp
    ratio = tq // pw
    wq, wk = hq * HEAD_DIM, hk * HEAD_DIM
    has_sink = sinks is not None

    def spec(width, rows, at_fn, row_fn):
        def index(b, s, i, *_):
            plane, col = at_fn(s)
            return (b, plane, row_fn(i), col)
        return pl.BlockSpec((None, None, rows, width), index)

    cur = lambda i: i
    prev = lambda i: jnp.maximum(i * ratio - 1, 0)
    in_specs = [spec(wq, tq, q_at, cur), spec(wk, tq, k_at, cur), spec(wk, pw, k_at, prev),
                spec(wk, tq, v_at, cur), spec(wk, pw, v_at, prev)]
    out_specs = [spec(wq, tq, o_at, cur)]
    out_shape = [jax.ShapeDtypeStruct((bsz, o_dims[0], seq, o_dims[1]), BF16)]
    if want_lse:
        out_specs.append(pl.BlockSpec((None, None, tq, hq), lambda b, s, i: (b, s, i, 0)))
        out_shape.append(jax.ShapeDtypeStruct((bsz, n_steps, seq, hq), F32))
    kern = functools.partial(_banded_kernel, hq=hq, grp=grp, window=window, tq=tq, pw=pw,
                             has_sink=has_sink, want_lse=want_lse)
    grid = (bsz, n_steps, seq // tq)
    params = _params("parallel", "parallel", "arbitrary")
    scratch = ([pltpu.VMEM((tq, pw + tq), F32)] * hq + [pltpu.VMEM((tq, pw + tq), BF16)] * hq
               + [pltpu.VMEM((tq, HEAD_DIM), F32)] * hq + [pltpu.VMEM((tq, pw + tq), F32)])
    if has_sink:
        grid_spec = pltpu.PrefetchScalarGridSpec(num_scalar_prefetch=1, grid=grid, in_specs=in_specs,
                                                 out_specs=out_specs, scratch_shapes=scratch)
        out = pl.pallas_call(kern, grid_spec=grid_spec, out_shape=out_shape, compiler_params=params,
                             name="banded_attn_sink")(sinks.astype(F32), arr, arr, arr, arr, arr)
    else:
        out = pl.pallas_call(kern, grid=grid, in_specs=in_specs, out_specs=out_specs, out_shape=out_shape,
                             scratch_shapes=scratch, compiler_params=params, name="banded_attn")(arr, arr, arr, arr, arr)
    return out if want_lse else out[0]


def _mix_out_proj_kernel(o0_ref, o1_ref, o2_ref, l0_ref, l1_ref, l2_ref, w_ref, res_ref, g_ref, b_ref,
                         of_ref, ob_ref, t1_ref, t2_ref, x_ref):
    tm = res_ref.shape[0]
    sub = min(PROJ_SUB_ROWS, tm)

    def mix(r):
        rows = slice(r * sub, (r + 1) * sub)
        for src, dst in ((o1_ref, t1_ref), (o2_ref, t2_ref)):
            dil = src.shape[0]
            n = sub // dil
            for c in range(dil):
                for h in range(DIL_HEADS):
                    dst[h, pl.ds(r * sub + c, n, stride=dil), :] = src[
                        c, r * n:(r + 1) * n, h * HEAD_DIM:(h + 1) * HEAD_DIM].astype(F32)
        l0, l1, l2 = l0_ref[rows, :], l1_ref[rows, :], l2_ref[rows, :]
        m = jnp.maximum(jnp.maximum(l0, l1), l2)
        e0, e1, e2 = jnp.exp(l0 - m), jnp.exp(l1 - m), jnp.exp(l2 - m)
        inv = 1.0 / (e0 + e1 + e2)
        w0, w1, w2 = e0 * inv, e1 * inv, e2 * inv
        for h in range(DIL_HEADS):
            cols = slice(h * HEAD_DIM, (h + 1) * HEAD_DIM)
            mixed = (w0[:, h:h + 1] * o0_ref[rows, cols].astype(F32)
                     + w1[:, h:h + 1] * t1_ref[h, rows, :]
                     + w2[:, h:h + 1] * t2_ref[h, rows, :])
            x_ref[rows, cols] = mixed.astype(x_ref.dtype)

    def finish(r, y):
        rows = slice(r * sub, (r + 1) * sub)
        out = _residual_layer_norm(res_ref[rows, :], y, g_ref[...], b_ref[...])
        of_ref[rows, :] = out
        ob_ref[rows, :] = out.astype(BF16)

    n_sub = tm // sub
    mix(0)
    pending = None
    for r in range(n_sub):
        y = _dot(x_ref[r * sub:(r + 1) * sub, :], w_ref[...])
        if r + 1 < n_sub:
            mix(r + 1)
        if pending is not None:
            finish(*pending)
        pending = (r, y)
    finish(*pending)


def _mix_out_proj_ln(outs, lses, w, res, g, b, *, seq, tm=512):
    bsz, _, _, d = outs[0].shape
    tm = min(tm, seq)
    tiles_per_batch = seq // tm

    def o_spec(dil):
        return pl.BlockSpec((None, dil, tm // dil, d), lambda i: (i // tiles_per_batch, 0, i % tiles_per_batch, 0))

    row = pl.BlockSpec((tm, d), lambda i: (i, 0))
    lrow = pl.BlockSpec((tm, DIL_HEADS), lambda i: (i, 0))
    vec = pl.BlockSpec((1, d), lambda i: (0, 0))
    dils = [o.shape[1] for o in outs]
    assert dils[0] == 1
    return pl.pallas_call(
        _mix_out_proj_kernel,
        grid=(bsz * tiles_per_batch,),
        in_specs=[row, o_spec(dils[1]), o_spec(dils[2]), lrow, lrow, lrow,
                  pl.BlockSpec(w.shape, lambda i: (0, 0), pipeline_mode=pl.Buffered(1)), row, vec, vec],
        out_specs=[row, row],
        out_shape=[jax.ShapeDtypeStruct((bsz * seq, d), F32), jax.ShapeDtypeStruct((bsz * seq, d), BF16)],
        scratch_shapes=[pltpu.VMEM((DIL_HEADS, tm, HEAD_DIM), F32)] * 2 + [pltpu.VMEM((tm, d), BF16)],
        compiler_params=_params("parallel"),
        name="mix_out_proj_ln",
    )(outs[0].reshape(bsz * seq, d), outs[1], outs[2], *lses, w, res, g.reshape(1, d), b.reshape(1, d))


def _compress_kernel(x_ref, pe_ref, w1_ref, w2_ref, o_ref, stage_ref, first_ref, second_ref):
    seq = x_ref.shape[0]
    n = seq // CMP_STRIDE
    stage_ref[...] = x_ref[...].astype(F32)
    for j in range(CMP_STRIDE):
        piece = stage_ref[pl.ds(j, n, stride=CMP_STRIDE), :]
        lo = _dot((piece + pe_ref[j:j + 1, :]).astype(BF16), w1_ref[j * HEAD_DIM:(j + 1) * HEAD_DIM, :])
        hi_j = CMP_STRIDE + j
        hi = _dot((piece + pe_ref[hi_j:hi_j + 1, :]).astype(BF16), w1_ref[hi_j * HEAD_DIM:(hi_j + 1) * HEAD_DIM, :])
        if j == 0:
            first_ref[...] = lo
            second_ref[0:n, :] = hi
        else:
            first_ref[...] += lo
            second_ref[0:n, :] += hi
    second_ref[n:n + 8, :] = jnp.zeros((8, HEAD_DIM), F32)
    h = first_ref[...] + second_ref[1:n + 1, :]
    o_ref[...] = _dot(jax.nn.gelu(h).astype(BF16), w2_ref[...]).astype(o_ref.dtype)


def _compress(proj4, blk0, pe, w1, w2):
    bsz, _, seq, _ = proj4.shape
    n = seq // CMP_STRIDE
    const = lambda shape: pl.BlockSpec(shape, lambda b, g: (0, 0))
    return pl.pallas_call(
        _compress_kernel,
        grid=(bsz, NSA_KV_HEADS),
        in_specs=[pl.BlockSpec((None, None, seq, HEAD_DIM), lambda b, g: (b, 0, 0, blk0 + g)),
                  const((CMP_BLOCK, HEAD_DIM)), const((CMP_BLOCK * HEAD_DIM, HEAD_DIM)), const((HEAD_DIM, HEAD_DIM))],
        out_specs=pl.BlockSpec((None, n, HEAD_DIM), lambda b, g: (b * NSA_KV_HEADS + g, 0, 0)),
        out_shape=jax.ShapeDtypeStruct((bsz * NSA_KV_HEADS, n, HEAD_DIM), BF16),
        scratch_shapes=[pltpu.VMEM((seq, HEAD_DIM), F32), pltpu.VMEM((n, HEAD_DIM), F32),
                        pltpu.VMEM((n + 8, HEAD_DIM), F32)],
        compiler_params=_params("parallel", "parallel"),
        name="nsa_compress",
    )(proj4, pe, w1.astype(BF16), w2.astype(BF16))


def _nsa_kernel(q_ref, kc_ref, vc_ref, c2s_ref, ks_ref, vs_ref, ow_ref, g_ref, o_ref,
                kaug_ref, qaug_ref, s0_ref, s1_ref, p0_ref, p1_ref, a0_ref, a1_ref, m_ref, l_ref, acc_ref, ocmp_ref,
                *, tq, kt, n_blk, n_sel, n_var):
    grp = NSA_GROUP
    n_cmp = kc_ref.shape[0]
    seq = ks_ref.shape[0]
    n_halves = qaug_ref.shape[0]
    chunks_per_half = HEAD_DIM * SEL_BLOCK // kt
    q0 = pl.program_id(2) * tq

    @pl.when(pl.program_id(2) == 0)
    def _():
        def fill(r, carry):
            r0 = pl.multiple_of(r * kt, kt)
            blk_of_row = (lax.broadcasted_iota(jnp.int32, (kt, HEAD_DIM), 0) + r0) // SEL_BLOCK
            lane = lax.broadcasted_iota(jnp.int32, (kt, HEAD_DIM), 1)
            kaug_ref[pl.ds(r0, kt), 0:HEAD_DIM] = ks_ref[pl.ds(r0, kt), :]
            kaug_ref[pl.ds(r0, kt), HEAD_DIM:2 * HEAD_DIM] = jnp.where(
                lane == blk_of_row % HEAD_DIM, NEG, 0.0).astype(kaug_ref.dtype)
            return carry
        lax.fori_loop(0, seq // kt, fill, 0)

    qs = jnp.concatenate([q_ref[:, h * HEAD_DIM:(h + 1) * HEAD_DIM] for h in range(grp)], axis=0)
    qpos = q0 + lax.broadcasted_iota(jnp.int32, (tq, 1), 0)

    def select_blocks(n_vis, n_rows):
        cmp_end = lax.broadcasted_iota(jnp.int32, (tq, n_vis), 1) * CMP_STRIDE + (CMP_BLOCK - 1)
        cmp_bias = jnp.where(cmp_end <= qpos, 0.0, NEG)
        s = _dot_nt(qs, kc_ref[0:n_vis, :]).reshape(grp, tq, n_vis) + cmp_bias[None]
        m = jnp.max(s, axis=-1, keepdims=True)
        m = jnp.where(m < 0.5 * NEG, 0.0, m)
        p = jnp.exp2(s - m)
        l = jnp.sum(p, axis=-1, keepdims=True)
        p = p * (1.0 / jnp.where(l > 0.0, l, 1.0))
        ocmp_ref[...] = _dot(p.reshape(grp * tq, n_vis).astype(BF16), vc_ref[0:n_vis, :])

        p_sum = jnp.sum(p, axis=0)
        p_hi = p_sum.astype(BF16)
        p_lo = (p_sum - p_hi.astype(F32)).astype(BF16)
        c2s = c2s_ref[0:n_rows, 0:n_vis]
        imp = _dot_nt(c2s, p_hi) + _dot_nt(c2s, p_lo)

        blk = lax.broadcasted_iota(jnp.int32, (n_rows, tq), 0)
        qpos_lane = q0 + lax.broadcasted_iota(jnp.int32, (1, tq), 1)
        cur = qpos_lane // SEL_BLOCK
        forced = (blk == 0) | (blk == cur) | (blk == cur - 1)
        score = jnp.where(forced, jnp.inf, jnp.where(blk * SEL_BLOCK <= qpos_lane, imp, -1.0))
        blk_f = blk.astype(F32)
        not_sel = jnp.ones((n_rows, tq), F32)
        for _ in range(n_sel):
            best = jnp.max(score, axis=0, keepdims=True)
            first = jnp.min(jnp.where(score == best, blk_f, float(n_rows)), axis=0, keepdims=True)
            hit = blk_f == first
            not_sel = jnp.where(hit, 0.0, not_sel)
            score = jnp.where(hit, -2.0, score)
        if n_rows < n_pad:
            not_sel = jnp.concatenate([not_sel, jnp.ones((n_pad - n_rows, tq), F32)], axis=0)
        not_sel = not_sel.T
        for half in range(n_halves):
            flags = not_sel[:, half * HEAD_DIM:(half + 1) * HEAD_DIM].astype(qaug_ref.dtype)
            qaug_ref[half, :, 0:HEAD_DIM] = qs
            qaug_ref[half, :, HEAD_DIM:2 * HEAD_DIM] = jnp.concatenate([flags] * grp, axis=0)

    n_pad = c2s_ref.shape[0]
    for v in range(n_var):
        @pl.when(q0 // (seq // n_var) == v)
        def _(v=v):
            select_blocks((v + 1) * (n_cmp // n_var), (v + 1) * (n_pad // n_var))

    rows_blk = SOFTMAX_ROWS
    bufs = ((s0_ref, p0_ref, a0_ref), (s1_ref, p1_ref, a1_ref))
    m_ref[...] = jnp.full(m_ref.shape, NEG, F32)
    l_ref[...] = jnp.zeros(l_ref.shape, F32)
    acc_ref[...] = jnp.zeros(acc_ref.shape, F32)

    def scores(j, slot):
        k0 = pl.multiple_of(j * kt, kt)
        bufs[slot][0][...] = _dot_nt(qaug_ref[j // chunks_per_half], kaug_ref[pl.ds(k0, kt), :])

    def softmax(j, slot, causal):
        s_buf, p_buf, a_buf = bufs[slot]
        k0 = j * kt
        for rb in range(grp * tq // rows_blk):
            rows = slice(rb * rows_blk, (rb + 1) * rows_blk)
            sb = s_buf[rows, :]
            if causal:
                row_pos = q0 + (rb * rows_blk) % tq + lax.broadcasted_iota(jnp.int32, (rows_blk, 1), 0)
                key_pos = k0 + lax.broadcasted_iota(jnp.int32, (rows_blk, kt), 1)
                sb = jnp.where(key_pos <= row_pos, sb, NEG)
            m_prev = m_ref[rows, :]
            m_new = jnp.maximum(m_prev, jnp.max(sb, axis=-1, keepdims=True))
            alpha = jnp.exp2(m_prev - m_new)
            p = jnp.exp2(sb - jnp.concatenate([m_new] * (kt // HEAD_DIM), axis=1))
            l_ref[rows, :] = alpha * l_ref[rows, :] + jnp.sum(p, axis=-1, keepdims=True)
            m_ref[rows, :] = m_new
            a_buf[rows, :] = alpha
            p_buf[rows, :] = p.astype(p_buf.dtype)

    def weighted_values(j, slot):
        _, p_buf, a_buf = bufs[slot]
        k0 = pl.multiple_of(j * kt, kt)
        acc_ref[...] = a_buf[...] * acc_ref[...] + _dot(p_buf[...], vs_ref[pl.ds(k0, kt), :])

    n_interior = q0 // kt
    scores(0, 0)

    def interior_run(j0, count):
        for c in range(count):
            scores(j0 + c + 1, (c + 1) % 2)
            softmax(j0 + c, c % 2, causal=False)
            weighted_values(j0 + c, c % 2)

    unroll = NSA_CHUNK_UNROLL
    n_runs = n_interior // unroll

    def interior_body(r, carry):
        interior_run(r * unroll, unroll)
        return carry

    lax.fori_loop(0, n_runs, interior_body, 0)
    for count in (c for c in (4, 2) if c < unroll):
        @pl.when((n_interior % (2 * count)) // count == 1)
        def _(count=count):
            interior_run(n_interior // (2 * count) * (2 * count), count)

    @pl.when(n_interior % 2 == 0)
    def _():
        softmax(n_interior, 0, causal=True)
        weighted_values(n_interior, 0)

    @pl.when(n_interior % 2 == 1)
    def _():
        scores(n_interior, 1)
        softmax(n_interior - 1, 0, causal=False)
        weighted_values(n_interior - 1, 0)
        softmax(n_interior, 1, causal=True)
        weighted_values(n_interior, 1)

    o_slc = acc_ref[...] * (1.0 / l_ref[...])

    gates = g_ref[...]
    for h in range(grp):
        rows = slice(h * tq, (h + 1) * tq)
        cols = slice(h * HEAD_DIM, (h + 1) * HEAD_DIM)
        out = (gates[:, 3 * h:3 * h + 1] * ocmp_ref[rows, :]
               + gates[:, 3 * h + 1:3 * h + 2] * o_slc[rows]
               + gates[:, 3 * h + 2:3 * h + 3] * ow_ref[:, cols].astype(F32))
        o_ref[:, cols] = out.astype(o_ref.dtype)


def _sel_from_cmp(n_cmp_pad, n_blk_pad):
    pos = np.arange(n_cmp_pad)[:, None] * CMP_STRIDE + np.arange(CMP_BLOCK)[None, :]
    owner = pos // SEL_BLOCK
    frac = (owner[:, :, None] == np.arange(n_blk_pad)[None, None, :]).sum(axis=1) / CMP_BLOCK
    return jnp.asarray(frac.T, dtype=BF16)


def _nsa_attention(proj, kc, vc, o_win, gates, *, q_idx, ks_idx, vs_idx, tq=256, kt=512):
    bsz, seq, _ = proj.shape
    tq = min(tq, seq)
    kt = min(kt, seq)
    assert kt % tq == 0 and (HEAD_DIM * SEL_BLOCK) % kt == 0 and seq % kt == 0
    n_cmp = kc.shape[1]
    n_blk = seq // SEL_BLOCK
    n_sel = min(N_SELECT, n_blk)
    n_halves = -(-n_blk // HEAD_DIM)
    wq = NSA_GROUP * HEAD_DIM
    rows = NSA_GROUP * tq
    c2s = _sel_from_cmp(n_cmp, n_halves * HEAD_DIM)
    n_var = max(1, min(NSA_PREFIX_VARIANTS, n_cmp // (2 * HEAD_DIM)))
    assert seq % n_var == 0 and (seq // n_var) % tq == 0 and (n_halves * HEAD_DIM) % (8 * n_var) == 0
    kern = functools.partial(_nsa_kernel, tq=tq, kt=kt, n_blk=n_blk, n_sel=n_sel, n_var=n_var)
    full = lambda idx_fn: pl.BlockSpec((None, seq, HEAD_DIM), lambda b, g, i: (b, 0, idx_fn(g)))
    cmp_spec = pl.BlockSpec((None, n_cmp, HEAD_DIM), lambda b, g, i: (b * NSA_KV_HEADS + g, 0, 0))
    return pl.pallas_call(
        kern,
        grid=(bsz, NSA_KV_HEADS, seq // tq),
        in_specs=[
            pl.BlockSpec((None, tq, wq), lambda b, g, i: (b, i, q_idx(g))),
            cmp_spec, cmp_spec,
            pl.BlockSpec(c2s.shape, lambda b, g, i: (0, 0)),
            full(ks_idx), full(vs_idx),
            pl.BlockSpec((None, tq, wq), lambda b, g, i: (b, i, g)),
            pl.BlockSpec((None, tq, HEAD_DIM), lambda b, g, i: (b, i, g)),
        ],
        out_specs=pl.BlockSpec((None, tq, wq), lambda b, g, i: (b, i, g)),
        out_shape=jax.ShapeDtypeStruct((bsz, seq, NSA_KV_HEADS * wq), BF16),
        scratch_shapes=[pltpu.VMEM((seq, 2 * HEAD_DIM), BF16),
                        pltpu.VMEM((n_halves, rows, 2 * HEAD_DIM), BF16),
                        pltpu.VMEM((rows, kt), F32), pltpu.VMEM((rows, kt), F32),
                        pltpu.VMEM((rows, kt), BF16), pltpu.VMEM((rows, kt), BF16)]
                       + [pltpu.VMEM((rows, HEAD_DIM), F32)] * 6,
        compiler_params=_params("parallel", "parallel", "arbitrary"),
        name="nsa_core",
    )(proj, kc, vc, c2s, proj, proj, o_win, gates)


_BLK_AK = NSA_HEADS + SWA_HEADS
_BLK_BK = _BLK_AK + 3 * NSA_KV_HEADS
_BLK_AV = _BLK_BK + SWA_KV_HEADS
_BLK_BV = _BLK_AV + 3 * NSA_KV_HEADS


def _even_weights(w_in):
    aq, ak, av, ag, bq, bk, bv = jnp.split(w_in, np.cumsum(EVEN_WIDTHS)[:-1].tolist(), axis=-1)
    w_main = jnp.concatenate([aq, bq, ak, bk, av, bv], axis=-1).astype(BF16)
    ag = ag.reshape(D_MODEL, NSA_KV_HEADS, 3 * NSA_GROUP)
    w_gate = jnp.pad(ag, ((0, 0), (0, 0), (0, HEAD_DIM - 3 * NSA_GROUP))).reshape(D_MODEL, NSA_KV_HEADS * HEAD_DIM)
    return w_main, w_gate.astype(BF16)


def _tile_modes(blocks_per_tile, block_modes):
    modes = np.asarray(block_modes, np.int32).reshape(-1, blocks_per_tile)
    assert (modes == modes[:, :1]).all(), "a projection tile must not straddle q/k/v column groups"
    return jnp.asarray(modes[:, 0])


def _even_mixer(xf, xb, cos, sin, bsz, seq, w_in, w_o, pe_k, pe_v, ck_w1, ck_w2, cv_w1, cv_w2, sinks, g, b):
    w_main, w_gate = _even_weights(w_in)
    tn = 1024
    n_q, n_k = NSA_HEADS + SWA_HEADS, 3 * NSA_KV_HEADS + SWA_KV_HEADS
    modes = _tile_modes(tn // HEAD_DIM, [MODE_ROPE_SCALED] * n_q + [MODE_ROPE] * n_k + [MODE_PLAIN] * n_k)
    proj4 = _project(xb, w_main, modes, cos, sin, seq=seq, tn=tn)
    proj = proj4.reshape(bsz, seq, EVEN_PROJ)
    gates = _gate_project(xb, w_gate).reshape(bsz, seq, NSA_KV_HEADS * HEAD_DIM)

    kc = _compress(proj4, _BLK_AK, pe_k, ck_w1, ck_w2)
    vc = _compress(proj4, _BLK_AV, pe_v, cv_w1, cv_w2)
    o_win = _banded_attention(
        proj4, n_steps=NSA_KV_HEADS, hq=NSA_GROUP, grp=NSA_GROUP, window=NSA_WINDOW,
        q_at=lambda s: (0, s), k_at=lambda s: (0, _BLK_AK + 2 * NSA_KV_HEADS + s),
        v_at=lambda s: (0, _BLK_AV + 2 * NSA_KV_HEADS + s), o_at=lambda s: (0, s), o_dims=(1, NSA_Q),
        tq=512, pw=512).reshape(bsz, seq, NSA_Q)
    o_a = _nsa_attention(proj, kc, vc, o_win, gates, q_idx=lambda s: s,
                         ks_idx=lambda s: _BLK_AK + NSA_KV_HEADS + s, vs_idx=lambda s: _BLK_AV + NSA_KV_HEADS + s)
    swa_grp = SWA_HEADS // SWA_KV_HEADS
    o_b = _banded_attention(
        proj4, n_steps=SWA_KV_HEADS, hq=swa_grp, grp=swa_grp, window=SWA_WINDOW,
        q_at=lambda s: (0, NSA_HEADS // swa_grp + s), k_at=lambda s: (0, _BLK_BK + s),
        v_at=lambda s: (0, _BLK_BV + s), o_at=lambda s: (0, s), o_dims=(1, SWA_Q),
        tq=256, pw=128, sinks=sinks)
    m = bsz * seq
    w_o = w_o.astype(BF16)
    return _out_proj_ln([o_a.reshape(m, NSA_Q), o_b.reshape(m, SWA_Q)], [w_o[:NSA_Q], w_o[NSA_Q:]], xf, g, b)


def _odd_mixer(xf, xb, cos, sin, bsz, seq, w_in, w_o, g, b):
    tn = 1024
    modes = _tile_modes(tn // HEAD_DIM, [MODE_ROPE_SCALED] * DIL_HEADS + [MODE_ROPE] * DIL_HEADS + [MODE_PLAIN] * DIL_HEADS)
    m = bsz * seq
    width = DIL_HEADS * HEAD_DIM
    w_in = w_in.astype(BF16)
    outs, lses = [], []
    for grp_i, (window, dil) in enumerate(DIL_PATTERNS):
        qkv = _project(xb, w_in, modes, cos, sin, seq=seq, dil=dil, tn=tn, col0=grp_i * DIL_QKV, n=DIL_QKV)
        o, lse = _banded_attention(
            qkv, n_steps=dil, hq=DIL_HEADS, grp=1, window=window // dil + 1,
            q_at=lambda s: (s, 0), k_at=lambda s: (s, 1), v_at=lambda s: (s, 2), o_at=lambda s: (s, 0),
            o_dims=(dil, width), tq=256, pw=128, want_lse=True)
        outs.append(o)
        lses.append(lse.transpose(0, 2, 1, 3).reshape(m, DIL_HEADS))
    return _mix_out_proj_ln(outs, lses, w_o.astype(BF16), xf, g, b, seq=seq)


def kernel(x, positions, e_w_in, e_w_o, nsa_pe_k, nsa_pe_v, nsa_ck_w1, nsa_ck_w2, nsa_cv_w1, nsa_cv_w2, swa_sinks, o_w_in, o_w_o, ln1_g, ln1_b, mlp_w1, mlp_w2, ln2_g, ln2_b):
    bsz, seq, d = x.shape
    m = bsz * seq
    cos, sin = _rope_tables(positions)
    xf = x.reshape(m, d)
    xb = xf.astype(BF16)
    for layer in range(DEPTH):
        i = layer // 2
        if layer % 2 == 0:
            xf, xb = _even_mixer(xf, xb, cos, sin, bsz, seq, e_w_in[i], e_w_o[i], nsa_pe_k[i], nsa_pe_v[i],
                                 nsa_ck_w1[i], nsa_ck_w2[i], nsa_cv_w1[i], nsa_cv_w2[i], swa_sinks[i],
                                 ln1_g[layer], ln1_b[layer])
        else:
            xf, xb = _odd_mixer(xf, xb, cos, sin, bsz, seq, o_w_in[i], o_w_o[i], ln1_g[layer], ln1_b[layer])
        xf, xb = _mlp_ln(xf, xb, mlp_w1[layer].astype(BF16), mlp_w2[layer].astype(BF16), ln2_g[layer], ln2_b[layer])
    return xf.reshape(bsz, seq, d)
```

```python
import functools
import math

import numpy as np
import jax
import jax.numpy as jnp
from jax import lax
from jax.experimental import pallas as pl
from jax.experimental.pallas import tpu as pltpu

D_MODEL = 2048
DEPTH = 4
HEAD_DIM = 128
ROPE_THETA = 10000.0
LN_EPS = 1e-5

NSA_HEADS = D_MODEL // (2 * HEAD_DIM)
NSA_KV_HEADS = 2
NSA_GROUP = NSA_HEADS // NSA_KV_HEADS
CMP_BLOCK = 32
CMP_STRIDE = 16
SEL_BLOCK = 64
N_SELECT = 16
NSA_WINDOW = 512

SWA_HEADS = D_MODEL // (2 * HEAD_DIM)
SWA_KV_HEADS = 2
SWA_WINDOW = 128

DIL_HEADS = D_MODEL // HEAD_DIM
DIL_PATTERNS = ((128, 1), (512, 4), (2048, 16))

DN_ALPHA = (2 * DEPTH) ** 0.25

NSA_Q = NSA_HEADS * HEAD_DIM
NSA_KV = NSA_KV_HEADS * HEAD_DIM
SWA_Q = SWA_HEADS * HEAD_DIM
SWA_KV = SWA_KV_HEADS * HEAD_DIM
EVEN_WIDTHS = (NSA_Q, 3 * NSA_KV, 3 * NSA_KV, 3 * NSA_HEADS, SWA_Q, SWA_KV, SWA_KV)
EVEN_PROJ = NSA_Q + SWA_Q + 3 * NSA_KV + SWA_KV + 3 * NSA_KV + SWA_KV
DIL_QKV = 3 * DIL_HEADS * HEAD_DIM

LOG2E = math.log2(math.e)
LN2 = math.log(2.0)
QK_SCALE = HEAD_DIM ** -0.5 * LOG2E
NEG = -1e30
SOFTMAX_ROWS = 32
BANDED_SOFTMAX_ROWS = 32
NSA_CHUNK_UNROLL = 4
NSA_PREFIX_VARIANTS = 4
PROJ_SUB_ROWS = 256
MLP_VMEM_LIMIT_BYTES = 62 * 1024 * 1024
VMEM_LIMIT_BYTES = 56 * 1024 * 1024

MODE_PLAIN, MODE_ROPE, MODE_ROPE_SCALED = 0, 1, 2

F32 = jnp.float32
BF16 = jnp.bfloat16


def _params(*semantics):
    return pltpu.CompilerParams(dimension_semantics=semantics, vmem_limit_bytes=VMEM_LIMIT_BYTES)


def _dot(a, b):
    return jnp.dot(a, b, preferred_element_type=F32)


def _dot_nt(a, b):
    return lax.dot_general(a, b, (((1,), (1,)), ((), ())), preferred_element_type=F32)


def _rope_table_kernel(pos_ref, inv_ref, sign_ref, cos_ref, sin_ref):
    ang = pos_ref[...].astype(F32) * inv_ref[...]
    cos_ref[...] = jnp.cos(ang)
    sin_ref[...] = jnp.sin(ang) * sign_ref[...]


def _rope_tables(positions):
    m = positions.size
    tm = min(m, 2048)
    inv = ROPE_THETA ** (-jnp.arange(0, HEAD_DIM, 2, dtype=F32) / HEAD_DIM)
    inv_full = jnp.concatenate([inv, inv]).reshape(1, HEAD_DIM)
    half = HEAD_DIM // 2
    sign = jnp.concatenate([-jnp.ones((half,), F32), jnp.ones((half,), F32)]).reshape(1, HEAD_DIM)
    row = pl.BlockSpec((tm, HEAD_DIM), lambda i: (i, 0))
    const = pl.BlockSpec((1, HEAD_DIM), lambda i: (0, 0))
    return pl.pallas_call(
        _rope_table_kernel,
        grid=(m // tm,),
        in_specs=[pl.BlockSpec((tm, 1), lambda i: (i, 0)), const, const],
        out_specs=[row, row],
        out_shape=[jax.ShapeDtypeStruct((m, HEAD_DIM), F32)] * 2,
        compiler_params=_params("parallel"),
        name="rope_tables",
    )(positions.reshape(m, 1), inv_full, sign)


def _proj_kernel(modes_ref, x_ref, w_ref, cos_ref, sin_ref, *refs, tn, dil):
    perm_ref = refs[0] if dil > 1 else None
    o_ref = refs[-1]
    mode = modes_ref[pl.program_id(1)]
    tm = x_ref.shape[0]
    sub = min(PROJ_SUB_ROWS, tm)
    blocks = [slice(blk * HEAD_DIM, (blk + 1) * HEAD_DIM) for blk in range(tn // HEAD_DIM)]
    is_rope = mode != MODE_PLAIN
    scale = jnp.where(mode == MODE_ROPE_SCALED, QK_SCALE, 1.0).astype(F32)

    def epilogue(r, acc):
        rows = slice(r * sub, (r + 1) * sub)
        cos = jnp.where(is_rope, cos_ref[rows, :] * scale, 1.0)
        sin = jnp.where(is_rope, sin_ref[rows, :] * scale, 0.0)
        vals = []
        for blk, cols in enumerate(blocks):
            t = acc[:, cols]
            val = (t * cos + pltpu.roll(t, HEAD_DIM // 2, 1) * sin).astype(o_ref.dtype)
            if dil == 1:
                o_ref[rows, cols] = val
            else:
                vals.append(val)
        if dil > 1:
            n = sub // dil
            grouped = _dot(perm_ref[...], jnp.concatenate(vals, axis=1)).astype(o_ref.dtype)
            for c in range(dil):
                o_ref[c, r * n:(r + 1) * n, :] = grouped[c * n:(c + 1) * n, :]

    pending = None
    for r in range(tm // sub):
        acc = _dot(x_ref[r * sub:(r + 1) * sub, :], w_ref[...])
        if pending is not None:
            epilogue(*pending)
        pending = (r, acc)
    epilogue(*pending)


def _project(x, w, modes, cos, sin, *, seq, dil=1, tm=1024, tn=1024, col0=0, n=None):
    m, k = x.shape
    n = w.shape[1] if n is None else n
    assert col0 % tn == 0 and n % tn == 0
    tile0 = col0 // tn
    tm = min(tm, seq)
    tiles_per_batch = seq // tm
    bsz = m // seq
    if dil == 1:
        out_spec = pl.BlockSpec((tm, tn), lambda i, j, modes: (i, j))
        out_shape = jax.ShapeDtypeStruct((m, n), BF16)
        extra_in, extra_specs = [], []
    else:
        out_spec = pl.BlockSpec((None, dil, tm // dil, tn),
                                lambda i, j, modes: (i // tiles_per_batch, 0, i % tiles_per_batch, j))
        out_shape = jax.ShapeDtypeStruct((bsz, dil, seq // dil, n), BF16)
        sub = min(PROJ_SUB_ROWS, tm)
        dst = np.arange(sub)
        perm = np.zeros((sub, sub), np.float32)
        perm[dst, (dst % (sub // dil)) * dil + dst // (sub // dil)] = 1.0
        extra_in = [jnp.asarray(perm, BF16)]
        extra_specs = [pl.BlockSpec((sub, sub), lambda i, j, modes: (0, 0))]
    grid_spec = pltpu.PrefetchScalarGridSpec(
        num_scalar_prefetch=1,
        grid=(m // tm, n // tn),
        in_specs=[
            pl.BlockSpec((tm, k), lambda i, j, modes: (i, 0)),
            pl.BlockSpec((k, tn), lambda i, j, modes: (0, tile0 + j)),
            pl.BlockSpec((tm, HEAD_DIM), lambda i, j, modes: (i, 0)),
            pl.BlockSpec((tm, HEAD_DIM), lambda i, j, modes: (i, 0)),
        ] + extra_specs,
        out_specs=out_spec,
    )
    out = pl.pallas_call(
        functools.partial(_proj_kernel, tn=tn, dil=dil),
        grid_spec=grid_spec,
        out_shape=out_shape,
        compiler_params=_params("parallel", "arbitrary"),
        name="in_proj" if dil == 1 else "in_proj_dil%d" % dil,
    )(modes, x, w, cos, sin, *extra_in)
    return out.reshape(bsz, 1, seq, n) if dil == 1 else out


def _gate_kernel(x_ref, w_ref, o_ref):
    o_ref[...] = jax.nn.sigmoid(_dot(x_ref[...], w_ref[...]))


def _gate_project(x, w, *, tm=1024):
    m, k = x.shape
    n = w.shape[1]
    tm = min(tm, m)
    return pl.pallas_call(
        _gate_kernel,
        grid=(m // tm,),
        in_specs=[pl.BlockSpec((tm, k), lambda i: (i, 0)), pl.BlockSpec((k, n), lambda i: (0, 0))],
        out_specs=pl.BlockSpec((tm, n), lambda i: (i, 0)),
        out_shape=jax.ShapeDtypeStruct((m, n), F32),
        compiler_params=_params("parallel"),
        name="gate_proj",
    )(x, w)


def _residual_layer_norm(res, y, g, b):
    z = DN_ALPHA * res + y
    mu = jnp.mean(z, axis=-1, keepdims=True)
    zc = z - mu
    var = jnp.mean(zc * zc, axis=-1, keepdims=True)
    return zc * lax.rsqrt(var + LN_EPS) * g + b


def _out_proj_kernel(*refs, n_in):
    x_refs, w_refs = refs[:n_in], refs[n_in:2 * n_in]
    res_ref, g_ref, b_ref, of_ref, ob_ref = refs[2 * n_in:]
    tm = res_ref.shape[0]
    sub = min(PROJ_SUB_ROWS, tm)

    def finish(r, y):
        rows = slice(r * sub, (r + 1) * sub)
        out = _residual_layer_norm(res_ref[rows, :], y, g_ref[...], b_ref[...])
        of_ref[rows, :] = out
        ob_ref[rows, :] = out.astype(BF16)

    pending = None
    for r in range(tm // sub):
        rows = slice(r * sub, (r + 1) * sub)
        y = _dot(x_refs[0][rows, :], w_refs[0][...])
        for x_ref, w_ref in zip(x_refs[1:], w_refs[1:]):
            y += _dot(x_ref[rows, :], w_ref[...])
        if pending is not None:
            finish(*pending)
        pending = (r, y)
    finish(*pending)


def _out_proj_ln(xs, ws, res, g, b, *, tm=512):
    m, d = res.shape
    tm = min(tm, m)
    n_in = len(xs)
    in_specs = [pl.BlockSpec((tm, x.shape[1]), lambda i: (i, 0)) for x in xs]
    in_specs += [pl.BlockSpec(w.shape, lambda i: (0, 0)) for w in ws]
    row = pl.BlockSpec((tm, d), lambda i: (i, 0))
    vec = pl.BlockSpec((1, d), lambda i: (0, 0))
    in_specs += [row, vec, vec]
    return pl.pallas_call(
        functools.partial(_out_proj_kernel, n_in=n_in),
        grid=(m // tm,),
        in_specs=in_specs,
        out_specs=[row, row],
        out_shape=[jax.ShapeDtypeStruct((m, d), F32), jax.ShapeDtypeStruct((m, d), BF16)],
        compiler_params=_params("parallel"),
        name="out_proj_ln",
    )(*xs, *ws, res, g.reshape(1, d), b.reshape(1, d))


def _mlp_kernel(xb_ref, w1_ref, w2_ref, xf_ref, g_ref, b_ref, of_ref, ob_ref):
    acc_ref = of_ref
    f = pl.program_id(1)
    tm = xb_ref.shape[0]
    sub = min(PROJ_SUB_ROWS, tm)
    @pl.when(f == 0)
    def _():
        acc_ref[...] = jnp.zeros(acc_ref.shape, F32)

    def chunk(finish):
        hidden = [None] * (tm // sub)
        for r in range(tm // sub + 1):
            if r < tm // sub:
                hidden[r] = _dot(xb_ref[r * sub:(r + 1) * sub, :], w1_ref[...])
            if r > 0:
                rows = slice((r - 1) * sub, r * sub)
                act = jnp.square(jnp.maximum(hidden[r - 1], 0.0)).astype(BF16)
                total = acc_ref[rows, :] + _dot(act, w2_ref[...])
                if finish:
                    out = _residual_layer_norm(xf_ref[rows, :], total, g_ref[...], b_ref[...])
                    of_ref[rows, :] = out
                    ob_ref[rows, :] = out.astype(BF16)
                else:
                    acc_ref[rows, :] = total

    last = pl.num_programs(1) - 1
    pl.when(f != last)(functools.partial(chunk, False))
    pl.when(f == last)(functools.partial(chunk, True))


def _mlp_ln(xf, xb, w1, w2, g, b, *, tm=1024, tf=512):
    m, d = xf.shape
    ff = w1.shape[1]
    tm = min(tm, m)
    row = pl.BlockSpec((tm, d), lambda i, f: (i, 0))
    vec = pl.BlockSpec((1, d), lambda i, f: (0, 0))
    return pl.pallas_call(
        _mlp_kernel,
        grid=(m // tm, ff // tf),
        in_specs=[row, pl.BlockSpec((d, tf), lambda i, f: (0, f)), pl.BlockSpec((tf, d), lambda i, f: (f, 0)),
                  row, vec, vec],
        out_specs=[row, row],
        out_shape=[jax.ShapeDtypeStruct((m, d), F32), jax.ShapeDtypeStruct((m, d), BF16)],
        compiler_params=pltpu.CompilerParams(dimension_semantics=("parallel", "arbitrary"),
                                             vmem_limit_bytes=MLP_VMEM_LIMIT_BYTES),
        name="mlp_ln",
    )(xb, w1, w2, xf, g.reshape(1, d), b.reshape(1, d))


def _banded_kernel(*refs, hq, grp, window, tq, pw, has_sink, want_lse):
    refs = list(refs)
    sink_ref = refs.pop(0) if has_sink else None
    q_ref, kc_ref, kp_ref, vc_ref, vp_ref, o_ref = refs[:6]
    lse_ref = refs[6] if want_lse else None
    bias_ref = refs[-1]
    score_refs = refs[-1 - 3 * hq:-1 - 2 * hq]
    prob_refs = refs[-1 - 2 * hq:-1 - hq]
    linv_refs = refs[-1 - hq:-1]
    step, i = pl.program_id(1), pl.program_id(2)
    span = pw + tq
    rows_blk = min(BANDED_SOFTMAX_ROWS, tq)

    row = lax.broadcasted_iota(jnp.int32, (tq, span), 0)
    rel = lax.broadcasted_iota(jnp.int32, (tq, span), 1) - pw - row
    mask = (rel <= 0) & (rel > -window) & ((rel + row >= 0) | (i > 0))
    bias_ref[...] = jnp.where(mask, 0.0, NEG)

    @pl.when(i == 0)
    def _():
        for h in range(hq):
            prob_refs[h][...] = jnp.zeros(prob_refs[h].shape, prob_refs[h].dtype)

    for h in range(hq):
        kv = h // grp
        q = q_ref[:, h * HEAD_DIM:(h + 1) * HEAD_DIM]
        kcols = slice(kv * HEAD_DIM, (kv + 1) * HEAD_DIM)
        score_refs[h][:, :pw] = _dot_nt(q, kp_ref[:, kcols])
        score_refs[h][:, pw:] = _dot_nt(q, kc_ref[:, kcols])

    for h in range(hq):
        kv = h // grp
        qcols = slice(h * HEAD_DIM, (h + 1) * HEAD_DIM)
        kcols = slice(kv * HEAD_DIM, (kv + 1) * HEAD_DIM)
        if has_sink:
            sink = sink_ref[step * hq + h] * LOG2E
        for rb in range(tq // rows_blk):
            r0 = rb * rows_blk
            rows = slice(r0, r0 + rows_blk)
            cols = slice(max(0, r0 + pw - (window - 1)) // HEAD_DIM * HEAD_DIM,
                         -(-(r0 + rows_blk + pw) // HEAD_DIM) * HEAD_DIM)
            s = score_refs[h][rows, cols] + bias_ref[rows, cols]
            m = jnp.max(s, axis=-1, keepdims=True)
            if has_sink:
                m = jnp.maximum(m, sink)
            p = jnp.exp2(s - m)
            l = jnp.sum(p, axis=-1, keepdims=True)
            if has_sink:
                l = l + jnp.exp2(sink - m)
            prob_refs[h][rows, cols] = p.astype(BF16)
            linv_refs[h][rows, :] = jnp.broadcast_to(1.0 / l, (rows_blk, HEAD_DIM))
            if want_lse:
                lse_ref[rows, h:h + 1] = (m + jnp.log2(l)) * LN2
        o = _dot(prob_refs[h][:, :pw], vp_ref[:, kcols]) + _dot(prob_refs[h][:, pw:], vc_ref[:, kcols])
        o_ref[:, qcols] = (o * linv_refs[h][...]).astype(o_ref.dtype)


def _banded_attention(arr, *, n_steps, hq, grp, window, q_at, k_at, v_at, o_at, o_dims, tq, pw,
                      sinks=None, want_lse=False):
    bsz, _, seq, _ = arr.shape
    tq = min(tq, seq)
    pw = min(pw, tq)
    assert pw >= window - 1 or seq <= pw, (pw, window, seq)
    assert seq % tq == 0 and tq % pw == 0
    hk = hq // grp
    ratio = tq // pw
    wq, wk = hq * HEAD_DIM, hk * HEAD_DIM
    has_sink = sinks is not None

    def spec(width, rows, at_fn, row_fn):
        def index(b, s, i, *_):
            plane, col = at_fn(s)
            return (b, plane, row_fn(i), col)
        return pl.BlockSpec((None, None, rows, width), index)

    cur = lambda i: i
    prev = lambda i: jnp.maximum(i * ratio - 1, 0)
    in_specs = [spec(wq, tq, q_at, cur), spec(wk, tq, k_at, cur), spec(wk, pw, k_at, prev),
                spec(wk, tq, v_at, cur), spec(wk, pw, v_at, prev)]
    out_specs = [spec(wq, tq, o_at, cur)]
    out_shape = [jax.ShapeDtypeStruct((bsz, o_dims[0], seq, o_dims[1]), BF16)]
    if want_lse:
        out_specs.append(pl.BlockSpec((None, None, tq, hq), lambda b, s, i: (b, s, i, 0)))
        out_shape.append(jax.ShapeDtypeStruct((bsz, n_steps, seq, hq), F32))
    kern = functools.partial(_banded_kernel, hq=hq, grp=grp, window=window, tq=tq, pw=pw,
                             has_sink=has_sink, want_lse=want_lse)
    grid = (bsz, n_steps, seq // tq)
    params = _params("parallel", "parallel", "arbitrary")
    scratch = ([pltpu.VMEM((tq, pw + tq), F32)] * hq + [pltpu.VMEM((tq, pw + tq), BF16)] * hq
               + [pltpu.VMEM((tq, HEAD_DIM), F32)] * hq + [pltpu.VMEM((tq, pw + tq), F32)])
    if has_sink:
        grid_spec = pltpu.PrefetchScalarGridSpec(num_scalar_prefetch=1, grid=grid, in_specs=in_specs,
                                                 out_specs=out_specs, scratch_shapes=scratch)
        out = pl.pallas_call(kern, grid_spec=grid_spec, out_shape=out_shape, compiler_params=params,
                             name="banded_attn_sink")(sinks.astype(F32), arr, arr, arr, arr, arr)
    else:
        out = pl.pallas_call(kern, grid=grid, in_specs=in_specs, out_specs=out_specs, out_shape=out_shape,
                             scratch_shapes=scratch, compiler_params=params, name="banded_attn")(arr, arr, arr, arr, arr)
    return out if want_lse else out[0]


def _mix_out_proj_kernel(o0_ref, o1_ref, o2_ref, l0_ref, l1_ref, l2_ref, w_ref, res_ref, g_ref, b_ref,
                         of_ref, ob_ref, t1_ref, t2_ref, x_ref):
    tm = res_ref.shape[0]
    sub = min(PROJ_SUB_ROWS, tm)

    def mix(r):
        rows = slice(r * sub, (r + 1) * sub)
        for src, dst in ((o1_ref, t1_ref), (o2_ref, t2_ref)):
            dil = src.shape[0]
            n = sub // dil
            for c in range(dil):
                for h in range(DIL_HEADS):
                    dst[h, pl.ds(r * sub + c, n, stride=dil), :] = src[
                        c, r * n:(r + 1) * n, h * HEAD_DIM:(h + 1) * HEAD_DIM].astype(F32)
        l0, l1, l2 = l0_ref[rows, :], l1_ref[rows, :], l2_ref[rows, :]
        m = jnp.maximum(jnp.maximum(l0, l1), l2)
        e0, e1, e2 = jnp.exp(l0 - m), jnp.exp(l1 - m), jnp.exp(l2 - m)
        inv = 1.0 / (e0 + e1 + e2)
        w0, w1, w2 = e0 * inv, e1 * inv, e2 * inv
        for h in range(DIL_HEADS):
            cols = slice(h * HEAD_DIM, (h + 1) * HEAD_DIM)
            mixed = (w0[:, h:h + 1] * o0_ref[rows, cols].astype(F32)
                     + w1[:, h:h + 1] * t1_ref[h, rows, :]
                     + w2[:, h:h + 1] * t2_ref[h, rows, :])
            x_ref[rows, cols] = mixed.astype(x_ref.dtype)

    def finish(r, y):
        rows = slice(r * sub, (r + 1) * sub)
        out = _residual_layer_norm(res_ref[rows, :], y, g_ref[...], b_ref[...])
        of_ref[rows, :] = out
        ob_ref[rows, :] = out.astype(BF16)

    n_sub = tm // sub
    mix(0)
    pending = None
    for r in range(n_sub):
        y = _dot(x_ref[r * sub:(r + 1) * sub, :], w_ref[...])
        if r + 1 < n_sub:
            mix(r + 1)
        if pending is not None:
            finish(*pending)
        pending = (r, y)
    finish(*pending)


def _mix_out_proj_ln(outs, lses, w, res, g, b, *, seq, tm=512):
    bsz, _, _, d = outs[0].shape
    tm = min(tm, seq)
    tiles_per_batch = seq // tm

    def o_spec(dil):
        return pl.BlockSpec((None, dil, tm // dil, d), lambda i: (i // tiles_per_batch, 0, i % tiles_per_batch, 0))

    row = pl.BlockSpec((tm, d), lambda i: (i, 0))
    lrow = pl.BlockSpec((tm, DIL_HEADS), lambda i: (i, 0))
    vec = pl.BlockSpec((1, d), lambda i: (0, 0))
    dils = [o.shape[1] for o in outs]
    assert dils[0] == 1
    return pl.pallas_call(
        _mix_out_proj_kernel,
        grid=(bsz * tiles_per_batch,),
        in_specs=[row, o_spec(dils[1]), o_spec(dils[2]), lrow, lrow, lrow,
                  pl.BlockSpec(w.shape, lambda i: (0, 0), pipeline_mode=pl.Buffered(1)), row, vec, vec],
        out_specs=[row, row],
        out_shape=[jax.ShapeDtypeStruct((bsz * seq, d), F32), jax.ShapeDtypeStruct((bsz * seq, d), BF16)],
        scratch_shapes=[pltpu.VMEM((DIL_HEADS, tm, HEAD_DIM), F32)] * 2 + [pltpu.VMEM((tm, d), BF16)],
        compiler_params=_params("parallel"),
        name="mix_out_proj_ln",
    )(outs[0].reshape(bsz * seq, d), outs[1], outs[2], *lses, w, res, g.reshape(1, d), b.reshape(1, d))


def _compress_kernel(x_ref, pe_ref, w1_ref, w2_ref, o_ref, stage_ref, first_ref, second_ref):
    seq = x_ref.shape[0]
    n = seq // CMP_STRIDE
    stage_ref[...] = x_ref[...].astype(F32)
    for j in range(CMP_STRIDE):
        piece = stage_ref[pl.ds(j, n, stride=CMP_STRIDE), :]
        lo = _dot((piece + pe_ref[j:j + 1, :]).astype(BF16), w1_ref[j * HEAD_DIM:(j + 1) * HEAD_DIM, :])
        hi_j = CMP_STRIDE + j
        hi = _dot((piece + pe_ref[hi_j:hi_j + 1, :]).astype(BF16), w1_ref[hi_j * HEAD_DIM:(hi_j + 1) * HEAD_DIM, :])
        if j == 0:
            first_ref[...] = lo
            second_ref[0:n, :] = hi
        else:
            first_ref[...] += lo
            second_ref[0:n, :] += hi
    second_ref[n:n + 8, :] = jnp.zeros((8, HEAD_DIM), F32)
    h = first_ref[...] + second_ref[1:n + 1, :]
    o_ref[...] = _dot(jax.nn.gelu(h).astype(BF16), w2_ref[...]).astype(o_ref.dtype)


def _compress(proj4, blk0, pe, w1, w2):
    bsz, _, seq, _ = proj4.shape
    n = seq // CMP_STRIDE
    const = lambda shape: pl.BlockSpec(shape, lambda b, g: (0, 0))
    return pl.pallas_call(
        _compress_kernel,
        grid=(bsz, NSA_KV_HEADS),
        in_specs=[pl.BlockSpec((None, None, seq, HEAD_DIM), lambda b, g: (b, 0, 0, blk0 + g)),
                  const((CMP_BLOCK, HEAD_DIM)), const((CMP_BLOCK * HEAD_DIM, HEAD_DIM)), const((HEAD_DIM, HEAD_DIM))],
        out_specs=pl.BlockSpec((None, n, HEAD_DIM), lambda b, g: (b * NSA_KV_HEADS + g, 0, 0)),
        out_shape=jax.ShapeDtypeStruct((bsz * NSA_KV_HEADS, n, HEAD_DIM), BF16),
        scratch_shapes=[pltpu.VMEM((seq, HEAD_DIM), F32), pltpu.VMEM((n, HEAD_DIM), F32),
                        pltpu.VMEM((n + 8, HEAD_DIM), F32)],
        compiler_params=_params("parallel", "parallel"),
        name="nsa_compress",
    )(proj4, pe, w1.astype(BF16), w2.astype(BF16))


def _nsa_kernel(q_ref, kc_ref, vc_ref, c2s_ref, ks_ref, vs_ref, ow_ref, g_ref, o_ref,
                kaug_ref, qaug_ref, s0_ref, s1_ref, p0_ref, p1_ref, a0_ref, a1_ref, m_ref, l_ref, acc_ref, ocmp_ref,
                *, tq, kt, n_blk, n_sel, n_var):
    grp = NSA_GROUP
    n_cmp = kc_ref.shape[0]
    seq = ks_ref.shape[0]
    n_halves = qaug_ref.shape[0]
    chunks_per_half = HEAD_DIM * SEL_BLOCK // kt
    q0 = pl.program_id(2) * tq

    @pl.when(pl.program_id(2) == 0)
    def _():
        def fill(r, carry):
            r0 = pl.multiple_of(r * kt, kt)
            blk_of_row = (lax.broadcasted_iota(jnp.int32, (kt, HEAD_DIM), 0) + r0) // SEL_BLOCK
            lane = lax.broadcasted_iota(jnp.int32, (kt, HEAD_DIM), 1)
            kaug_ref[pl.ds(r0, kt), 0:HEAD_DIM] = ks_ref[pl.ds(r0, kt), :]
            kaug_ref[pl.ds(r0, kt), HEAD_DIM:2 * HEAD_DIM] = jnp.where(
                lane == blk_of_row % HEAD_DIM, NEG, 0.0).astype(kaug_ref.dtype)
            return carry
        lax.fori_loop(0, seq // kt, fill, 0)

    qs = jnp.concatenate([q_ref[:, h * HEAD_DIM:(h + 1) * HEAD_DIM] for h in range(grp)], axis=0)
    qpos = q0 + lax.broadcasted_iota(jnp.int32, (tq, 1), 0)

    def select_blocks(n_vis, n_rows):
        cmp_end = lax.broadcasted_iota(jnp.int32, (tq, n_vis), 1) * CMP_STRIDE + (CMP_BLOCK - 1)
        cmp_bias = jnp.where(cmp_end <= qpos, 0.0, NEG)
        s = _dot_nt(qs, kc_ref[0:n_vis, :]).reshape(grp, tq, n_vis) + cmp_bias[None]
        m = jnp.max(s, axis=-1, keepdims=True)
        m = jnp.where(m < 0.5 * NEG, 0.0, m)
        p = jnp.exp2(s - m)
        l = jnp.sum(p, axis=-1, keepdims=True)
        p = p * (1.0 / jnp.where(l > 0.0, l, 1.0))
        ocmp_ref[...] = _dot(p.reshape(grp * tq, n_vis).astype(BF16), vc_ref[0:n_vis, :])

        p_sum = jnp.sum(p, axis=0)
        p_hi = p_sum.astype(BF16)
        p_lo = (p_sum - p_hi.astype(F32)).astype(BF16)
        c2s = c2s_ref[0:n_rows, 0:n_vis]
        imp = _dot_nt(c2s, p_hi) + _dot_nt(c2s, p_lo)

        blk = lax.broadcasted_iota(jnp.int32, (n_rows, tq), 0)
        qpos_lane = q0 + lax.broadcasted_iota(jnp.int32, (1, tq), 1)
        cur = qpos_lane // SEL_BLOCK
        forced = (blk == 0) | (blk == cur) | (blk == cur - 1)
        score = jnp.where(forced, jnp.inf, jnp.where(blk * SEL_BLOCK <= qpos_lane, imp, -1.0))
        blk_f = blk.astype(F32)
        not_sel = jnp.ones((n_rows, tq), F32)
        for _ in range(n_sel):
            best = jnp.max(score, axis=0, keepdims=True)
            first = jnp.min(jnp.where(score == best, blk_f, float(n_rows)), axis=0, keepdims=True)
            hit = blk_f == first
            not_sel = jnp.where(hit, 0.0, not_sel)
            score = jnp.where(hit, -2.0, score)
        if n_rows < n_pad:
            not_sel = jnp.concatenate([not_sel, jnp.ones((n_pad - n_rows, tq), F32)], axis=0)
        not_sel = not_sel.T
        for half in range(n_halves):
            flags = not_sel[:, half * HEAD_DIM:(half + 1) * HEAD_DIM].astype(qaug_ref.dtype)
            qaug_ref[half, :, 0:HEAD_DIM] = qs
            qaug_ref[half, :, HEAD_DIM:2 * HEAD_DIM] = jnp.concatenate([flags] * grp, axis=0)

    n_pad = c2s_ref.shape[0]
    for v in range(n_var):
        @pl.when(q0 // (seq // n_var) == v)
        def _(v=v):
            select_blocks((v + 1) * (n_cmp // n_var), (v + 1) * (n_pad // n_var))

    rows_blk = SOFTMAX_ROWS
    bufs = ((s0_ref, p0_ref, a0_ref), (s1_ref, p1_ref, a1_ref))
    m_ref[...] = jnp.full(m_ref.shape, NEG, F32)
    l_ref[...] = jnp.zeros(l_ref.shape, F32)
    acc_ref[...] = jnp.zeros(acc_ref.shape, F32)

    def scores(j, slot):
        k0 = pl.multiple_of(j * kt, kt)
        bufs[slot][0][...] = _dot_nt(qaug_ref[j // chunks_per_half], kaug_ref[pl.ds(k0, kt), :])

    def softmax(j, slot, causal):
        s_buf, p_buf, a_buf = bufs[slot]
        k0 = j * kt
        for rb in range(grp * tq // rows_blk):
            rows = slice(rb * rows_blk, (rb + 1) * rows_blk)
            sb = s_buf[rows, :]
            if causal:
                row_pos = q0 + (rb * rows_blk) % tq + lax.broadcasted_iota(jnp.int32, (rows_blk, 1), 0)
                key_pos = k0 + lax.broadcasted_iota(jnp.int32, (rows_blk, kt), 1)
                sb = jnp.where(key_pos <= row_pos, sb, NEG)
            m_prev = m_ref[rows, :]
            m_new = jnp.maximum(m_prev, jnp.max(sb, axis=-1, keepdims=True))
            alpha = jnp.exp2(m_prev - m_new)
            p = jnp.exp2(sb - jnp.concatenate([m_new] * (kt // HEAD_DIM), axis=1))
            l_ref[rows, :] = alpha * l_ref[rows, :] + jnp.sum(p, axis=-1, keepdims=True)
            m_ref[rows, :] = m_new
            a_buf[rows, :] = alpha
            p_buf[rows, :] = p.astype(p_buf.dtype)

    def weighted_values(j, slot):
        _, p_buf, a_buf = bufs[slot]
        k0 = pl.multiple_of(j * kt, kt)
        acc_ref[...] = a_buf[...] * acc_ref[...] + _dot(p_buf[...], vs_ref[pl.ds(k0, kt), :])

    n_interior = q0 // kt
    scores(0, 0)

    def interior_run(j0, count):
        for c in range(count):
            scores(j0 + c + 1, (c + 1) % 2)
            softmax(j0 + c, c % 2, causal=False)
            weighted_values(j0 + c, c % 2)

    unroll = NSA_CHUNK_UNROLL
    n_runs = n_interior // unroll

    def interior_body(r, carry):
        interior_run(r * unroll, unroll)
        return carry

    lax.fori_loop(0, n_runs, interior_body, 0)
    for count in (c for c in (4, 2) if c < unroll):
        @pl.when((n_interior % (2 * count)) // count == 1)
        def _(count=count):
            interior_run(n_interior // (2 * count) * (2 * count), count)

    @pl.when(n_interior % 2 == 0)
    def _():
        softmax(n_interior, 0, causal=True)
        weighted_values(n_interior, 0)

    @pl.when(n_interior % 2 == 1)
    def _():
        scores(n_interior, 1)
        softmax(n_interior - 1, 0, causal=False)
        weighted_values(n_interior - 1, 0)
        softmax(n_interior, 1, causal=True)
        weighted_values(n_interior, 1)

    o_slc = acc_ref[...] * (1.0 / l_ref[...])

    gates = g_ref[...]
    for h in range(grp):
        rows = slice(h * tq, (h + 1) * tq)
        cols = slice(h * HEAD_DIM, (h + 1) * HEAD_DIM)
        out = (gates[:, 3 * h:3 * h + 1] * ocmp_ref[rows, :]
               + gates[:, 3 * h + 1:3 * h + 2] * o_slc[rows]
               + gates[:, 3 * h + 2:3 * h + 3] * ow_ref[:, cols].astype(F32))
        o_ref[:, cols] = out.astype(o_ref.dtype)


def _sel_from_cmp(n_cmp_pad, n_blk_pad):
    pos = np.arange(n_cmp_pad)[:, None] * CMP_STRIDE + np.arange(CMP_BLOCK)[None, :]
    owner = pos // SEL_BLOCK
    frac = (owner[:, :, None] == np.arange(n_blk_pad)[None, None, :]).sum(axis=1) / CMP_BLOCK
    return jnp.asarray(frac.T, dtype=BF16)


def _nsa_attention(proj, kc, vc, o_win, gates, *, q_idx, ks_idx, vs_idx, tq=256, kt=512):
    bsz, seq, _ = proj.shape
    tq = min(tq, seq)
    kt = min(kt, seq)
    assert kt % tq == 0 and (HEAD_DIM * SEL_BLOCK) % kt == 0 and seq % kt == 0
    n_cmp = kc.shape[1]
    n_blk = seq // SEL_BLOCK
    n_sel = min(N_SELECT, n_blk)
    n_halves = -(-n_blk // HEAD_DIM)
    wq = NSA_GROUP * HEAD_DIM
    rows = NSA_GROUP * tq
    c2s = _sel_from_cmp(n_cmp, n_halves * HEAD_DIM)
    n_var = max(1, min(NSA_PREFIX_VARIANTS, n_cmp // (2 * HEAD_DIM)))
    assert seq % n_var == 0 and (seq // n_var) % tq == 0 and (n_halves * HEAD_DIM) % (8 * n_var) == 0
    kern = functools.partial(_nsa_kernel, tq=tq, kt=kt, n_blk=n_blk, n_sel=n_sel, n_var=n_var)
    full = lambda idx_fn: pl.BlockSpec((None, seq, HEAD_DIM), lambda b, g, i: (b, 0, idx_fn(g)))
    cmp_spec = pl.BlockSpec((None, n_cmp, HEAD_DIM), lambda b, g, i: (b * NSA_KV_HEADS + g, 0, 0))
    return pl.pallas_call(
        kern,
        grid=(bsz, NSA_KV_HEADS, seq // tq),
        in_specs=[
            pl.BlockSpec((None, tq, wq), lambda b, g, i: (b, i, q_idx(g))),
            cmp_spec, cmp_spec,
            pl.BlockSpec(c2s.shape, lambda b, g, i: (0, 0)),
            full(ks_idx), full(vs_idx),
            pl.BlockSpec((None, tq, wq), lambda b, g, i: (b, i, g)),
            pl.BlockSpec((None, tq, HEAD_DIM), lambda b, g, i: (b, i, g)),
        ],
        out_specs=pl.BlockSpec((None, tq, wq), lambda b, g, i: (b, i, g)),
        out_shape=jax.ShapeDtypeStruct((bsz, seq, NSA_KV_HEADS * wq), BF16),
        scratch_shapes=[pltpu.VMEM((seq, 2 * HEAD_DIM), BF16),
                        pltpu.VMEM((n_halves, rows, 2 * HEAD_DIM), BF16),
                        pltpu.VMEM((rows, kt), F32), pltpu.VMEM((rows, kt), F32),
                        pltpu.VMEM((rows, kt), BF16), pltpu.VMEM((rows, kt), BF16)]
                       + [pltpu.VMEM((rows, HEAD_DIM), F32)] * 6,
        compiler_params=_params("parallel", "parallel", "arbitrary"),
        name="nsa_core",
    )(proj, kc, vc, c2s, proj, proj, o_win, gates)


_BLK_AK = NSA_HEADS + SWA_HEADS
_BLK_BK = _BLK_AK + 3 * NSA_KV_HEADS
_BLK_AV = _BLK_BK + SWA_KV_HEADS
_BLK_BV = _BLK_AV + 3 * NSA_KV_HEADS


def _even_weights(w_in):
    aq, ak, av, ag, bq, bk, bv = jnp.split(w_in, np.cumsum(EVEN_WIDTHS)[:-1].tolist(), axis=-1)
    w_main = jnp.concatenate([aq, bq, ak, bk, av, bv], axis=-1).astype(BF16)
    ag = ag.reshape(D_MODEL, NSA_KV_HEADS, 3 * NSA_GROUP)
    w_gate = jnp.pad(ag, ((0, 0), (0, 0), (0, HEAD_DIM - 3 * NSA_GROUP))).reshape(D_MODEL, NSA_KV_HEADS * HEAD_DIM)
    return w_main, w_gate.astype(BF16)


def _tile_modes(blocks_per_tile, block_modes):
    modes = np.asarray(block_modes, np.int32).reshape(-1, blocks_per_tile)
    assert (modes == modes[:, :1]).all(), "a projection tile must not straddle q/k/v column groups"
    return jnp.asarray(modes[:, 0])


def _even_mixer(xf, xb, cos, sin, bsz, seq, w_in, w_o, pe_k, pe_v, ck_w1, ck_w2, cv_w1, cv_w2, sinks, g, b):
    w_main, w_gate = _even_weights(w_in)
    tn = 1024
    n_q, n_k = NSA_HEADS + SWA_HEADS, 3 * NSA_KV_HEADS + SWA_KV_HEADS
    modes = _tile_modes(tn // HEAD_DIM, [MODE_ROPE_SCALED] * n_q + [MODE_ROPE] * n_k + [MODE_PLAIN] * n_k)
    proj4 = _project(xb, w_main, modes, cos, sin, seq=seq, tn=tn)
    proj = proj4.reshape(bsz, seq, EVEN_PROJ)
    gates = _gate_project(xb, w_gate).reshape(bsz, seq, NSA_KV_HEADS * HEAD_DIM)

    kc = _compress(proj4, _BLK_AK, pe_k, ck_w1, ck_w2)
    vc = _compress(proj4, _BLK_AV, pe_v, cv_w1, cv_w2)
    o_win = _banded_attention(
        proj4, n_steps=NSA_KV_HEADS, hq=NSA_GROUP, grp=NSA_GROUP, window=NSA_WINDOW,
        q_at=lambda s: (0, s), k_at=lambda s: (0, _BLK_AK + 2 * NSA_KV_HEADS + s),
        v_at=lambda s: (0, _BLK_AV + 2 * NSA_KV_HEADS + s), o_at=lambda s: (0, s), o_dims=(1, NSA_Q),
        tq=512, pw=512).reshape(bsz, seq, NSA_Q)
    o_a = _nsa_attention(proj, kc, vc, o_win, gates, q_idx=lambda s: s,
                         ks_idx=lambda s: _BLK_AK + NSA_KV_HEADS + s, vs_idx=lambda s: _BLK_AV + NSA_KV_HEADS + s)
    swa_grp = SWA_HEADS // SWA_KV_HEADS
    o_b = _banded_attention(
        proj4, n_steps=SWA_KV_HEADS, hq=swa_grp, grp=swa_grp, window=SWA_WINDOW,
        q_at=lambda s: (0, NSA_HEADS // swa_grp + s), k_at=lambda s: (0, _BLK_BK + s),
        v_at=lambda s: (0, _BLK_BV + s), o_at=lambda s: (0, s), o_dims=(1, SWA_Q),
        tq=256, pw=128, sinks=sinks)
    m = bsz * seq
    w_o = w_o.astype(BF16)
    return _out_proj_ln([o_a.reshape(m, NSA_Q), o_b.reshape(m, SWA_Q)], [w_o[:NSA_Q], w_o[NSA_Q:]], xf, g, b)


def _odd_mixer(xf, xb, cos, sin, bsz, seq, w_in, w_o, g, b):
    tn = 1024
    modes = _tile_modes(tn // HEAD_DIM, [MODE_ROPE_SCALED] * DIL_HEADS + [MODE_ROPE] * DIL_HEADS + [MODE_PLAIN] * DIL_HEADS)
    m = bsz * seq
    width = DIL_HEADS * HEAD_DIM
    w_in = w_in.astype(BF16)
    outs, lses = [], []
    for grp_i, (window, dil) in enumerate(DIL_PATTERNS):
        qkv = _project(xb, w_in, modes, cos, sin, seq=seq, dil=dil, tn=tn, col0=grp_i * DIL_QKV, n=DIL_QKV)
        o, lse = _banded_attention(
            qkv, n_steps=dil, hq=DIL_HEADS, grp=1, window=window // dil + 1,
            q_at=lambda s: (s, 0), k_at=lambda s: (s, 1), v_at=lambda s: (s, 2), o_at=lambda s: (s, 0),
            o_dims=(dil, width), tq=256, pw=128, want_lse=True)
        outs.append(o)
        lses.append(lse.transpose(0, 2, 1, 3).reshape(m, DIL_HEADS))
    return _mix_out_proj_ln(outs, lses, w_o.astype(BF16), xf, g, b, seq=seq)


def kernel(x, positions, e_w_in, e_w_o, nsa_pe_k, nsa_pe_v, nsa_ck_w1, nsa_ck_w2, nsa_cv_w1, nsa_cv_w2, swa_sinks, o_w_in, o_w_o, ln1_g, ln1_b, mlp_w1, mlp_w2, ln2_g, ln2_b):
    bsz, seq, d = x.shape
    m = bsz * seq
    cos, sin = _rope_tables(positions)
    xf = x.reshape(m, d)
    xb = xf.astype(BF16)
    for layer in range(DEPTH):
        i = layer // 2
        if layer % 2 == 0:
            xf, xb = _even_mixer(xf, xb, cos, sin, bsz, seq, e_w_in[i], e_w_o[i], nsa_pe_k[i], nsa_pe_v[i],
                                 nsa_ck_w1[i], nsa_ck_w2[i], nsa_cv_w1[i], nsa_cv_w2[i], swa_sinks[i],
                                 ln1_g[layer], ln1_b[layer])
        else:
            xf, xb = _odd_mixer(xf, xb, cos, sin, bsz, seq, o_w_in[i], o_w_o[i], ln1_g[layer], ln1_b[layer])
        xf, xb = _mlp_ln(xf, xb, mlp_w1[layer].astype(BF16), mlp_w2[layer].astype(BF16), ln2_g[layer], ln2_b[layer])
    return xf.reshape(bsz, seq, d)
```
